```python
import jax, jax.numpy as jnp
from jax import lax
import numpy as np

D_MODEL = 2048
BATCH = 4
SEQ = 2048
DEPTH = 1

HEAD_DIM = 64
N_ATTN_HEADS = 16
ATTN_WIDTH = N_ATTN_HEADS * HEAD_DIM
LRU_WIDTH = D_MODEL - ATTN_WIDTH
N_LRU_BLOCKS = 16
LRU_BLOCK = LRU_WIDTH // N_LRU_BLOCKS
CONV_WIDTH = 4
LRU_C = 8.0
IN_COLS = 3 * ATTN_WIDTH + 2 * LRU_WIDTH
DILATED_BRANCHES = ((128, 1), (512, 4), (2048, 16))
ATTN_BLOCK = 128
N_EXPERTS = 32
TOP_K = 4
D_FF = D_MODEL
SWIGLU_LIMIT = 7.0
SWIGLU_ALPHA = 1.702
MOE_BLOCK = 128
EPS = 1e-6

kernel_name = 'hybrid_dilated_attn_rglru_moe_layer'


def rms_norm(x, g):
    xf = x.astype(jnp.float32)
    y = xf * lax.rsqrt(jnp.mean(xf * xf, axis=-1, keepdims=True) + EPS)
    return (y * g.astype(jnp.float32)).astype(x.dtype)


def alibi_slopes(n_heads):
    return jnp.asarray(2.0 ** (-8.0 * np.arange(1, n_heads + 1) / n_heads), jnp.float32)


def dilated_branch(q, k, v, slopes, window, dilation):
    B, S, H, Dh = q.shape
    n_back = window // dilation
    assert n_back <= ATTN_BLOCK
    span = dilation * ATTN_BLOCK
    S_pad = -(-S // span) * span
    L = S_pad // dilation
    nb = L // ATTN_BLOCK

    def to_blocks(t):
        t = jnp.pad(t, ((0, 0), (0, S_pad - S), (0, 0), (0, 0)))
        t = t.reshape(B, L, dilation, H, Dh).transpose(0, 2, 3, 1, 4)
        return t.reshape(B, dilation, H, nb, ATTN_BLOCK, Dh)

    def with_prev(t):
        prev = jnp.pad(t[:, :, :, :-1], ((0, 0), (0, 0), (0, 0), (1, 0), (0, 0), (0, 0)))
        return jnp.concatenate([prev, t], axis=4)

    qb = to_blocks(q)
    kk = with_prev(to_blocks(k))
    vv = with_prev(to_blocks(v))
    scores = jnp.einsum('bdhnqe,bdhnke->bdhnqk', qb, kk).astype(jnp.float32) * (Dh ** -0.5)
    qi = jnp.arange(ATTN_BLOCK)[:, None] + ATTN_BLOCK
    ki = jnp.arange(2 * ATTN_BLOCK)[None, :]
    rel = qi - ki
    key_pos = jnp.arange(nb)[:, None, None] * ATTN_BLOCK + ki[None] - ATTN_BLOCK
    mask = (rel >= 0)[None] & (rel <= n_back)[None] & (key_pos >= 0)
    bias = -slopes[:, None, None] * (rel * dilation).astype(jnp.float32)[None]
    scores = jnp.where(mask[None, None, None], scores + bias[None, None, :, None], -jnp.inf)
    m = jnp.max(scores, axis=-1, keepdims=True)
    p = jnp.exp(scores - m)
    s = jnp.sum(p, axis=-1, keepdims=True)
    o = jnp.einsum('bdhnqk,bdhnke->bdhnqe', p, vv.astype(jnp.float32)) / s
    lse = (m + jnp.log(s))[..., 0]
    o = o.reshape(B, dilation, H, L, Dh).transpose(0, 3, 1, 2, 4).reshape(B, S_pad, H, Dh)[:, :S]
    lse = lse.reshape(B, dilation, H, L).transpose(0, 3, 1, 2).reshape(B, S_pad, H)[:, :S]
    return o, lse


def dilated_mixture_attention(q, k, v):
    slopes = alibi_slopes(q.shape[2])
    outs, lses = [], []
    for window, dilation in DILATED_BRANCHES:
        o, lse = dilated_branch(q, k, v, slopes, window, dilation)
        outs.append(o)
        lses.append(lse)
    w = jax.nn.softmax(jnp.stack(lses, axis=0), axis=0)
    o = jnp.einsum('nbsh,nbshe->bshe', w, jnp.stack(outs, axis=0))
    return o.astype(q.dtype)


def causal_depthwise_conv(x, w, b):
    S = x.shape[1]
    xp = jnp.pad(x, ((0, 0), (CONV_WIDTH - 1, 0), (0, 0)))
    out = xp[:, 0:S] * w[0]
    for i in range(1, CONV_WIDTH):
        out = out + xp[:, i:i + S] * w[i]
    return out + b


def block_diag_linear(x, w, b):
    B, S, _ = x.shape
    xb = x.reshape(B, S, N_LRU_BLOCKS, LRU_BLOCK)
    y = jnp.einsum('bsnc,ncd->bsnd', xb, w.astype(jnp.float32)).reshape(B, S, LRU_WIDTH)
    return y + b.astype(jnp.float32)


def rg_lru(x, w_a, b_a, w_x, b_x, lam):
    xf = x.astype(jnp.float32)
    r = jax.nn.sigmoid(block_diag_linear(xf, w_a, b_a))
    i = jax.nn.sigmoid(block_diag_linear(xf, w_x, b_x))
    log_a = -LRU_C * r * jax.nn.softplus(-lam.astype(jnp.float32))
    a = jnp.exp(log_a)
    b = jnp.sqrt(-jnp.expm1(2.0 * log_a)) * (i * xf)

    def combine(c1, c2):
        a1, b1 = c1
        a2, b2 = c2
        return a1 * a2, a2 * b1 + b2

    _, h = lax.associative_scan(combine, (a, b), axis=1)
    return h.astype(x.dtype)


def moe_ffn(h, w_router, b_router, w_gate_up, b_gate_up, w_down, b_down):
    B, S, D = h.shape
    N = B * S
    NK = N * TOP_K
    xt = h.reshape(N, D)
    logits = xt.astype(jnp.float32) @ w_router.astype(jnp.float32) + b_router.astype(jnp.float32)
    top_val, top_idx = lax.top_k(logits, TOP_K)
    gates = jax.nn.softmax(top_val, axis=-1)
    e_flat = top_idx.reshape(NK)
    tok_flat = jnp.repeat(jnp.arange(N, dtype=jnp.int32), TOP_K)
    order = jnp.argsort(e_flat)
    e_sorted = e_flat[order]
    counts = jnp.bincount(e_flat, length=N_EXPERTS)
    start = jnp.cumsum(counts) - counts
    padded = (counts + MOE_BLOCK - 1) // MOE_BLOCK * MOE_BLOCK
    pend = jnp.cumsum(padded)
    pstart = pend - padded
    dest_sorted = (pstart[e_sorted] + jnp.arange(NK) - start[e_sorted]).astype(jnp.int32)
    P = NK + N_EXPERTS * MOE_BLOCK
    n_blocks = P // MOE_BLOCK
    tok_buf = jnp.full((P,), N, jnp.int32).at[dest_sorted].set(tok_flat[order])
    block_expert = jnp.minimum(
        jnp.searchsorted(pend, jnp.arange(n_blocks) * MOE_BLOCK, side='right'), N_EXPERTS - 1)
    x_pad = jnp.concatenate([xt, jnp.zeros((1, D), xt.dtype)], axis=0)

    def run_block(args):
        tok, e = args
        xb = x_pad[tok]
        gu = xb @ w_gate_up[e] + b_gate_up[e]
        gate = jnp.minimum(gu[:, 0::2], SWIGLU_LIMIT)
        up = jnp.clip(gu[:, 1::2], -SWIGLU_LIMIT, SWIGLU_LIMIT)
        act = gate * jax.nn.sigmoid(SWIGLU_ALPHA * gate) * (up + 1.0)
        return act @ w_down[e] + b_down[e]

    y_buf = lax.map(run_block, (tok_buf.reshape(n_blocks, MOE_BLOCK), block_expert))
    y_buf = y_buf.reshape(P, D)
    dest = jnp.zeros((NK,), jnp.int32).at[order].set(dest_sorted)
    y = y_buf[dest].reshape(N, TOP_K, D)
    out = jnp.einsum('nk,nkd->nd', gates.astype(y.dtype), y)
    return out.reshape(B, S, D)


def setup_inputs(seed: int = 0) -> dict:
    key = jax.random.key(seed)
    ks = jax.random.split(key, 22)
    f32 = jnp.float32

    def nrm(k, shape, scale):
        return jax.random.normal(k, shape, f32) * scale

    def gain(k, n):
        return 1.0 + 0.02 * jax.random.normal(k, (n,), f32)

    a_c = jax.random.uniform(ks[9], (LRU_WIDTH,), f32, 0.9, 0.999)
    a0 = a_c ** (1.0 / LRU_C)
    lru_lambda = jnp.log(a0) - jnp.log1p(-a0)
    return {
        'x': jax.random.normal(ks[0], (BATCH, SEQ, D_MODEL), f32),
        'norm_mix': gain(ks[1], D_MODEL),
        'w_in': nrm(ks[2], (D_MODEL, IN_COLS), D_MODEL ** -0.5),
        'conv_w': nrm(ks[3], (CONV_WIDTH, LRU_WIDTH), CONV_WIDTH ** -0.5),
        'conv_b': nrm(ks[4], (LRU_WIDTH,), 0.02),
        'w_a': nrm(ks[5], (N_LRU_BLOCKS, LRU_BLOCK, LRU_BLOCK), LRU_BLOCK ** -0.5),
        'b_a': nrm(ks[6], (LRU_WIDTH,), 0.02),
        'w_x': nrm(ks[7], (N_LRU_BLOCKS, LRU_BLOCK, LRU_BLOCK), LRU_BLOCK ** -0.5),
        'b_x': nrm(ks[8], (LRU_WIDTH,), 0.02),
        'lru_lambda': lru_lambda,
        'attn_out_norm': gain(ks[10], ATTN_WIDTH),
        'lru_out_norm': gain(ks[11], LRU_WIDTH),
        'w_out': nrm(ks[12], (D_MODEL, D_MODEL), D_MODEL ** -0.5),
        'norm_ffn': gain(ks[13], D_MODEL),
        'w_router': nrm(ks[14], (D_MODEL, N_EXPERTS), D_MODEL ** -0.5),
        'b_router': nrm(ks[15], (N_EXPERTS,), 0.01),
        'w_gate_up': nrm(ks[16], (N_EXPERTS, D_MODEL, 2 * D_FF), D_MODEL ** -0.5),
        'b_gate_up': nrm(ks[17], (N_EXPERTS, 2 * D_FF), 0.02),
        'w_down': nrm(ks[18], (N_EXPERTS, D_FF, D_MODEL), D_FF ** -0.5),
        'b_down': nrm(ks[19], (N_EXPERTS, D_MODEL), 0.02),
        'norm_final': gain(ks[20], D_MODEL),
    }


def reference(x, norm_mix, w_in, conv_w, conv_b, w_a, b_a, w_x, b_x, lru_lambda,
              attn_out_norm, lru_out_norm, w_out, norm_ffn, w_router, b_router,
              w_gate_up, b_gate_up, w_down, b_down, norm_final):
    B, S, _ = x.shape
    for _layer in range(DEPTH):
        h = rms_norm(x, norm_mix)
        proj = h @ w_in
        q, k, v, xr, gr = jnp.split(
            proj, [ATTN_WIDTH, 2 * ATTN_WIDTH, 3 * ATTN_WIDTH, 3 * ATTN_WIDTH + LRU_WIDTH], axis=-1)
        q = q.reshape(B, S, N_ATTN_HEADS, HEAD_DIM)
        k = k.reshape(B, S, N_ATTN_HEADS, HEAD_DIM)
        v = v.reshape(B, S, N_ATTN_HEADS, HEAD_DIM)
        attn = dilated_mixture_attention(q, k, v).reshape(B, S, ATTN_WIDTH)
        xr = causal_depthwise_conv(xr, conv_w, conv_b)
        rec = rg_lru(xr, w_a, b_a, w_x, b_x, lru_lambda) * jax.nn.gelu(gr)
        mixed = jnp.concatenate([rms_norm(attn, attn_out_norm), rms_norm(rec, lru_out_norm)], axis=-1)
        x = x + mixed @ w_out
        x = x + moe_ffn(rms_norm(x, norm_ffn), w_router, b_router, w_gate_up, b_gate_up, w_down, b_down)
    return rms_norm(x, norm_final)
```

```python
import functools

import numpy as np
import jax
import jax.numpy as jnp
from jax import lax
from jax.experimental import pallas as pl
from jax.experimental.pallas import tpu as pltpu

F32 = jnp.float32
BF16 = jnp.bfloat16

D_MODEL = 2048
HEAD_DIM = 64
N_HEADS = 16
ATTN_WIDTH = N_HEADS * HEAD_DIM
LRU_WIDTH = D_MODEL - ATTN_WIDTH
LRU_BLOCK = 64
CONV_WIDTH = 4
LRU_C = 8.0
IN_COLS = 3 * ATTN_WIDTH + 2 * LRU_WIDTH
DILATIONS = (1, 4, 16)
ATTN_BLOCK = 128
N_EXPERTS = 32
TOP_K = 4
D_FF = D_MODEL
SWIGLU_LIMIT = 7.0
SWIGLU_ALPHA = 1.702
MOE_BLOCK = 128
EPS = 1e-6

LANES = 128
SUBLANES = 8
VMEM_LIMIT = 56 * 1024 * 1024


def _params(sem, vmem=VMEM_LIMIT):
    return pltpu.CompilerParams(dimension_semantics=sem, vmem_limit_bytes=vmem)


def _rms(xf, g):
    return xf * lax.rsqrt(jnp.mean(xf * xf, axis=-1, keepdims=True) + EPS) * g


def _in_proj_body(x_ref, g_ref, w_ref, o_ref, h_ref):
    @pl.when(pl.program_id(1) == 0)
    def _():
        h_ref[...] = _rms(x_ref[...], g_ref[...]).astype(BF16)

    o_ref[...] = jnp.dot(h_ref[...], w_ref[...].astype(BF16), preferred_element_type=F32)


def _in_proj(x2, g, w_in, tm=1024, tn=512):
    n, d = x2.shape
    cols = w_in.shape[1]
    return pl.pallas_call(
        _in_proj_body,
        grid=(n // tm, cols // tn),
        in_specs=[
            pl.BlockSpec((tm, d), lambda i, j: (i, 0)),
            pl.BlockSpec((1, d), lambda i, j: (0, 0)),
            pl.BlockSpec((d, tn), lambda i, j: (0, j)),
        ],
        out_specs=pl.BlockSpec((tm, tn), lambda i, j: (i, j)),
        out_shape=jax.ShapeDtypeStruct((n, cols), F32),
        scratch_shapes=[pltpu.VMEM((tm, d), BF16)],
        compiler_params=_params(("parallel", "arbitrary")),
        name="in_proj",
    )(x2, g.reshape(1, d), w_in)


def _attention_body(slope_ref, q_ref, k_ref, v_ref, o_ref, qh, kh, vh, acc, mx):
    seq = q_ref.shape[1]
    pair = pl.program_id(1)
    lane = lax.broadcasted_iota(jnp.int32, (seq, LANES), 1)
    qi = lax.broadcasted_iota(jnp.int32, (ATTN_BLOCK, 2 * ATTN_BLOCK), 0)
    ki = lax.broadcasted_iota(jnp.int32, (ATTN_BLOCK, 2 * ATTN_BLOCK), 1)
    rel2 = qi + ATTN_BLOCK - ki
    ok2 = (rel2 >= 0) & (rel2 <= ATTN_BLOCK)
    rel1 = (qi - ki)[:, :ATTN_BLOCK]
    ok1 = rel1 >= 0

    def unit(br, d, q_start, k_start, nk, bias):
        if d == 1:
            qs, ks = pl.ds(q_start, ATTN_BLOCK), pl.ds(k_start, nk)
        else:
            qs = pl.ds(q_start, ATTN_BLOCK, stride=d)
            ks = pl.ds(k_start, nk, stride=d)
        q = qh[qs, :].astype(BF16)
        k = kh[ks, :].astype(BF16)
        v = vh[ks, :].astype(BF16)
        s = lax.dot_general(q, k, (((1,), (1,)), ((), ())), preferred_element_type=F32) + bias
        m = jnp.max(s, axis=1, keepdims=True)
        p = jnp.exp(s - m).astype(BF16)
        acc[br, qs, :] = jnp.dot(p, v, preferred_element_type=F32)
        mx[br, qs, :] = jnp.broadcast_to(m, (ATTN_BLOCK, LANES))

    for hh in range(LANES // HEAD_DIM):
        lo = hh * HEAD_DIM
        slope = slope_ref[pair * (LANES // HEAD_DIM) + hh]
        qh[...] = q_ref[0, :, lo:lo + HEAD_DIM] * (HEAD_DIM ** -0.5)
        kh[...] = k_ref[0, :, lo:lo + HEAD_DIM]
        vfull = v_ref[0]
        if lo:
            vfull = pltpu.roll(vfull, LANES - lo, axis=1)
        vh[...] = jnp.where(lane < HEAD_DIM, vfull, 1.0)

        for br, d in enumerate(DILATIONS):
            nb = seq // (ATTN_BLOCK * d)
            bias1 = jnp.where(ok1, -slope * (rel1 * d).astype(F32), -jnp.inf)
            bias2 = jnp.where(ok2, -slope * (rel2 * d).astype(F32), -jnp.inf)

            def first(r, carry, br=br, d=d, bias1=bias1):
                unit(br, d, r, r, ATTN_BLOCK, bias1)
                return carry

            lax.fori_loop(0, d, first, 0)

            if nb > 1:
                def later(u, carry, br=br, d=d, nb=nb, bias2=bias2):
                    r = u // (nb - 1)
                    n = u % (nb - 1) + 1
                    start = r + n * ATTN_BLOCK * d
                    unit(br, d, start, start - ATTN_BLOCK * d, 2 * ATTN_BLOCK, bias2)
                    return carry

                lax.fori_loop(0, d * (nb - 1), later, 0)

        rows = 256

        def merge(c, carry, lo=lo):
            sl = pl.ds(pl.multiple_of(c * rows, rows), rows)
            m0, m1, m2 = mx[0, sl, :], mx[1, sl, :], mx[2, sl, :]
            mt = jnp.maximum(jnp.maximum(m0, m1), m2)
            tot = (jnp.exp(m0 - mt) * acc[0, sl, :] + jnp.exp(m1 - mt) * acc[1, sl, :]
                   + jnp.exp(m2 - mt) * acc[2, sl, :])
            den = pltpu.roll(tot, HEAD_DIM, axis=1)
            out = tot / den
            o_ref[0, sl, lo:lo + HEAD_DIM] = out[:, :HEAD_DIM]
            return carry

        lax.fori_loop(0, seq // rows, merge, 0)


def _attention(proj3, slopes):
    b, seq, _ = proj3.shape
    npair = ATTN_WIDTH // LANES
    blk = (1, seq, LANES)
    return pl.pallas_call(
        _attention_body,
        grid_spec=pltpu.PrefetchScalarGridSpec(
            num_scalar_prefetch=1,
            grid=(b, npair),
            in_specs=[
                pl.BlockSpec(blk, lambda i, j, s: (i, 0, j)),
                pl.BlockSpec(blk, lambda i, j, s: (i, 0, npair + j)),
                pl.BlockSpec(blk, lambda i, j, s: (i, 0, 2 * npair + j)),
            ],
            out_specs=pl.BlockSpec(blk, lambda i, j, s: (i, 0, j)),
            scratch_shapes=[
                pltpu.VMEM((seq, HEAD_DIM), F32),
                pltpu.VMEM((seq, HEAD_DIM), F32),
                pltpu.VMEM((seq, LANES), F32),
                pltpu.VMEM((len(DILATIONS), seq, LANES), F32),
                pltpu.VMEM((len(DILATIONS), seq, LANES), F32),
            ],
        ),
        out_shape=jax.ShapeDtypeStruct((b, seq, ATTN_WIDTH), F32),
        compiler_params=_params(("parallel", "parallel")),
        name="attention",
    )(slopes, proj3, proj3, proj3)


def _rglru_body(xr_ref, gr_ref, cw_ref, cb_ref, w_ref, ba_ref, bx_ref, lam_ref, o_ref,
                xp, a_s, b_s):
    seq = xr_ref.shape[1]
    pad = SUBLANES
    xp[0:pad, :] = jnp.zeros((pad, LANES), F32)
    xp[pad:pad + seq, :] = xr_ref[0]
    lam = lam_ref[...]
    sp = jnp.maximum(-lam, 0.0) + jnp.log(1.0 + jnp.exp(-jnp.abs(lam)))
    w_hi = w_ref[0].astype(BF16)
    w_lo = (w_ref[0] - w_hi.astype(F32)).astype(BF16)
    rows = 256

    def gates(c, carry):
        base = pl.multiple_of(c * rows, rows)
        xc = cb_ref[...] + cw_ref[0:1, :] * xp[pl.ds(base + pad - 3, rows), :]
        for i in range(1, CONV_WIDTH):
            xc = xc + cw_ref[i:i + 1, :] * xp[pl.ds(base + pad - 3 + i, rows), :]
        hi = xc.astype(BF16)
        lo = (xc - hi.astype(F32)).astype(BF16)
        pre = (jnp.dot(hi, w_hi, preferred_element_type=F32)
               + jnp.dot(lo, w_hi, preferred_element_type=F32)
               + jnp.dot(hi, w_lo, preferred_element_type=F32))
        r = jax.nn.sigmoid(pre[:, :LANES] + ba_ref[...])
        ig = jax.nn.sigmoid(pre[:, LANES:] + bx_ref[...])
        log_a = -LRU_C * r * sp
        a = jnp.exp(log_a)
        t = jnp.tanh(log_a)
        b = jnp.sqrt(-2.0 * t / (1.0 - t)) * (ig * xc)
        a_s[pl.ds(base, rows), :] = a
        b_s[pl.ds(base, rows), :] = b
        return carry

    lax.fori_loop(0, seq // rows, gates, 0)

    row = lax.broadcasted_iota(jnp.int32, (SUBLANES, LANES), 0)

    def scan(c, h_prev):
        sl = pl.ds(pl.multiple_of(c * SUBLANES, SUBLANES), SUBLANES)
        a = a_s[sl, :]
        b = b_s[sl, :]
        for s in (1, 2, 4):
            keep = row >= s
            a_sh = jnp.where(keep, pltpu.roll(a, s, axis=0), 1.0)
            b_sh = jnp.where(keep, pltpu.roll(b, s, axis=0), 0.0)
            b = a * b_sh + b
            a = a * a_sh
        h = a * h_prev + b
        o_ref[0, sl, :] = h * jax.nn.gelu(gr_ref[0, sl, :])
        return jnp.broadcast_to(h[SUBLANES - 1:SUBLANES, :], (SUBLANES, LANES))

    lax.fori_loop(0, seq // SUBLANES, scan, jnp.zeros((SUBLANES, LANES), F32), unroll=8)


def _rglru(proj3, conv_w, conv_b, w_bd, b_a, b_x, lam):
    b, seq, _ = proj3.shape
    nt = LRU_WIDTH // LANES
    xr0 = 3 * ATTN_WIDTH // LANES
    gr0 = xr0 + nt
    blk = (1, seq, LANES)
    vec = lambda: pl.BlockSpec((1, LANES), lambda i, j: (0, j))
    return pl.pallas_call(
        _rglru_body,
        grid=(b, nt),
        in_specs=[
            pl.BlockSpec(blk, lambda i, j: (i, 0, xr0 + j)),
            pl.BlockSpec(blk, lambda i, j: (i, 0, gr0 + j)),
            pl.BlockSpec((CONV_WIDTH, LANES), lambda i, j: (0, j)),
            vec(),
            pl.BlockSpec((1, LANES, 2 * LANES), lambda i, j: (j, 0, 0)),
            vec(), vec(), vec(),
        ],
        out_specs=pl.BlockSpec(blk, lambda i, j: (i, 0, j)),
        out_shape=jax.ShapeDtypeStruct((b, seq, LRU_WIDTH), F32),
        scratch_shapes=[
            pltpu.VMEM((seq + SUBLANES, LANES), F32),
            pltpu.VMEM((seq, LANES), F32),
            pltpu.VMEM((seq, LANES), F32),
        ],
        compiler_params=_params(("parallel", "parallel")),
        name="rglru",
    )(proj3, proj3, conv_w, conv_b.reshape(1, -1), w_bd, b_a.reshape(1, -1),
      b_x.reshape(1, -1), lam.reshape(1, -1))


def _out_proj_body(at_ref, rc_ref, ga_ref, gr_ref, x_ref, w_ref, o_ref, h_ref):
    wa = at_ref.shape[1]

    @pl.when(pl.program_id(1) == 0)
    def _():
        h_ref[:, :wa] = _rms(at_ref[...], ga_ref[...]).astype(BF16)
        h_ref[:, wa:] = _rms(rc_ref[...], gr_ref[...]).astype(BF16)

    o_ref[...] = x_ref[...] + jnp.dot(h_ref[...], w_ref[...].astype(BF16),
                                      preferred_element_type=F32)


def _out_proj(attn2, rec2, g_attn, g_rec, x2, w_out, tm=512, tn=1024):
    n, d = x2.shape
    wa, wr = attn2.shape[1], rec2.shape[1]
    return pl.pallas_call(
        _out_proj_body,
        grid=(n // tm, d // tn),
        in_specs=[
            pl.BlockSpec((tm, wa), lambda i, j: (i, 0)),
            pl.BlockSpec((tm, wr), lambda i, j: (i, 0)),
            pl.BlockSpec((1, wa), lambda i, j: (0, 0)),
            pl.BlockSpec((1, wr), lambda i, j: (0, 0)),
            pl.BlockSpec((tm, tn), lambda i, j: (i, j)),
            pl.BlockSpec((d, tn), lambda i, j: (0, j)),
        ],
        out_specs=pl.BlockSpec((tm, tn), lambda i, j: (i, j)),
        out_shape=jax.ShapeDtypeStruct((n, d), F32),
        scratch_shapes=[pltpu.VMEM((tm, d), BF16)],
        compiler_params=_params(("parallel", "arbitrary")),
        name="out_proj",
    )(attn2, rec2, g_attn.reshape(1, wa), g_rec.reshape(1, wr), x2, w_out)


def _router_body(x_ref, g_ref, wt_ref, b_ref, xn_ref, idx_ref, gate_ref, rank_ref, cnt_ref,
                 base):
    tm = x_ref.shape[0]

    @pl.when(pl.program_id(0) == 0)
    def _():
        base[...] = jnp.zeros_like(base)

    xn = _rms(x_ref[...], g_ref[...])
    xn_ref[...] = xn
    x_hi = xn.astype(BF16)
    x_lo = (xn - x_hi.astype(F32)).astype(BF16)
    w = wt_ref[...]
    w_hi = w.astype(BF16)
    w_lo = (w - w_hi.astype(F32)).astype(BF16)
    nt = (((1,), (1,)), ((), ()))
    logits = (lax.dot_general(w_hi, x_hi, nt, preferred_element_type=F32)
              + lax.dot_general(w_hi, x_lo, nt, preferred_element_type=F32)
              + lax.dot_general(w_lo, x_hi, nt, preferred_element_type=F32)
              + b_ref[...])

    eid = lax.broadcasted_iota(jnp.int32, (N_EXPERTS, tm), 0)
    work = logits
    vals, hots = [], []
    for _ in range(TOP_K):
        best = jnp.max(work, axis=0, keepdims=True)
        pick = jnp.min(jnp.where(work == best, eid, N_EXPERTS), axis=0, keepdims=True)
        hot = eid == pick
        vals.append(best)
        hots.append(hot)
        work = jnp.where(hot, -jnp.inf, work)
        idx_ref[len(vals) - 1:len(vals), :] = pick

    ex = [jnp.exp(v - vals[0]) for v in vals]
    den = ex[0] + ex[1] + ex[2] + ex[3]
    for k in range(TOP_K):
        gate_ref[k:k + 1, :] = ex[k] / den

    chosen = (hots[0] | hots[1] | hots[2] | hots[3])
    si = lax.broadcasted_iota(jnp.int32, (tm, tm), 0)
    ti = lax.broadcasted_iota(jnp.int32, (tm, tm), 1)
    before = (si < ti).astype(BF16)
    prefix = jnp.dot(chosen.astype(BF16), before, preferred_element_type=F32)
    slot = base[:, 0:1] + prefix
    for k in range(TOP_K):
        rank_ref[k:k + 1, :] = jnp.sum(jnp.where(hots[k], slot, 0.0), axis=0,
                                       keepdims=True).astype(jnp.int32)
    base[...] = base[...] + jnp.sum(chosen.astype(F32), axis=1, keepdims=True)
    cnt_ref[...] = base[...].astype(jnp.int32)


def _router(x2, g, w_router_t, b_router, tm=256):
    n, d = x2.shape
    row = lambda: pl.BlockSpec((TOP_K, tm), lambda i: (0, i))
    return pl.pallas_call(
        _router_body,
        grid=(n // tm,),
        in_specs=[
            pl.BlockSpec((tm, d), lambda i: (i, 0)),
            pl.BlockSpec((1, d), lambda i: (0, 0)),
            pl.BlockSpec((N_EXPERTS, d), lambda i: (0, 0)),
            pl.BlockSpec((N_EXPERTS, 1), lambda i: (0, 0)),
        ],
        out_specs=[
            pl.BlockSpec((tm, d), lambda i: (i, 0)),
            row(), row(), row(),
            pl.BlockSpec((N_EXPERTS, LANES), lambda i: (0, 0)),
        ],
        out_shape=[
            jax.ShapeDtypeStruct((n, d), F32),
            jax.ShapeDtypeStruct((TOP_K, n), jnp.int32),
            jax.ShapeDtypeStruct((TOP_K, n), F32),
            jax.ShapeDtypeStruct((TOP_K, n), jnp.int32),
            jax.ShapeDtypeStruct((N_EXPERTS, LANES), jnp.int32),
        ],
        scratch_shapes=[pltpu.VMEM((N_EXPERTS, LANES), F32)],
        compiler_params=_params(("arbitrary",)),
        name="router",
    )(x2, g.reshape(1, d), w_router_t, b_router.reshape(N_EXPERTS, 1))


def _scatter_body(dest_ref, pad_ref, xn_ref, o_ref, zrow, sem, *, n_tokens, pad_per_step):
    tm = xn_ref.shape[0]
    i = pl.program_id(0)

    @pl.when(i == 0)
    def _():
        zrow[...] = jnp.zeros_like(zrow)

    def row_copy(t, k):
        d = dest_ref[k * n_tokens + i * tm + t]
        return pltpu.make_async_copy(xn_ref.at[pl.ds(t, 1), :], o_ref.at[pl.ds(d, 1), :], sem)

    def pad_copy(q):
        d = pad_ref[i * pad_per_step + q]
        return pltpu.make_async_copy(zrow.at[pl.ds(0, 1), :], o_ref.at[pl.ds(d, 1), :], sem)

    def issue(t, c):
        for k in range(TOP_K):
            row_copy(t, k).start()
        return c

    lax.fori_loop(0, tm, issue, 0)

    def issue_pad(q, c):
        pad_copy(q).start()
        return c

    lax.fori_loop(0, pad_per_step, issue_pad, 0)

    def drain(t, c):
        for k in range(TOP_K):
            row_copy(t, k).wait()
        return c

    lax.fori_loop(0, tm, drain, 0)

    def drain_pad(q, c):
        pad_copy(q).wait()
        return c

    lax.fori_loop(0, pad_per_step, drain_pad, 0)


def _scatter(dest_flat, pad_dest, xn, n_rows, tm=256):
    n, d = xn.shape
    steps = n // tm
    pad_per_step = pad_dest.shape[0] // steps
    return pl.pallas_call(
        functools.partial(_scatter_body, n_tokens=n, pad_per_step=pad_per_step),
        grid_spec=pltpu.PrefetchScalarGridSpec(
            num_scalar_prefetch=2,
            grid=(steps,),
            in_specs=[pl.BlockSpec((tm, d), lambda i, *_: (i, 0))],
            out_specs=pl.BlockSpec(memory_space=pl.ANY),
            scratch_shapes=[pltpu.VMEM((SUBLANES, d), F32), pltpu.SemaphoreType.DMA(())],
        ),
        out_shape=jax.ShapeDtypeStruct((n_rows, d), F32),
        compiler_params=_params(("arbitrary",)),
        name="scatter",
    )(dest_flat, pad_dest, xn)


def _gate_up_body(meta_ref, x_ref, w_ref, b_ref, o_ref, wbf):
    i = pl.program_id(1)
    n_valid = meta_ref[meta_ref.shape[0] - 1]
    ic = jnp.minimum(i, n_valid - 1)
    changed = (i == 0) | (meta_ref[ic] != meta_ref[jnp.maximum(ic - 1, 0)])
    kdim, tn = wbf.shape
    rows = 256

    @pl.when(changed & (i < n_valid))
    def _():
        def cast(c, carry):
            sl = pl.ds(pl.multiple_of(c * rows, rows), rows)
            wbf[sl, :] = w_ref[0, sl, :].astype(BF16)
            return carry

        lax.fori_loop(0, kdim // rows, cast, 0)

    @pl.when(i < n_valid)
    def _():
        x = x_ref[...].astype(BF16)
        even = lax.broadcasted_iota(jnp.int32, (x.shape[0], LANES), 1) % 2 == 0
        for c in range(tn // (2 * LANES)):
            c0 = c * 2 * LANES
            gu = jnp.dot(x, wbf[:, c0:c0 + 2 * LANES], preferred_element_type=F32)
            gu = gu + b_ref[0, :, c0:c0 + 2 * LANES]
            v1, v2 = gu[:, :LANES], gu[:, LANES:]
            gate = jnp.where(even, v1, pltpu.roll(v2, 1, axis=1))
            up = jnp.where(even, pltpu.roll(v1, LANES - 1, axis=1), v2)
            gate = jnp.minimum(gate, SWIGLU_LIMIT)
            up = jnp.clip(up, -SWIGLU_LIMIT, SWIGLU_LIMIT)
            act = gate * jax.nn.sigmoid(SWIGLU_ALPHA * gate) * (up + 1.0)
            o_ref[:, c * LANES:(c + 1) * LANES] = act.astype(BF16)

    @pl.when(i >= n_valid)
    def _():
        o_ref[...] = jnp.zeros_like(o_ref)


def _gate_up(meta, x_sorted, w_gate_up, b_gate_up, tn=1024):
    p, d = x_sorted.shape
    nblk = p // MOE_BLOCK
    cols = w_gate_up.shape[2]

    def blk(i, m):
        return jnp.minimum(i, m[nblk] - 1)

    return pl.pallas_call(
        _gate_up_body,
        grid_spec=pltpu.PrefetchScalarGridSpec(
            num_scalar_prefetch=1,
            grid=(cols // tn, nblk),
            in_specs=[
                pl.BlockSpec((MOE_BLOCK, d), lambda j, i, m: (blk(i, m), 0)),
                pl.BlockSpec((1, d, tn), lambda j, i, m: (m[blk(i, m)], 0, j)),
                pl.BlockSpec((1, 1, tn), lambda j, i, m: (m[blk(i, m)], 0, j)),
            ],
            out_specs=pl.BlockSpec((MOE_BLOCK, tn // 2), lambda j, i, m: (i, j)),
            scratch_shapes=[pltpu.VMEM((d, tn), BF16)],
        ),
        out_shape=jax.ShapeDtypeStruct((p, cols // 2), BF16),
        compiler_params=_params(("arbitrary", "arbitrary")),
        name="gate_up",
    )(meta, x_sorted, w_gate_up, b_gate_up.reshape(N_EXPERTS, 1, cols))


def _down_body(meta_ref, a_ref, w_ref, b_ref, o_ref, wbf, tmp):
    i = pl.program_id(1)
    n_valid = meta_ref[meta_ref.shape[0] - 1]
    ic = jnp.minimum(i, n_valid - 1)
    changed = (i == 0) | (meta_ref[ic] != meta_ref[jnp.maximum(ic - 1, 0)])
    kdim = wbf.shape[0]
    half = LANES // 2

    @pl.when(changed & (i < n_valid))
    def _():
        def permute(g, carry):
            base = pl.multiple_of(g * LANES, LANES)
            for c in range(tmp.shape[0]):
                cs = slice(c * LANES, (c + 1) * LANES)
                tmp[c, pl.ds(0, half, stride=2), :] = w_ref[0, pl.ds(base, half), cs]
                tmp[c, pl.ds(1, half, stride=2), :] = w_ref[0, pl.ds(base + half, half), cs]
                wbf[pl.ds(base, LANES), cs] = tmp[c].astype(BF16)
            return carry

        lax.fori_loop(0, kdim // LANES, permute, 0)

    @pl.when(i < n_valid)
    def _():
        o_ref[...] = jnp.dot(a_ref[...], wbf[...], preferred_element_type=F32) + b_ref[0]

    @pl.when(i >= n_valid)
    def _():
        o_ref[...] = jnp.zeros_like(o_ref)


def _down(meta, act, w_down, b_down, tn=1024):
    p, f = act.shape
    nblk = p // MOE_BLOCK
    d = w_down.shape[2]

    def blk(i, m):
        return jnp.minimum(i, m[nblk] - 1)

    return pl.pallas_call(
        _down_body,
        grid_spec=pltpu.PrefetchScalarGridSpec(
            num_scalar_prefetch=1,
            grid=(d // tn, nblk),
            in_specs=[
                pl.BlockSpec((MOE_BLOCK, f), lambda j, i, m: (blk(i, m), 0)),
                pl.BlockSpec((1, f, tn), lambda j, i, m: (m[blk(i, m)], 0, j)),
                pl.BlockSpec((1, 1, tn), lambda j, i, m: (m[blk(i, m)], 0, j)),
            ],
            out_specs=pl.BlockSpec((MOE_BLOCK, tn), lambda j, i, m: (i, j)),
            scratch_shapes=[pltpu.VMEM((f, tn), BF16),
                            pltpu.VMEM((tn // LANES, LANES, LANES), F32)],
        ),
        out_shape=jax.ShapeDtypeStruct((p, d), F32),
        compiler_params=_params(("arbitrary", "arbitrary")),
        name="down",
    )(meta, act, w_down, b_down.reshape(N_EXPERTS, 1, d))


def _combine_body(dest_ref, y_ref, x_ref, gate_ref, g_ref, o_ref, ybuf, sem, *, n_tokens):
    tm = x_ref.shape[0]
    i = pl.program_id(0)

    def row_copy(t, k):
        d = dest_ref[k * n_tokens + i * tm + t]
        return pltpu.make_async_copy(y_ref.at[pl.ds(d, 1), :], ybuf.at[k, pl.ds(t, 1), :], sem)

    def issue(t, c):
        for k in range(TOP_K):
            row_copy(t, k).start()
        return c

    lax.fori_loop(0, tm, issue, 0)

    def drain(t, c):
        for k in range(TOP_K):
            row_copy(t, k).wait()
        return c

    lax.fori_loop(0, tm, drain, 0)

    acc = x_ref[...]
    for k in range(TOP_K):
        acc = acc + gate_ref[:, k:k + 1] * ybuf[k]
    o_ref[...] = _rms(acc, g_ref[...])


def _combine(dest_flat, y_buf, x2, gates, g, tm=256):
    n, d = x2.shape
    return pl.pallas_call(
        functools.partial(_combine_body, n_tokens=n),
        grid_spec=pltpu.PrefetchScalarGridSpec(
            num_scalar_prefetch=1,
            grid=(n // tm,),
            in_specs=[
                pl.BlockSpec(memory_space=pl.ANY),
                pl.BlockSpec((tm, d), lambda i, *_: (i, 0)),
                pl.BlockSpec((tm, TOP_K), lambda i, *_: (i, 0)),
                pl.BlockSpec((1, d), lambda i, *_: (0, 0)),
            ],
            out_specs=pl.BlockSpec((tm, d), lambda i, *_: (i, 0)),
            scratch_shapes=[pltpu.VMEM((TOP_K, tm, d), F32), pltpu.SemaphoreType.DMA(())],
        ),
        out_shape=jax.ShapeDtypeStruct((n, d), F32),
        compiler_params=_params(("arbitrary",)),
        name="combine",
    )(dest_flat, y_buf, x2, gates, g.reshape(1, d))


def _routing_tables(counts, idx_t, rank_t, n_rows):
    nblk = n_rows // MOE_BLOCK
    padded = (counts + MOE_BLOCK - 1) // MOE_BLOCK * MOE_BLOCK
    pend = jnp.cumsum(padded)
    pstart = pend - padded
    dest = (pstart[idx_t] + rank_t).astype(jnp.int32)
    block_expert = jnp.minimum(
        jnp.searchsorted(pend, jnp.arange(nblk, dtype=jnp.int32) * MOE_BLOCK, side='right'),
        N_EXPERTS - 1).astype(jnp.int32)
    n_valid = (pend[-1] // MOE_BLOCK).astype(jnp.int32)
    meta = jnp.concatenate([block_expert, n_valid[None]])
    n_pad = n_rows - idx_t.size
    gap = padded - counts
    gap_end = jnp.cumsum(gap)
    q = jnp.arange(n_pad, dtype=jnp.int32)
    e = jnp.searchsorted(gap_end, q, side='right')
    ec = jnp.minimum(e, N_EXPERTS - 1)
    inside = (pstart + counts)[ec] + q - (gap_end - gap)[ec]
    tail = pend[-1] + q - gap_end[-1]
    pad_dest = jnp.where(e < N_EXPERTS, inside, tail).astype(jnp.int32)
    return dest.reshape(-1), meta, pad_dest


def kernel(x, norm_mix, w_in, conv_w, conv_b, w_a, b_a, w_x, b_x, lru_lambda, attn_out_norm, lru_out_norm, w_out, norm_ffn, w_router, b_router, w_gate_up, b_gate_up, w_down, b_down, norm_final):
    b, seq, d = x.shape
    n = b * seq
    x2 = x.reshape(n, d)

    proj = _in_proj(x2, norm_mix, w_in)
    proj3 = proj.reshape(b, seq, IN_COLS)

    slopes = jnp.asarray(2.0 ** (-8.0 * np.arange(1, N_HEADS + 1) / N_HEADS), F32)
    attn = _attention(proj3, slopes)

    def pair_blocks(w):
        w4 = w.reshape(-1, 2, LRU_BLOCK, LRU_BLOCK)
        z = jnp.zeros_like(w4[:, 0])
        top = jnp.concatenate([w4[:, 0], z], axis=2)
        bot = jnp.concatenate([z, w4[:, 1]], axis=2)
        return jnp.concatenate([top, bot], axis=1)

    w_bd = jnp.concatenate([pair_blocks(w_a), pair_blocks(w_x)], axis=2)
    rec = _rglru(proj3, conv_w, conv_b, w_bd, b_a, b_x, lru_lambda)

    x_mid = _out_proj(attn.reshape(n, ATTN_WIDTH), rec.reshape(n, LRU_WIDTH),
                      attn_out_norm, lru_out_norm, x2, w_out)

    xn, idx_t, gate_t, rank_t, cnt = _router(x_mid, norm_ffn, w_router.T, b_router)
    n_rows = n * TOP_K + N_EXPERTS * MOE_BLOCK
    dest_flat, meta, pad_dest = _routing_tables(cnt[:, 0], idx_t, rank_t, n_rows)

    x_sorted = _scatter(dest_flat, pad_dest, xn, n_rows)
    act = _gate_up(meta, x_sorted, w_gate_up, b_gate_up)
    y_buf = _down(meta, act, w_down, b_down)
    out = _combine(dest_flat, y_buf, x_mid, gate_t.T, norm_final)
    return out.reshape(b, seq, d)
```

```python
import functools

import numpy as np
import jax
import jax.numpy as jnp
from jax import lax
from jax.experimental import pallas as pl
from jax.experimental.pallas import tpu as pltpu

F32 = jnp.float32
BF16 = jnp.bfloat16

D_MODEL = 2048
HEAD_DIM = 64
N_HEADS = 16
ATTN_WIDTH = N_HEADS * HEAD_DIM
LRU_WIDTH = D_MODEL - ATTN_WIDTH
LRU_BLOCK = 64
CONV_WIDTH = 4
LRU_C = 8.0
IN_COLS = 3 * ATTN_WIDTH + 2 * LRU_WIDTH
DILATIONS = (1, 4, 16)
ATTN_BLOCK = 128
N_EXPERTS = 32
TOP_K = 4
D_FF = D_MODEL
SWIGLU_LIMIT = 7.0
SWIGLU_ALPHA = 1.702
MOE_BLOCK = 128
EPS = 1e-6

LANES = 128
SUBLANES = 8
VMEM_LIMIT = 56 * 1024 * 1024


def _params(sem, vmem=VMEM_LIMIT):
    return pltpu.CompilerParams(dimension_semantics=sem, vmem_limit_bytes=vmem)


def _rms(xf, g):
    return xf * lax.rsqrt(jnp.mean(xf * xf, axis=-1, keepdims=True) + EPS) * g


def _in_proj_body(x_ref, g_ref, w_ref, o_ref, h_ref):
    @pl.when(pl.program_id(1) == 0)
    def _():
        h_ref[...] = _rms(x_ref[...], g_ref[...]).astype(BF16)

    o_ref[...] = jnp.dot(h_ref[...], w_ref[...].astype(BF16), preferred_element_type=F32)


def _in_proj(x2, g, w_in, tm=1024, tn=512):
    n, d = x2.shape
    cols = w_in.shape[1]
    return pl.pallas_call(
        _in_proj_body,
        grid=(n // tm, cols // tn),
        in_specs=[
            pl.BlockSpec((tm, d), lambda i, j: (i, 0)),
            pl.BlockSpec((1, d), lambda i, j: (0, 0)),
            pl.BlockSpec((d, tn), lambda i, j: (0, j)),
        ],
        out_specs=pl.BlockSpec((tm, tn), lambda i, j: (i, j)),
        out_shape=jax.ShapeDtypeStruct((n, cols), F32),
        scratch_shapes=[pltpu.VMEM((tm, d), BF16)],
        compiler_params=_params(("parallel", "arbitrary")),
        name="in_proj",
    )(x2, g.reshape(1, d), w_in)


def _attention_body(slope_ref, q_ref, k_ref, v_ref, o_ref, qh, kh, vh, acc, mx):
    seq = q_ref.shape[1]
    pair = pl.program_id(1)
    lane = lax.broadcasted_iota(jnp.int32, (seq, LANES), 1)
    qi = lax.broadcasted_iota(jnp.int32, (ATTN_BLOCK, 2 * ATTN_BLOCK), 0)
    ki = lax.broadcasted_iota(jnp.int32, (ATTN_BLOCK, 2 * ATTN_BLOCK), 1)
    rel2 = qi + ATTN_BLOCK - ki
    ok2 = (rel2 >= 0) & (rel2 <= ATTN_BLOCK)
    rel1 = (qi - ki)[:, :ATTN_BLOCK]
    ok1 = rel1 >= 0

    def unit(br, d, q_start, k_start, nk, bias):
        if d == 1:
            qs, ks = pl.ds(q_start, ATTN_BLOCK), pl.ds(k_start, nk)
        else:
            qs = pl.ds(q_start, ATTN_BLOCK, stride=d)
            ks = pl.ds(k_start, nk, stride=d)
        q = qh[qs, :].astype(BF16)
        k = kh[ks, :].astype(BF16)
        v = vh[ks, :].astype(BF16)
        s = lax.dot_general(q, k, (((1,), (1,)), ((), ())), preferred_element_type=F32) + bias
        m = jnp.max(s, axis=1, keepdims=True)
        p = jnp.exp(s - m).astype(BF16)
        acc[br, qs, :] = jnp.dot(p, v, preferred_element_type=F32)
        mx[br, qs, :] = jnp.broadcast_to(m, (ATTN_BLOCK, LANES))

    for hh in range(LANES // HEAD_DIM):
        lo = hh * HEAD_DIM
        slope = slope_ref[pair * (LANES // HEAD_DIM) + hh]
        qh[...] = q_ref[0, :, lo:lo + HEAD_DIM] * (HEAD_DIM ** -0.5)
        kh[...] = k_ref[0, :, lo:lo + HEAD_DIM]
        vfull = v_ref[0]
        if lo:
            vfull = pltpu.roll(vfull, LANES - lo, axis=1)
        vh[...] = jnp.where(lane < HEAD_DIM, vfull, 1.0)

        for br, d in enumerate(DILATIONS):
            nb = seq // (ATTN_BLOCK * d)
            bias1 = jnp.where(ok1, -slope * (rel1 * d).astype(F32), -jnp.inf)
            bias2 = jnp.where(ok2, -slope * (rel2 * d).astype(F32), -jnp.inf)

            def first(r, carry, br=br, d=d, bias1=bias1):
                unit(br, d, r, r, ATTN_BLOCK, bias1)
                return carry

            lax.fori_loop(0, d, first, 0)

            if nb > 1:
                def later(u, carry, br=br, d=d, nb=nb, bias2=bias2):
                    r = u // (nb - 1)
                    n = u % (nb - 1) + 1
                    start = r + n * ATTN_BLOCK * d
                    unit(br, d, start, start - ATTN_BLOCK * d, 2 * ATTN_BLOCK, bias2)
                    return carry

                lax.fori_loop(0, d * (nb - 1), later, 0)

        rows = 256

        def merge(c, carry, lo=lo):
            sl = pl.ds(pl.multiple_of(c * rows, rows), rows)
            m0, m1, m2 = mx[0, sl, :], mx[1, sl, :], mx[2, sl, :]
            mt = jnp.maximum(jnp.maximum(m0, m1), m2)
            tot = (jnp.exp(m0 - mt) * acc[0, sl, :] + jnp.exp(m1 - mt) * acc[1, sl, :]
                   + jnp.exp(m2 - mt) * acc[2, sl, :])
            den = pltpu.roll(tot, HEAD_DIM, axis=1)
            out = tot / den
            o_ref[0, sl, lo:lo + HEAD_DIM] = out[:, :HEAD_DIM]
            return carry

        lax.fori_loop(0, seq // rows, merge, 0)


def _attention(proj3, slopes):
    b, seq, _ = proj3.shape
    npair = ATTN_WIDTH // LANES
    blk = (1, seq, LANES)
    return pl.pallas_call(
        _attention_body,
        grid_spec=pltpu.PrefetchScalarGridSpec(
            num_scalar_prefetch=1,
            grid=(b, npair),
            in_specs=[
                pl.BlockSpec(blk, lambda i, j, s: (i, 0, j)),
                pl.BlockSpec(blk, lambda i, j, s: (i, 0, npair + j)),
                pl.BlockSpec(blk, lambda i, j, s: (i, 0, 2 * npair + j)),
            ],
            out_specs=pl.BlockSpec(blk, lambda i, j, s: (i, 0, j)),
            scratch_shapes=[
                pltpu.VMEM((seq, HEAD_DIM), F32),
                pltpu.VMEM((seq, HEAD_DIM), F32),
                pltpu.VMEM((seq, LANES), F32),
                pltpu.VMEM((len(DILATIONS), seq, LANES), F32),
                pltpu.VMEM((len(DILATIONS), seq, LANES), F32),
            ],
        ),
        out_shape=jax.ShapeDtypeStruct((b, seq, ATTN_WIDTH), F32),
        compiler_params=_params(("parallel", "parallel")),
        name="attention",
    )(slopes, proj3, proj3, proj3)


def _rglru_body(xr_ref, gr_ref, cw_ref, cb_ref, w_ref, ba_ref, bx_ref, lam_ref, o_ref,
                xp, a_s, b_s):
    seq = xr_ref.shape[1]
    pad = SUBLANES
    xp[0:pad, :] = jnp.zeros((pad, LANES), F32)
    xp[pad:pad + seq, :] = xr_ref[0]
    lam = lam_ref[...]
    sp = jnp.maximum(-lam, 0.0) + jnp.log(1.0 + jnp.exp(-jnp.abs(lam)))
    w_hi = w_ref[0].astype(BF16)
    w_lo = (w_ref[0] - w_hi.astype(F32)).astype(BF16)
    rows = 256

    def gates(c, carry):
        base = pl.multiple_of(c * rows, rows)
        xc = cb_ref[...] + cw_ref[0:1, :] * xp[pl.ds(base + pad - 3, rows), :]
        for i in range(1, CONV_WIDTH):
            xc = xc + cw_ref[i:i + 1, :] * xp[pl.ds(base + pad - 3 + i, rows), :]
        hi = xc.astype(BF16)
        lo = (xc - hi.astype(F32)).astype(BF16)
        pre = (jnp.dot(hi, w_hi, preferred_element_type=F32)
               + jnp.dot(lo, w_hi, preferred_element_type=F32)
               + jnp.dot(hi, w_lo, preferred_element_type=F32))
        r = jax.nn.sigmoid(pre[:, :LANES] + ba_ref[...])
        ig = jax.nn.sigmoid(pre[:, LANES:] + bx_ref[...])
        log_a = -LRU_C * r * sp
        a = jnp.exp(log_a)
        t = jnp.tanh(log_a)
        b = jnp.sqrt(-2.0 * t / (1.0 - t)) * (ig * xc)
        a_s[pl.ds(base, rows), :] = a
        b_s[pl.ds(base, rows), :] = b
        return carry

    lax.fori_loop(0, seq // rows, gates, 0)

    row = lax.broadcasted_iota(jnp.int32, (SUBLANES, LANES), 0)

    def scan(c, h_prev):
        sl = pl.ds(pl.multiple_of(c * SUBLANES, SUBLANES), SUBLANES)
        a = a_s[sl, :]
        b = b_s[sl, :]
        for s in (1, 2, 4):
            keep = row >= s
            a_sh = jnp.where(keep, pltpu.roll(a, s, axis=0), 1.0)
            b_sh = jnp.where(keep, pltpu.roll(b, s, axis=0), 0.0)
            b = a * b_sh + b
            a = a * a_sh
        h = a * h_prev + b
        o_ref[0, sl, :] = h * jax.nn.gelu(gr_ref[0, sl, :])
        return jnp.broadcast_to(h[SUBLANES - 1:SUBLANES, :], (SUBLANES, LANES))

    lax.fori_loop(0, seq // SUBLANES, scan, jnp.zeros((SUBLANES, LANES), F32), unroll=8)


def _rglru(proj3, conv_w, conv_b, w_bd, b_a, b_x, lam):
    b, seq, _ = proj3.shape
    nt = LRU_WIDTH // LANES
    xr0 = 3 * ATTN_WIDTH // LANES
    gr0 = xr0 + nt
    blk = (1, seq, LANES)
    vec = lambda: pl.BlockSpec((1, LANES), lambda i, j: (0, j))
    return pl.pallas_call(
        _rglru_body,
        grid=(b, nt),
        in_specs=[
            pl.BlockSpec(blk, lambda i, j: (i, 0, xr0 + j)),
            pl.BlockSpec(blk, lambda i, j: (i, 0, gr0 + j)),
            pl.BlockSpec((CONV_WIDTH, LANES), lambda i, j: (0, j)),
            vec(),
            pl.BlockSpec((1, LANES, 2 * LANES), lambda i, j: (j, 0, 0)),
            vec(), vec(), vec(),
        ],
        out_specs=pl.BlockSpec(blk, lambda i, j: (i, 0, j)),
        out_shape=jax.ShapeDtypeStruct((b, seq, LRU_WIDTH), F32),
        scratch_shapes=[
            pltpu.VMEM((seq + SUBLANES, LANES), F32),
            pltpu.VMEM((seq, LANES), F32),
            pltpu.VMEM((seq, LANES), F32),
        ],
        compiler_params=_params(("parallel", "parallel")),
        name="rglru",
    )(proj3, proj3, conv_w, conv_b.reshape(1, -1), w_bd, b_a.reshape(1, -1),
      b_x.reshape(1, -1), lam.reshape(1, -1))


def _out_proj_body(at_ref, rc_ref, ga_ref, gr_ref, x_ref, w_ref, o_ref, h_ref):
    wa = at_ref.shape[1]

    @pl.when(pl.program_id(1) == 0)
    def _():
        h_ref[:, :wa] = _rms(at_ref[...], ga_ref[...]).astype(BF16)
        h_ref[:, wa:] = _rms(rc_ref[...], gr_ref[...]).astype(BF16)

    o_ref[...] = x_ref[...] + jnp.dot(h_ref[...], w_ref[...].astype(BF16),
                                      preferred_element_type=F32)


def _out_proj(attn2, rec2, g_attn, g_rec, x2, w_out, tm=512, tn=1024):
    n, d = x2.shape
    wa, wr = attn2.shape[1], rec2.shape[1]
    return pl.pallas_call(
        _out_proj_body,
        grid=(n // tm, d // tn),
        in_specs=[
            pl.BlockSpec((tm, wa), lambda i, j: (i, 0)),
            pl.BlockSpec((tm, wr), lambda i, j: (i, 0)),
            pl.BlockSpec((1, wa), lambda i, j: (0, 0)),
            pl.BlockSpec((1, wr), lambda i, j: (0, 0)),
            pl.BlockSpec((tm, tn), lambda i, j: (i, j)),
            pl.BlockSpec((d, tn), lambda i, j: (0, j)),
        ],
        out_specs=pl.BlockSpec((tm, tn), lambda i, j: (i, j)),
        out_shape=jax.ShapeDtypeStruct((n, d), F32),
        scratch_shapes=[pltpu.VMEM((tm, d), BF16)],
        compiler_params=_params(("parallel", "arbitrary")),
        name="out_proj",
    )(attn2, rec2, g_attn.reshape(1, wa), g_rec.reshape(1, wr), x2, w_out)


def _router_body(x_ref, g_ref, wt_ref, b_ref, xn_ref, idx_ref, gate_ref, rank_ref, cnt_ref,
                 base):
    tm = x_ref.shape[0]

    @pl.when(pl.program_id(0) == 0)
    def _():
        base[...] = jnp.zeros_like(base)

    xn = _rms(x_ref[...], g_ref[...])
    xn_ref[...] = xn
    x_hi = xn.astype(BF16)
    x_lo = (xn - x_hi.astype(F32)).astype(BF16)
    w = wt_ref[...]
    w_hi = w.astype(BF16)
    w_lo = (w - w_hi.astype(F32)).astype(BF16)
    nt = (((1,), (1,)), ((), ()))
    logits = (lax.dot_general(w_hi, x_hi, nt, preferred_element_type=F32)
              + lax.dot_general(w_hi, x_lo, nt, preferred_element_type=F32)
              + lax.dot_general(w_lo, x_hi, nt, preferred_element_type=F32)
              + b_ref[...])

    eid = lax.broadcasted_iota(jnp.int32, (N_EXPERTS, tm), 0)
    work = logits
    vals, hots = [], []
    for _ in range(TOP_K):
        best = jnp.max(work, axis=0, keepdims=True)
        pick = jnp.min(jnp.where(work == best, eid, N_EXPERTS), axis=0, keepdims=True)
        hot = eid == pick
        vals.append(best)
        hots.append(hot)
        work = jnp.where(hot, -jnp.inf, work)
        idx_ref[len(vals) - 1:len(vals), :] = pick

    ex = [jnp.exp(v - vals[0]) for v in vals]
    den = ex[0] + ex[1] + ex[2] + ex[3]
    for k in range(TOP_K):
        gate_ref[k:k + 1, :] = ex[k] / den

    chosen = (hots[0] | hots[1] | hots[2] | hots[3])
    si = lax.broadcasted_iota(jnp.int32, (tm, tm), 0)
    ti = lax.broadcasted_iota(jnp.int32, (tm, tm), 1)
    before = (si < ti).astype(BF16)
    prefix = jnp.dot(chosen.astype(BF16), before, preferred_element_type=F32)
    slot = base[:, 0:1] + prefix
    for k in range(TOP_K):
        rank_ref[k:k + 1, :] = jnp.sum(jnp.where(hots[k], slot, 0.0), axis=0,
                                       keepdims=True).astype(jnp.int32)
    base[...] = base[...] + jnp.sum(chosen.astype(F32), axis=1, keepdims=True)
    cnt_ref[...] = base[...].astype(jnp.int32)


def _router(x2, g, w_router_t, b_router, tm=256):
    n, d = x2.shape
    row = lambda: pl.BlockSpec((TOP_K, tm), lambda i: (0, i))
    return pl.pallas_call(
        _router_body,
        grid=(n // tm,),
        in_specs=[
            pl.BlockSpec((tm, d), lambda i: (i, 0)),
            pl.BlockSpec((1, d), lambda i: (0, 0)),
            pl.BlockSpec((N_EXPERTS, d), lambda i: (0, 0)),
            pl.BlockSpec((N_EXPERTS, 1), lambda i: (0, 0)),
        ],
        out_specs=[
            pl.BlockSpec((tm, d), lambda i: (i, 0)),
            row(), row(), row(),
            pl.BlockSpec((N_EXPERTS, LANES), lambda i: (0, 0)),
        ],
        out_shape=[
            jax.ShapeDtypeStruct((n, d), F32),
            jax.ShapeDtypeStruct((TOP_K, n), jnp.int32),
            jax.ShapeDtypeStruct((TOP_K, n), F32),
            jax.ShapeDtypeStruct((TOP_K, n), jnp.int32),
            jax.ShapeDtypeStruct((N_EXPERTS, LANES), jnp.int32),
        ],
        scratch_shapes=[pltpu.VMEM((N_EXPERTS, LANES), F32)],
        compiler_params=_params(("arbitrary",)),
        name="router",
    )(x2, g.reshape(1, d), w_router_t, b_router.reshape(N_EXPERTS, 1))


def _scatter_body(dest_ref, pad_ref, xn_ref, o_ref, zrow, sem, *, n_tokens, pad_per_step):
    tm = xn_ref.shape[0]
    i = pl.program_id(0)

    @pl.when(i == 0)
    def _():
        zrow[...] = jnp.zeros_like(zrow)

    def row_copy(t, k):
        d = dest_ref[k * n_tokens + i * tm + t]
        return pltpu.make_async_copy(xn_ref.at[pl.ds(t, 1), :], o_ref.at[pl.ds(d, 1), :], sem)

    def pad_copy(q):
        d = pad_ref[i * pad_per_step + q]
        return pltpu.make_async_copy(zrow.at[pl.ds(0, 1), :], o_ref.at[pl.ds(d, 1), :], sem)

    def issue(t, c):
        for k in range(TOP_K):
            row_copy(t, k).start()
        return c

    lax.fori_loop(0, tm, issue, 0)

    def issue_pad(q, c):
        pad_copy(q).start()
        return c

    lax.fori_loop(0, pad_per_step, issue_pad, 0)

    def drain(t, c):
        for k in range(TOP_K):
            row_copy(t, k).wait()
        return c

    lax.fori_loop(0, tm, drain, 0)

    def drain_pad(q, c):
        pad_copy(q).wait()
        return c

    lax.fori_loop(0, pad_per_step, drain_pad, 0)


def _scatter(dest_flat, pad_dest, xn, n_rows, tm=256):
    n, d = xn.shape
    steps = n // tm
    pad_per_step = pad_dest.shape[0] // steps
    return pl.pallas_call(
        functools.partial(_scatter_body, n_tokens=n, pad_per_step=pad_per_step),
        grid_spec=pltpu.PrefetchScalarGridSpec(
            num_scalar_prefetch=2,
            grid=(steps,),
            in_specs=[pl.BlockSpec((tm, d), lambda i, *_: (i, 0))],
            out_specs=pl.BlockSpec(memory_space=pl.ANY),
            scratch_shapes=[pltpu.VMEM((SUBLANES, d), F32), pltpu.SemaphoreType.DMA(())],
        ),
        out_shape=jax.ShapeDtypeStruct((n_rows, d), F32),
        compiler_params=_params(("arbitrary",)),
        name="scatter",
    )(dest_flat, pad_dest, xn)


def _gate_up_body(meta_ref, x_ref, w_ref, b_ref, o_ref, wbf):
    i = pl.program_id(1)
    n_valid = meta_ref[pl.num_programs(1)]
    ic = jnp.minimum(i, n_valid - 1)
    changed = (i == 0) | (meta_ref[ic] != meta_ref[jnp.maximum(ic - 1, 0)])
    kdim, tn = wbf.shape
    rows = 256

    @pl.when(changed & (i < n_valid))
    def _():
        def cast(c, carry):
            sl = pl.ds(pl.multiple_of(c * rows, rows), rows)
            wbf[sl, :] = w_ref[0, sl, :].astype(BF16)
            return carry

        lax.fori_loop(0, kdim // rows, cast, 0)

    @pl.when(i < n_valid)
    def _():
        x = x_ref[...].astype(BF16)
        even = lax.broadcasted_iota(jnp.int32, (x.shape[0], LANES), 1) % 2 == 0
        for c in range(tn // (2 * LANES)):
            c0 = c * 2 * LANES
            gu = jnp.dot(x, wbf[:, c0:c0 + 2 * LANES], preferred_element_type=F32)
            gu = gu + b_ref[0, :, c0:c0 + 2 * LANES]
            v1, v2 = gu[:, :LANES], gu[:, LANES:]
            gate = jnp.where(even, v1, pltpu.roll(v2, 1, axis=1))
            up = jnp.where(even, pltpu.roll(v1, LANES - 1, axis=1), v2)
            gate = jnp.minimum(gate, SWIGLU_LIMIT)
            up = jnp.clip(up, -SWIGLU_LIMIT, SWIGLU_LIMIT)
            act = gate * jax.nn.sigmoid(SWIGLU_ALPHA * gate) * (up + 1.0)
            o_ref[:, c * LANES:(c + 1) * LANES] = act.astype(BF16)

    @pl.when(i >= n_valid)
    def _():
        o_ref[...] = jnp.zeros_like(o_ref)


def _gate_up(meta, x_sorted, w_gate_up, b_gate_up, tn=1024):
    p, d = x_sorted.shape
    nblk = p // MOE_BLOCK
    cols = w_gate_up.shape[2]

    def blk(i, m):
        return jnp.minimum(i, m[nblk] - 1)

    return pl.pallas_call(
        _gate_up_body,
        grid_spec=pltpu.PrefetchScalarGridSpec(
            num_scalar_prefetch=1,
            grid=(cols // tn, nblk),
            in_specs=[
                pl.BlockSpec((MOE_BLOCK, d), lambda j, i, m: (blk(i, m), 0)),
                pl.BlockSpec((1, d, tn), lambda j, i, m: (m[blk(i, m)], 0, j)),
                pl.BlockSpec((1, 1, tn), lambda j, i, m: (m[blk(i, m)], 0, j)),
            ],
            out_specs=pl.BlockSpec((MOE_BLOCK, tn // 2), lambda j, i, m: (i, j)),
            scratch_shapes=[pltpu.VMEM((d, tn), BF16)],
        ),
        out_shape=jax.ShapeDtypeStruct((p, cols // 2), BF16),
        compiler_params=_params(("arbitrary", "arbitrary")),
        name="gate_up",
    )(meta, x_sorted, w_gate_up, b_gate_up.reshape(N_EXPERTS, 1, cols))


def _down_body(meta_ref, a_ref, w_ref, b_ref, o_ref, wbf, tmp):
    i = pl.program_id(1)
    n_valid = meta_ref[pl.num_programs(1)]
    ic = jnp.minimum(i, n_valid - 1)
    changed = (i == 0) | (meta_ref[ic] != meta_ref[jnp.maximum(ic - 1, 0)])
    kdim = wbf.shape[0]
    half = LANES // 2

    @pl.when(changed & (i < n_valid))
    def _():
        def permute(g, carry):
            base = pl.multiple_of(g * LANES, LANES)
            for c in range(tmp.shape[0]):
                cs = slice(c * LANES, (c + 1) * LANES)
                tmp[c, pl.ds(0, half, stride=2), :] = w_ref[0, pl.ds(base, half), cs]
                tmp[c, pl.ds(1, half, stride=2), :] = w_ref[0, pl.ds(base + half, half), cs]
                wbf[pl.ds(base, LANES), cs] = tmp[c].astype(BF16)
            return carry

        lax.fori_loop(0, kdim // LANES, permute, 0)

    @pl.when(i < n_valid)
    def _():
        o_ref[...] = jnp.dot(a_ref[...], wbf[...], preferred_element_type=F32) + b_ref[0]

    @pl.when(i >= n_valid)
    def _():
        o_ref[...] = jnp.zeros_like(o_ref)


def _down(meta, act, w_down, b_down, tn=1024):
    p, f = act.shape
    nblk = p // MOE_BLOCK
    d = w_down.shape[2]

    def blk(i, m):
        return jnp.minimum(i, m[nblk] - 1)

    return pl.pallas_call(
        _down_body,
        grid_spec=pltpu.PrefetchScalarGridSpec(
            num_scalar_prefetch=1,
            grid=(d // tn, nblk),
            in_specs=[
                pl.BlockSpec((MOE_BLOCK, f), lambda j, i, m: (blk(i, m), 0)),
                pl.BlockSpec((1, f, tn), lambda j, i, m: (m[blk(i, m)], 0, j)),
                pl.BlockSpec((1, 1, tn), lambda j, i, m: (m[blk(i, m)], 0, j)),
            ],
            out_specs=pl.BlockSpec((MOE_BLOCK, tn), lambda j, i, m: (i, j)),
            scratch_shapes=[pltpu.VMEM((f, tn), BF16),
                            pltpu.VMEM((tn // LANES, LANES, LANES), F32)],
        ),
        out_shape=jax.ShapeDtypeStruct((p, d), F32),
        compiler_params=_params(("arbitrary", "arbitrary")),
        name="down",
    )(meta, act, w_down, b_down.reshape(N_EXPERTS, 1, d))


def _combine_body(dest_ref, y_ref, x_ref, gate_ref, g_ref, o_ref, ybuf, sem, *, n_tokens):
    tm = x_ref.shape[0]
    i = pl.program_id(0)

    def row_copy(t, k):
        d = dest_ref[k * n_tokens + i * tm + t]
        return pltpu.make_async_copy(y_ref.at[pl.ds(d, 1), :], ybuf.at[k, pl.ds(t, 1), :], sem)

    def issue(t, c):
        for k in range(TOP_K):
            row_copy(t, k).start()
        return c

    lax.fori_loop(0, tm, issue, 0)

    def drain(t, c):
        for k in range(TOP_K):
            row_copy(t, k).wait()
        return c

    lax.fori_loop(0, tm, drain, 0)

    acc = x_ref[...]
    for k in range(TOP_K):
        acc = acc + gate_ref[:, k:k + 1] * ybuf[k]
    o_ref[...] = _rms(acc, g_ref[...])


def _combine(dest_flat, y_buf, x2, gates, g, tm=256):
    n, d = x2.shape
    return pl.pallas_call(
        functools.partial(_combine_body, n_tokens=n),
        grid_spec=pltpu.PrefetchScalarGridSpec(
            num_scalar_prefetch=1,
            grid=(n // tm,),
            in_specs=[
                pl.BlockSpec(memory_space=pl.ANY),
                pl.BlockSpec((tm, d), lambda i, *_: (i, 0)),
                pl.BlockSpec((tm, TOP_K), lambda i, *_: (i, 0)),
                pl.BlockSpec((1, d), lambda i, *_: (0, 0)),
            ],
            out_specs=pl.BlockSpec((tm, d), lambda i, *_: (i, 0)),
            scratch_shapes=[pltpu.VMEM((TOP_K, tm, d), F32), pltpu.SemaphoreType.DMA(())],
        ),
        out_shape=jax.ShapeDtypeStruct((n, d), F32),
        compiler_params=_params(("arbitrary",)),
        name="combine",
    )(dest_flat, y_buf, x2, gates, g.reshape(1, d))


def _cumsum_sublanes(x):
    row = lax.broadcasted_iota(jnp.int32, x.shape, 0)
    s = 1
    while s < x.shape[0]:
        x = x + jnp.where(row >= s, pltpu.roll(x, s, axis=0), 0)
        s *= 2
    return x


def _tables_body(cnt_ref, idx_ref, rank_ref, dest_ref, meta_ref, pad_ref, *, nblk):
    cnt = cnt_ref[...]
    padded = (cnt + (MOE_BLOCK - 1)) // MOE_BLOCK * MOE_BLOCK
    pend = _cumsum_sublanes(padded)
    pstart = pend - padded
    total = pend[N_EXPERTS - 1:N_EXPERTS, 0:1]

    def lookup(table, sel):
        eid = lax.broadcasted_iota(jnp.int32, (N_EXPERTS, sel.shape[1]), 0)
        return jnp.sum(jnp.where(eid == sel, table[:, 0:1], 0), axis=0, keepdims=True)

    def count_le(table, v):
        return jnp.sum((table[:, 0:1] <= v).astype(jnp.int32), axis=0, keepdims=True)

    width = 2048
    for k in range(TOP_K):
        for c in range(idx_ref.shape[1] // width):
            sl = slice(c * width, (c + 1) * width)
            dest_ref[k:k + 1, sl] = lookup(pstart, idx_ref[k:k + 1, sl]) + rank_ref[k:k + 1, sl]

    blk = lax.broadcasted_iota(jnp.int32, meta_ref.shape, 1)
    expert = jnp.minimum(count_le(pend, blk * MOE_BLOCK), N_EXPERTS - 1)
    meta_ref[...] = jnp.where(blk == nblk, total // MOE_BLOCK, expert)

    gap = padded - cnt
    gap_end = _cumsum_sublanes(gap)
    q = lax.broadcasted_iota(jnp.int32, pad_ref.shape, 1)
    e = count_le(gap_end, q)
    inside = lookup(pstart + cnt - (gap_end - gap), e) + q
    tail = total + q - gap_end[N_EXPERTS - 1:N_EXPERTS, 0:1]
    pad_ref[...] = jnp.where(e < N_EXPERTS, inside, tail)


def _tables(cnt, idx_t, rank_t, n_rows):
    nblk = n_rows // MOE_BLOCK
    n_pad = n_rows - idx_t.size
    meta_w = (nblk + 1 + LANES - 1) // LANES * LANES
    dest, meta, pad_dest = pl.pallas_call(
        functools.partial(_tables_body, nblk=nblk),
        out_shape=[
            jax.ShapeDtypeStruct(idx_t.shape, jnp.int32),
            jax.ShapeDtypeStruct((1, meta_w), jnp.int32),
            jax.ShapeDtypeStruct((1, n_pad), jnp.int32),
        ],
        name="tables",
    )(cnt, idx_t, rank_t)
    return dest.reshape(-1), meta.reshape(-1), pad_dest.reshape(-1)


def kernel(x, norm_mix, w_in, conv_w, conv_b, w_a, b_a, w_x, b_x, lru_lambda, attn_out_norm, lru_out_norm, w_out, norm_ffn, w_router, b_router, w_gate_up, b_gate_up, w_down, b_down, norm_final):
    b, seq, d = x.shape
    n = b * seq
    x2 = x.reshape(n, d)

    proj = _in_proj(x2, norm_mix, w_in)
    proj3 = proj.reshape(b, seq, IN_COLS)

    slopes = jnp.asarray(2.0 ** (-8.0 * np.arange(1, N_HEADS + 1) / N_HEADS), F32)
    attn = _attention(proj3, slopes)

    def pair_blocks(w):
        w4 = w.reshape(-1, 2, LRU_BLOCK, LRU_BLOCK)
        z = jnp.zeros_like(w4[:, 0])
        top = jnp.concatenate([w4[:, 0], z], axis=2)
        bot = jnp.concatenate([z, w4[:, 1]], axis=2)
        return jnp.concatenate([top, bot], axis=1)

    w_bd = jnp.concatenate([pair_blocks(w_a), pair_blocks(w_x)], axis=2)
    rec = _rglru(proj3, conv_w, conv_b, w_bd, b_a, b_x, lru_lambda)

    x_mid = _out_proj(attn.reshape(n, ATTN_WIDTH), rec.reshape(n, LRU_WIDTH),
                      attn_out_norm, lru_out_norm, x2, w_out)

    xn, idx_t, gate_t, rank_t, cnt = _router(x_mid, norm_ffn, w_router.T, b_router)
    n_rows = n * TOP_K + N_EXPERTS * MOE_BLOCK
    dest_flat, meta, pad_dest = _tables(cnt, idx_t, rank_t, n_rows)

    x_sorted = _scatter(dest_flat, pad_dest, xn, n_rows)
    act = _gate_up(meta, x_sorted, w_gate_up, b_gate_up)
    y_buf = _down(meta, act, w_down, b_down)
    out = _combine(dest_flat, y_buf, x_mid, gate_t.T, norm_final)
    return out.reshape(b, seq, d)
```

```python
import functools

import numpy as np
import jax
import jax.numpy as jnp
from jax import lax
from jax.experimental import pallas as pl
from jax.experimental.pallas import tpu as pltpu

F32 = jnp.float32
BF16 = jnp.bfloat16

D_MODEL = 2048
HEAD_DIM = 64
N_HEADS = 16
ATTN_WIDTH = N_HEADS * HEAD_DIM
LRU_WIDTH = D_MODEL - ATTN_WIDTH
LRU_BLOCK = 64
CONV_WIDTH = 4
LRU_C = 8.0
IN_COLS = 3 * ATTN_WIDTH + 2 * LRU_WIDTH
DILATIONS = (1, 4, 16)
ATTN_BLOCK = 128
UNITS_PER_TRIP = (5, 6, 8)
N_EXPERTS = 32
TOP_K = 4
D_FF = D_MODEL
SWIGLU_LIMIT = 7.0
SWIGLU_ALPHA = 1.702
MOE_BLOCK = 128
EPS = 1e-6

LANES = 128
SUBLANES = 8
VMEM_LIMIT = 56 * 1024 * 1024


def _params(sem, vmem=VMEM_LIMIT):
    return pltpu.CompilerParams(dimension_semantics=sem, vmem_limit_bytes=vmem)


def _rms(xf, g):
    return xf * lax.rsqrt(jnp.mean(xf * xf, axis=-1, keepdims=True) + EPS) * g


def _in_proj_body(x_ref, g_ref, w_ref, o_ref, h_ref):
    @pl.when(pl.program_id(1) == 0)
    def _():
        h_ref[...] = _rms(x_ref[...], g_ref[...]).astype(BF16)

    o_ref[...] = jnp.dot(h_ref[...], w_ref[...].astype(BF16), preferred_element_type=F32)


def _in_proj(x2, g, w_in, tm=1024, tn=512):
    n, d = x2.shape
    cols = w_in.shape[1]
    return pl.pallas_call(
        _in_proj_body,
        grid=(n // tm, cols // tn),
        in_specs=[
            pl.BlockSpec((tm, d), lambda i, j: (i, 0)),
            pl.BlockSpec((1, d), lambda i, j: (0, 0)),
            pl.BlockSpec((d, tn), lambda i, j: (0, j)),
        ],
        out_specs=pl.BlockSpec((tm, tn), lambda i, j: (i, j)),
        out_shape=jax.ShapeDtypeStruct((n, cols), F32),
        scratch_shapes=[pltpu.VMEM((tm, d), BF16)],
        compiler_params=_params(("parallel", "arbitrary")),
        name="in_proj",
    )(x2, g.reshape(1, d), w_in)


REGROUP = 4


def _attention_body(slope_ref, q_ref, k_ref, v_ref, o_ref, qh, kh, vh, acc, mx):
    seq = q_ref.shape[1]
    run = seq // REGROUP
    pair = pl.program_id(1)
    qi = lax.broadcasted_iota(jnp.int32, (ATTN_BLOCK, 2 * ATTN_BLOCK), 0)
    ki = lax.broadcasted_iota(jnp.int32, (ATTN_BLOCK, 2 * ATTN_BLOCK), 1)
    rel2 = qi + ATTN_BLOCK - ki
    ok2 = (rel2 >= 0) & (rel2 <= ATTN_BLOCK)
    rel1 = (qi - ki)[:, :ATTN_BLOCK]
    ok1 = rel1 >= 0

    nh = LANES // HEAD_DIM
    slopes = [slope_ref[pair * nh + hh] for hh in range(nh)]
    def stage(order, rows_dst, rows_src):
        lane = lax.broadcasted_iota(jnp.int32, (run, LANES), 1)
        qv, vv = q_ref[0, rows_src, :] * (HEAD_DIM ** -0.5), v_ref[0, rows_src, :]
        if order:
            kh[rows_dst, :] = k_ref[0, rows_src, :]
        for hh in range(nh):
            mine = (lane >= hh * HEAD_DIM) & (lane < (hh + 1) * HEAD_DIM)
            qh[order, hh, rows_dst, :] = jnp.where(mine, qv, 0.0)
            vh[order, hh, rows_dst, :] = jnp.where(mine, vv, 1.0)

    for c in range(REGROUP):
        stage(0, slice(c * run, (c + 1) * run), pl.ds(c * run, run))
        stage(1, slice(c * run, (c + 1) * run), pl.ds(c, run, stride=REGROUP))

    def run_units(units):
        loaded = [(qh[order, hh, qs, :].astype(BF16),
                   (kh[ks, :] if order else k_ref[0, ks, :]).astype(BF16),
                   vh[order, hh, ks, :].astype(BF16)) for hh, br, order, qs, ks, bias in units]
        scores = [lax.dot_general(q, k, (((1,), (1,)), ((), ())), preferred_element_type=F32)
                  + u[5] for (q, k, v), u in zip(loaded, units)]
        maxes = [jnp.max(s, axis=1, keepdims=True) for s in scores]
        probs = [jnp.exp(s - m).astype(BF16) for s, m in zip(scores, maxes)]
        results = [(jnp.dot(p, v, preferred_element_type=F32), m)
                   for p, (q, k, v), m in zip(probs, loaded, maxes)]
        for (a, m), (hh, br, order, qs, ks, bias) in zip(results, units):
            acc[hh, br, qs, :] = a
            mx[hh, br, qs, :] = jnp.broadcast_to(m, (ATTN_BLOCK, LANES))

    def bias_pair(ok, rel, d):
        return [jnp.where(ok, -sl * (rel * d).astype(F32), -jnp.inf) for sl in slopes]

    def contiguous_branch(br, order, d, n_runs, per_trip):
        nb = seq // (n_runs * ATTN_BLOCK)
        bias1, bias2 = bias_pair(ok1, rel1, d), bias_pair(ok2, rel2, d)
        run_units([(hh, br, order, pl.ds(r * nb * ATTN_BLOCK, ATTN_BLOCK),
                    pl.ds(r * nb * ATTN_BLOCK, ATTN_BLOCK), bias1[hh])
                   for r in range(n_runs) for hh in range(nh)])

        def later(g, carry):
            units = []
            for u in range(per_trip):
                idx = g * per_trip + u
                blk = idx // (nb - 1) * nb + idx % (nb - 1) + 1
                start = pl.multiple_of(blk * ATTN_BLOCK, ATTN_BLOCK)
                units += [(hh, br, order, pl.ds(start, ATTN_BLOCK),
                           pl.ds(start - ATTN_BLOCK, 2 * ATTN_BLOCK), bias2[hh])
                          for hh in range(nh)]
            run_units(units)
            return carry

        lax.fori_loop(0, n_runs * (nb - 1) // per_trip, later, 0)

    contiguous_branch(0, 0, DILATIONS[0], 1, UNITS_PER_TRIP[0])
    contiguous_branch(1, 1, DILATIONS[1], REGROUP, UNITS_PER_TRIP[1])

    d2 = DILATIONS[2]
    sub = d2 // REGROUP
    bias16 = bias_pair(ok1, rel1, d2)

    def strided(g, carry):
        units = []
        for u in range(UNITS_PER_TRIP[2]):
            idx = g * UNITS_PER_TRIP[2] + u
            rows = pl.ds(idx // sub * run + idx % sub, ATTN_BLOCK, stride=sub)
            units += [(hh, 2, 1, rows, rows, bias16[hh]) for hh in range(nh)]
        run_units(units)
        return carry

    lax.fori_loop(0, d2 // UNITS_PER_TRIP[2], strided, 0)

    def merge(c, carry):
        start = c // (run // ATTN_BLOCK) + c % (run // ATTN_BLOCK) * (ATTN_BLOCK * REGROUP)
        nat = pl.ds(start, ATTN_BLOCK, stride=REGROUP)
        reg = pl.ds(pl.multiple_of(c * ATTN_BLOCK, ATTN_BLOCK), ATTN_BLOCK)
        outs = []
        for hh in range(nh):
            m0, m1, m2 = mx[hh, 0, nat, :], mx[hh, 1, reg, :], mx[hh, 2, reg, :]
            mt = jnp.maximum(jnp.maximum(m0, m1), m2)
            tot = (jnp.exp(m0 - mt) * acc[hh, 0, nat, :] + jnp.exp(m1 - mt) * acc[hh, 1, reg, :]
                   + jnp.exp(m2 - mt) * acc[hh, 2, reg, :])
            den = pltpu.roll(tot, HEAD_DIM, axis=1)
            outs.append(tot / den)
        lane_r = lax.broadcasted_iota(jnp.int32, (ATTN_BLOCK, LANES), 1)
        o_ref[0, nat, :] = jnp.where(lane_r < HEAD_DIM, outs[0], outs[1])
        return carry

    lax.fori_loop(0, seq // ATTN_BLOCK, merge, 0, unroll=2)


def _attention(proj3, slopes):
    b, seq, _ = proj3.shape
    npair = ATTN_WIDTH // LANES
    nh = LANES // HEAD_DIM
    blk = (1, seq, LANES)
    return pl.pallas_call(
        _attention_body,
        grid_spec=pltpu.PrefetchScalarGridSpec(
            num_scalar_prefetch=1,
            grid=(b, npair),
            in_specs=[
                pl.BlockSpec(blk, lambda i, j, s: (i, 0, j)),
                pl.BlockSpec(blk, lambda i, j, s: (i, 0, npair + j)),
                pl.BlockSpec(blk, lambda i, j, s: (i, 0, 2 * npair + j)),
            ],
            out_specs=pl.BlockSpec(blk, lambda i, j, s: (i, 0, j)),
            scratch_shapes=[
                pltpu.VMEM((2, nh, seq, LANES), F32),
                pltpu.VMEM((seq, LANES), F32),
                pltpu.VMEM((2, nh, seq, LANES), F32),
                pltpu.VMEM((nh, len(DILATIONS), seq, LANES), F32),
                pltpu.VMEM((nh, len(DILATIONS), seq, LANES), F32),
            ],
        ),
        out_shape=jax.ShapeDtypeStruct((b, seq, ATTN_WIDTH), F32),
        compiler_params=_params(("parallel", "parallel")),
        name="attention",
    )(slopes, proj3, proj3, proj3)


def _rglru_body(xr_ref, gr_ref, cw_ref, cb_ref, w_ref, ba_ref, bx_ref, lam_ref, o_ref,
                xp, a_s, b_s):
    seq = xr_ref.shape[1]
    pad = SUBLANES
    xp[0:pad, :] = jnp.zeros((pad, LANES), F32)
    xp[pad:pad + seq, :] = xr_ref[0]
    lam = lam_ref[...]
    sp = jnp.maximum(-lam, 0.0) + jnp.log(1.0 + jnp.exp(-jnp.abs(lam)))
    w_hi = w_ref[0].astype(BF16)
    w_lo = (w_ref[0] - w_hi.astype(F32)).astype(BF16)
    rows = 256

    def gates(c, carry):
        base = pl.multiple_of(c * rows, rows)
        xc = cb_ref[...] + cw_ref[0:1, :] * xp[pl.ds(base + pad - 3, rows), :]
        for i in range(1, CONV_WIDTH):
            xc = xc + cw_ref[i:i + 1, :] * xp[pl.ds(base + pad - 3 + i, rows), :]
        hi = xc.astype(BF16)
        lo = (xc - hi.astype(F32)).astype(BF16)
        pre = (jnp.dot(hi, w_hi, preferred_element_type=F32)
               + jnp.dot(lo, w_hi, preferred_element_type=F32)
               + jnp.dot(hi, w_lo, preferred_element_type=F32))
        r = jax.nn.sigmoid(pre[:, :LANES] + ba_ref[...])
        ig = jax.nn.sigmoid(pre[:, LANES:] + bx_ref[...])
        log_a = -LRU_C * r * sp
        a = jnp.exp(log_a)
        t = jnp.tanh(log_a)
        b = jnp.sqrt(-2.0 * t / (1.0 - t)) * (ig * xc)
        a_s[pl.ds(base, rows), :] = a
        b_s[pl.ds(base, rows), :] = b
        return carry

    lax.fori_loop(0, seq // rows, gates, 0)

    row = lax.broadcasted_iota(jnp.int32, (SUBLANES, LANES), 0)

    def scan(c, h_prev):
        sl = pl.ds(pl.multiple_of(c * SUBLANES, SUBLANES), SUBLANES)
        a = a_s[sl, :]
        b = b_s[sl, :]
        for s in (1, 2, 4):
            keep = row >= s
            a_sh = jnp.where(keep, pltpu.roll(a, s, axis=0), 1.0)
            b_sh = jnp.where(keep, pltpu.roll(b, s, axis=0), 0.0)
            b = a * b_sh + b
            a = a * a_sh
        h = a * h_prev + b
        o_ref[0, sl, :] = h * jax.nn.gelu(gr_ref[0, sl, :])
        return jnp.broadcast_to(h[SUBLANES - 1:SUBLANES, :], (SUBLANES, LANES))

    lax.fori_loop(0, seq // SUBLANES, scan, jnp.zeros((SUBLANES, LANES), F32), unroll=8)


def _rglru(proj3, conv_w, conv_b, w_bd, b_a, b_x, lam):
    b, seq, _ = proj3.shape
    nt = LRU_WIDTH // LANES
    xr0 = 3 * ATTN_WIDTH // LANES
    gr0 = xr0 + nt
    blk = (1, seq, LANES)
    vec = lambda: pl.BlockSpec((1, LANES), lambda i, j: (0, j))
    return pl.pallas_call(
        _rglru_body,
        grid=(b, nt),
        in_specs=[
            pl.BlockSpec(blk, lambda i, j: (i, 0, xr0 + j)),
            pl.BlockSpec(blk, lambda i, j: (i, 0, gr0 + j)),
            pl.BlockSpec((CONV_WIDTH, LANES), lambda i, j: (0, j)),
            vec(),
            pl.BlockSpec((1, LANES, 2 * LANES), lambda i, j: (j, 0, 0)),
            vec(), vec(), vec(),
        ],
        out_specs=pl.BlockSpec(blk, lambda i, j: (i, 0, j)),
        out_shape=jax.ShapeDtypeStruct((b, seq, LRU_WIDTH), F32),
        scratch_shapes=[
            pltpu.VMEM((seq + SUBLANES, LANES), F32),
            pltpu.VMEM((seq, LANES), F32),
            pltpu.VMEM((seq, LANES), F32),
        ],
        compiler_params=_params(("parallel", "parallel")),
        name="rglru",
    )(proj3, proj3, conv_w, conv_b.reshape(1, -1), w_bd, b_a.reshape(1, -1),
      b_x.reshape(1, -1), lam.reshape(1, -1))


def _out_proj_body(at_ref, rc_ref, ga_ref, gr_ref, x_ref, w_ref, o_ref, h_ref):
    wa = at_ref.shape[1]

    @pl.when(pl.program_id(1) == 0)
    def _():
        h_ref[:, :wa] = _rms(at_ref[...], ga_ref[...]).astype(BF16)
        h_ref[:, wa:] = _rms(rc_ref[...], gr_ref[...]).astype(BF16)

    o_ref[...] = x_ref[...] + jnp.dot(h_ref[...], w_ref[...].astype(BF16),
                                      preferred_element_type=F32)


def _out_proj(attn2, rec2, g_attn, g_rec, x2, w_out, tm=512, tn=1024):
    n, d = x2.shape
    wa, wr = attn2.shape[1], rec2.shape[1]
    return pl.pallas_call(
        _out_proj_body,
        grid=(n // tm, d // tn),
        in_specs=[
            pl.BlockSpec((tm, wa), lambda i, j: (i, 0)),
            pl.BlockSpec((tm, wr), lambda i, j: (i, 0)),
            pl.BlockSpec((1, wa), lambda i, j: (0, 0)),
            pl.BlockSpec((1, wr), lambda i, j: (0, 0)),
            pl.BlockSpec((tm, tn), lambda i, j: (i, j)),
            pl.BlockSpec((d, tn), lambda i, j: (0, j)),
        ],
        out_specs=pl.BlockSpec((tm, tn), lambda i, j: (i, j)),
        out_shape=jax.ShapeDtypeStruct((n, d), F32),
        scratch_shapes=[pltpu.VMEM((tm, d), BF16)],
        compiler_params=_params(("parallel", "arbitrary")),
        name="out_proj",
    )(attn2, rec2, g_attn.reshape(1, wa), g_rec.reshape(1, wr), x2, w_out)


def _router_body(x_ref, g_ref, wt_ref, b_ref, xn_ref, idx_ref, gate_ref, rank_ref, cnt_ref,
                 base):
    tm = x_ref.shape[0]

    @pl.when(pl.program_id(0) == 0)
    def _():
        base[...] = jnp.zeros_like(base)

    xn = _rms(x_ref[...], g_ref[...])
    xn_ref[...] = xn
    x_hi = xn.astype(BF16)
    x_lo = (xn - x_hi.astype(F32)).astype(BF16)
    w = wt_ref[...]
    w_hi = w.astype(BF16)
    w_lo = (w - w_hi.astype(F32)).astype(BF16)
    nt = (((1,), (1,)), ((), ()))
    logits = (lax.dot_general(w_hi, x_hi, nt, preferred_element_type=F32)
              + lax.dot_general(w_hi, x_lo, nt, preferred_element_type=F32)
              + lax.dot_general(w_lo, x_hi, nt, preferred_element_type=F32)
              + b_ref[...])

    eid = lax.broadcasted_iota(jnp.int32, (N_EXPERTS, tm), 0)
    work = logits
    vals, hots = [], []
    for _ in range(TOP_K):
        best = jnp.max(work, axis=0, keepdims=True)
        pick = jnp.min(jnp.where(work == best, eid, N_EXPERTS), axis=0, keepdims=True)
        hot = eid == pick
        vals.append(best)
        hots.append(hot)
        work = jnp.where(hot, -jnp.inf, work)
        idx_ref[len(vals) - 1:len(vals), :] = pick

    ex = [jnp.exp(v - vals[0]) for v in vals]
    den = ex[0] + ex[1] + ex[2] + ex[3]
    for k in range(TOP_K):
        gate_ref[k:k + 1, :] = ex[k] / den

    chosen = (hots[0] | hots[1] | hots[2] | hots[3])
    si = lax.broadcasted_iota(jnp.int32, (tm, tm), 0)
    ti = lax.broadcasted_iota(jnp.int32, (tm, tm), 1)
    before = (si < ti).astype(BF16)
    prefix = jnp.dot(chosen.astype(BF16), before, preferred_element_type=F32)
    slot = base[:, 0:1] + prefix
    for k in range(TOP_K):
        rank_ref[k:k + 1, :] = jnp.sum(jnp.where(hots[k], slot, 0.0), axis=0,
                                       keepdims=True).astype(jnp.int32)
    base[...] = base[...] + jnp.sum(chosen.astype(F32), axis=1, keepdims=True)
    cnt_ref[...] = base[...].astype(jnp.int32)


def _router(x2, g, w_router_t, b_router, tm=256):
    n, d = x2.shape
    row = lambda: pl.BlockSpec((TOP_K, tm), lambda i: (0, i))
    return pl.pallas_call(
        _router_body,
        grid=(n // tm,),
        in_specs=[
            pl.BlockSpec((tm, d), lambda i: (i, 0)),
            pl.BlockSpec((1, d), lambda i: (0, 0)),
            pl.BlockSpec((N_EXPERTS, d), lambda i: (0, 0)),
            pl.BlockSpec((N_EXPERTS, 1), lambda i: (0, 0)),
        ],
        out_specs=[
            pl.BlockSpec((tm, d), lambda i: (i, 0)),
            row(), row(), row(),
            pl.BlockSpec((N_EXPERTS, LANES), lambda i: (0, 0)),
        ],
        out_shape=[
            jax.ShapeDtypeStruct((n, d), F32),
            jax.ShapeDtypeStruct((TOP_K, n), jnp.int32),
            jax.ShapeDtypeStruct((TOP_K, n), F32),
            jax.ShapeDtypeStruct((TOP_K, n), jnp.int32),
            jax.ShapeDtypeStruct((N_EXPERTS, LANES), jnp.int32),
        ],
        scratch_shapes=[pltpu.VMEM((N_EXPERTS, LANES), F32)],
        compiler_params=_params(("arbitrary",)),
        name="router",
    )(x2, g.reshape(1, d), w_router_t, b_router.reshape(N_EXPERTS, 1))


def _scatter_body(dest_ref, pad_ref, xn_ref, o_ref, zrow, sem, *, n_tokens, pad_per_step):
    tm = xn_ref.shape[0]
    i = pl.program_id(0)

    @pl.when(i == 0)
    def _():
        zrow[...] = jnp.zeros_like(zrow)

    def row_copy(t, k):
        d = dest_ref[k * n_tokens + i * tm + t]
        return pltpu.make_async_copy(xn_ref.at[pl.ds(t, 1), :], o_ref.at[pl.ds(d, 1), :], sem)

    def pad_copy(q):
        d = pad_ref[i * pad_per_step + q]
        return pltpu.make_async_copy(zrow.at[pl.ds(0, 1), :], o_ref.at[pl.ds(d, 1), :], sem)

    def issue(t, c):
        for k in range(TOP_K):
            row_copy(t, k).start(priority=k % 2)
        return c

    lax.fori_loop(0, tm, issue, 0)

    def issue_pad(q, c):
        pad_copy(q).start()
        return c

    lax.fori_loop(0, pad_per_step, issue_pad, 0)

    def drain(t, c):
        for k in range(TOP_K):
            row_copy(t, k).wait()
        return c

    lax.fori_loop(0, tm, drain, 0)

    def drain_pad(q, c):
        pad_copy(q).wait()
        return c

    lax.fori_loop(0, pad_per_step, drain_pad, 0)


def _scatter(dest_flat, pad_dest, xn, n_rows, tm=256):
    n, d = xn.shape
    steps = n // tm
    pad_per_step = pad_dest.shape[0] // steps
    return pl.pallas_call(
        functools.partial(_scatter_body, n_tokens=n, pad_per_step=pad_per_step),
        grid_spec=pltpu.PrefetchScalarGridSpec(
            num_scalar_prefetch=2,
            grid=(steps,),
            in_specs=[pl.BlockSpec((tm, d), lambda i, *_: (i, 0))],
            out_specs=pl.BlockSpec(memory_space=pl.ANY),
            scratch_shapes=[pltpu.VMEM((SUBLANES, d), F32), pltpu.SemaphoreType.DMA(())],
        ),
        out_shape=jax.ShapeDtypeStruct((n_rows, d), F32),
        compiler_params=_params(("arbitrary",)),
        name="scatter",
    )(dest_flat, pad_dest, xn)


def _gate_up_body(meta_ref, x_ref, w_ref, b_ref, o_ref, wbf):
    i = pl.program_id(1)
    n_valid = meta_ref[pl.num_programs(1)]
    ic = jnp.minimum(i, n_valid - 1)
    changed = (i == 0) | (meta_ref[ic] != meta_ref[jnp.maximum(ic - 1, 0)])
    kdim, tn = wbf.shape
    rows = 256

    @pl.when(changed & (i < n_valid))
    def _():
        def cast(c, carry):
            sl = pl.ds(pl.multiple_of(c * rows, rows), rows)
            wbf[sl, :] = w_ref[0, sl, :].astype(BF16)
            return carry

        lax.fori_loop(0, kdim // rows, cast, 0)

    @pl.when(i < n_valid)
    def _():
        x = x_ref[...].astype(BF16)
        even = lax.broadcasted_iota(jnp.int32, (x.shape[0], LANES), 1) % 2 == 0
        for c in range(tn // (2 * LANES)):
            c0 = c * 2 * LANES
            gu = jnp.dot(x, wbf[:, c0:c0 + 2 * LANES], preferred_element_type=F32)
            gu = gu + b_ref[0, :, c0:c0 + 2 * LANES]
            v1, v2 = gu[:, :LANES], gu[:, LANES:]
            gate = jnp.where(even, v1, pltpu.roll(v2, 1, axis=1))
            up = jnp.where(even, pltpu.roll(v1, LANES - 1, axis=1), v2)
            gate = jnp.minimum(gate, SWIGLU_LIMIT)
            up = jnp.clip(up, -SWIGLU_LIMIT, SWIGLU_LIMIT)
            act = gate * jax.nn.sigmoid(SWIGLU_ALPHA * gate) * (up + 1.0)
            o_ref[:, c * LANES:(c + 1) * LANES] = act.astype(BF16)

    @pl.when(i >= n_valid)
    def _():
        o_ref[...] = jnp.zeros_like(o_ref)


def _gate_up(meta, x_sorted, w_gate_up, b_gate_up, tn=1024):
    p, d = x_sorted.shape
    nblk = p // MOE_BLOCK
    cols = w_gate_up.shape[2]

    def blk(i, m):
        return jnp.minimum(i, m[nblk] - 1)

    return pl.pallas_call(
        _gate_up_body,
        grid_spec=pltpu.PrefetchScalarGridSpec(
            num_scalar_prefetch=1,
            grid=(cols // tn, nblk),
            in_specs=[
                pl.BlockSpec((MOE_BLOCK, d), lambda j, i, m: (blk(i, m), 0)),
                pl.BlockSpec((1, d, tn), lambda j, i, m: (m[blk(i, m)], 0, j)),
                pl.BlockSpec((1, 1, tn), lambda j, i, m: (m[blk(i, m)], 0, j)),
            ],
            out_specs=pl.BlockSpec((MOE_BLOCK, tn // 2), lambda j, i, m: (i, j)),
            scratch_shapes=[pltpu.VMEM((d, tn), BF16)],
        ),
        out_shape=jax.ShapeDtypeStruct((p, cols // 2), BF16),
        compiler_params=_params(("arbitrary", "arbitrary")),
        name="gate_up",
    )(meta, x_sorted, w_gate_up, b_gate_up.reshape(N_EXPERTS, 1, cols))


def _down_body(meta_ref, a_ref, w_ref, b_ref, o_ref, wbf, tmp):
    i = pl.program_id(1)
    n_valid = meta_ref[pl.num_programs(1)]
    ic = jnp.minimum(i, n_valid - 1)
    changed = (i == 0) | (meta_ref[ic] != meta_ref[jnp.maximum(ic - 1, 0)])
    kdim = wbf.shape[0]
    half = LANES // 2

    @pl.when(changed & (i < n_valid))
    def _():
        def permute(g, carry):
            base = pl.multiple_of(g * LANES, LANES)
            for c in range(tmp.shape[0]):
                cs = slice(c * LANES, (c + 1) * LANES)
                tmp[c, pl.ds(0, half, stride=2), :] = w_ref[0, pl.ds(base, half), cs]
                tmp[c, pl.ds(1, half, stride=2), :] = w_ref[0, pl.ds(base + half, half), cs]
                wbf[pl.ds(base, LANES), cs] = tmp[c].astype(BF16)
            return carry

        lax.fori_loop(0, kdim // LANES, permute, 0)

    @pl.when(i < n_valid)
    def _():
        o_ref[...] = jnp.dot(a_ref[...], wbf[...], preferred_element_type=F32) + b_ref[0]

    @pl.when(i >= n_valid)
    def _():
        o_ref[...] = jnp.zeros_like(o_ref)


def _down(meta, act, w_down, b_down, tn=1024):
    p, f = act.shape
    nblk = p // MOE_BLOCK
    d = w_down.shape[2]

    def blk(i, m):
        return jnp.minimum(i, m[nblk] - 1)

    return pl.pallas_call(
        _down_body,
        grid_spec=pltpu.PrefetchScalarGridSpec(
            num_scalar_prefetch=1,
            grid=(d // tn, nblk),
            in_specs=[
                pl.BlockSpec((MOE_BLOCK, f), lambda j, i, m: (blk(i, m), 0)),
                pl.BlockSpec((1, f, tn), lambda j, i, m: (m[blk(i, m)], 0, j)),
                pl.BlockSpec((1, 1, tn), lambda j, i, m: (m[blk(i, m)], 0, j)),
            ],
            out_specs=pl.BlockSpec((MOE_BLOCK, tn), lambda j, i, m: (i, j)),
            scratch_shapes=[pltpu.VMEM((f, tn), BF16),
                            pltpu.VMEM((tn // LANES, LANES, LANES), F32)],
        ),
        out_shape=jax.ShapeDtypeStruct((p, d), F32),
        compiler_params=_params(("arbitrary", "arbitrary")),
        name="down",
    )(meta, act, w_down, b_down.reshape(N_EXPERTS, 1, d))


def _combine_body(dest_ref, y_ref, x_ref, gate_ref, g_ref, o_ref, ybuf, sem, *, n_tokens):
    tm = x_ref.shape[0]
    i = pl.program_id(0)

    def row_copy(t, k):
        d = dest_ref[k * n_tokens + i * tm + t]
        return pltpu.make_async_copy(y_ref.at[pl.ds(d, 1), :], ybuf.at[k, pl.ds(t, 1), :], sem)

    def issue(t, c):
        for k in range(TOP_K):
            row_copy(t, k).start(priority=k % 2)
        return c

    lax.fori_loop(0, tm, issue, 0)

    def drain(t, c):
        for k in range(TOP_K):
            row_copy(t, k).wait()
        return c

    lax.fori_loop(0, tm, drain, 0)

    acc = x_ref[...]
    for k in range(TOP_K):
        acc = acc + gate_ref[:, k:k + 1] * ybuf[k]
    o_ref[...] = _rms(acc, g_ref[...])


def _combine(dest_flat, y_buf, x2, gates, g, tm=256):
    n, d = x2.shape
    return pl.pallas_call(
        functools.partial(_combine_body, n_tokens=n),
        grid_spec=pltpu.PrefetchScalarGridSpec(
            num_scalar_prefetch=1,
            grid=(n // tm,),
            in_specs=[
                pl.BlockSpec(memory_space=pl.ANY),
                pl.BlockSpec((tm, d), lambda i, *_: (i, 0)),
                pl.BlockSpec((tm, TOP_K), lambda i, *_: (i, 0)),
                pl.BlockSpec((1, d), lambda i, *_: (0, 0)),
            ],
            out_specs=pl.BlockSpec((tm, d), lambda i, *_: (i, 0)),
            scratch_shapes=[pltpu.VMEM((TOP_K, tm, d), F32), pltpu.SemaphoreType.DMA(())],
        ),
        out_shape=jax.ShapeDtypeStruct((n, d), F32),
        compiler_params=_params(("arbitrary",)),
        name="combine",
    )(dest_flat, y_buf, x2, gates, g.reshape(1, d))


def _cumsum_sublanes(x):
    row = lax.broadcasted_iota(jnp.int32, x.shape, 0)
    s = 1
    while s < x.shape[0]:
        x = x + jnp.where(row >= s, pltpu.roll(x, s, axis=0), 0)
        s *= 2
    return x


def _tables_body(cnt_ref, idx_ref, rank_ref, dest_ref, meta_ref, pad_ref, *, nblk):
    cnt = cnt_ref[...]
    padded = (cnt + (MOE_BLOCK - 1)) // MOE_BLOCK * MOE_BLOCK
    pend = _cumsum_sublanes(padded)
    pstart = pend - padded
    total = pend[N_EXPERTS - 1:N_EXPERTS, 0:1]

    def lookup(table, sel):
        eid = lax.broadcasted_iota(jnp.int32, (N_EXPERTS, sel.shape[1]), 0)
        return jnp.sum(jnp.where(eid == sel, table[:, 0:1], 0), axis=0, keepdims=True)

    def count_le(table, v):
        return jnp.sum((table[:, 0:1] <= v).astype(jnp.int32), axis=0, keepdims=True)

    width = 2048
    for k in range(TOP_K):
        for c in range(idx_ref.shape[1] // width):
            sl = slice(c * width, (c + 1) * width)
            dest_ref[k:k + 1, sl] = lookup(pstart, idx_ref[k:k + 1, sl]) + rank_ref[k:k + 1, sl]

    blk = lax.broadcasted_iota(jnp.int32, meta_ref.shape, 1)
    expert = jnp.minimum(count_le(pend, blk * MOE_BLOCK), N_EXPERTS - 1)
    meta_ref[...] = jnp.where(blk == nblk, total // MOE_BLOCK, expert)

    gap = padded - cnt
    gap_end = _cumsum_sublanes(gap)
    q = lax.broadcasted_iota(jnp.int32, pad_ref.shape, 1)
    e = count_le(gap_end, q)
    inside = lookup(pstart + cnt - (gap_end - gap), e) + q
    tail = total + q - gap_end[N_EXPERTS - 1:N_EXPERTS, 0:1]
    pad_ref[...] = jnp.where(e < N_EXPERTS, inside, tail)


def _tables(cnt, idx_t, rank_t, n_rows):
    nblk = n_rows // MOE_BLOCK
    n_pad = n_rows - idx_t.size
    meta_w = (nblk + 1 + LANES - 1) // LANES * LANES
    dest, meta, pad_dest = pl.pallas_call(
        functools.partial(_tables_body, nblk=nblk),
        out_shape=[
            jax.ShapeDtypeStruct(idx_t.shape, jnp.int32),
            jax.ShapeDtypeStruct((1, meta_w), jnp.int32),
            jax.ShapeDtypeStruct((1, n_pad), jnp.int32),
        ],
        name="tables",
    )(cnt, idx_t, rank_t)
    return dest.reshape(-1), meta.reshape(-1), pad_dest.reshape(-1)


def kernel(x, norm_mix, w_in, conv_w, conv_b, w_a, b_a, w_x, b_x, lru_lambda, attn_out_norm, lru_out_norm, w_out, norm_ffn, w_router, b_router, w_gate_up, b_gate_up, w_down, b_down, norm_final):
    b, seq, d = x.shape
    n = b * seq
    x2 = x.reshape(n, d)

    proj = _in_proj(x2, norm_mix, w_in)
    proj3 = proj.reshape(b, seq, IN_COLS)

    slopes = jnp.asarray(2.0 ** (-8.0 * np.arange(1, N_HEADS + 1) / N_HEADS), F32)
    attn = _attention(proj3, slopes)

    def pair_blocks(w):
        w4 = w.reshape(-1, 2, LRU_BLOCK, LRU_BLOCK)
        z = jnp.zeros_like(w4[:, 0])
        top = jnp.concatenate([w4[:, 0], z], axis=2)
        bot = jnp.concatenate([z, w4[:, 1]], axis=2)
        return jnp.concatenate([top, bot], axis=1)

    w_bd = jnp.concatenate([pair_blocks(w_a), pair_blocks(w_x)], axis=2)
    rec = _rglru(proj3, conv_w, conv_b, w_bd, b_a, b_x, lru_lambda)

    x_mid = _out_proj(attn.reshape(n, ATTN_WIDTH), rec.reshape(n, LRU_WIDTH),
                      attn_out_norm, lru_out_norm, x2, w_out)

    xn, idx_t, gate_t, rank_t, cnt = _router(x_mid, norm_ffn, w_router.T, b_router)
    n_rows = n * TOP_K + N_EXPERTS * MOE_BLOCK
    dest_flat, meta, pad_dest = _tables(cnt, idx_t, rank_t, n_rows)

    x_sorted = _scatter(dest_flat, pad_dest, xn, n_rows)
    act = _gate_up(meta, x_sorted, w_gate_up, b_gate_up)
    y_buf = _down(meta, act, w_down, b_down)
    out = _combine(dest_flat, y_buf, x_mid, gate_t.T, norm_final)
    return out.reshape(b, seq, d)
```

```python
import functools

import numpy as np
import jax
import jax.numpy as jnp
from jax import lax
from jax.experimental import pallas as pl
from jax.experimental.pallas import tpu as pltpu

F32 = jnp.float32
BF16 = jnp.bfloat16

D_MODEL = 2048
HEAD_DIM = 64
N_HEADS = 16
ATTN_WIDTH = N_HEADS * HEAD_DIM
LRU_WIDTH = D_MODEL - ATTN_WIDTH
LRU_BLOCK = 64
CONV_WIDTH = 4
LRU_C = 8.0
IN_COLS = 3 * ATTN_WIDTH + 2 * LRU_WIDTH
DILATIONS = (1, 4, 16)
ATTN_BLOCK = 128
UNITS_PER_TRIP = (5, 6, 8)
N_EXPERTS = 32
TOP_K = 4
D_FF = D_MODEL
SWIGLU_LIMIT = 7.0
SWIGLU_ALPHA = 1.702
MOE_BLOCK = 256
EPS = 1e-6

LANES = 128
SUBLANES = 8
VMEM_LIMIT = 56 * 1024 * 1024


def _params(sem, vmem=VMEM_LIMIT):
    return pltpu.CompilerParams(dimension_semantics=sem, vmem_limit_bytes=vmem)


def _rms(xf, g):
    return xf * lax.rsqrt(jnp.mean(xf * xf, axis=-1, keepdims=True) + EPS) * g


def _in_proj_body(x_ref, g_ref, w_ref, o_ref, h_ref):
    @pl.when(pl.program_id(1) == 0)
    def _():
        h_ref[...] = _rms(x_ref[...], g_ref[...]).astype(BF16)

    o_ref[...] = jnp.dot(h_ref[...], w_ref[...].astype(BF16), preferred_element_type=F32)


def _in_proj(x2, g, w_in, tm=1024, tn=512):
    n, d = x2.shape
    cols = w_in.shape[1]
    return pl.pallas_call(
        _in_proj_body,
        grid=(n // tm, cols // tn),
        in_specs=[
            pl.BlockSpec((tm, d), lambda i, j: (i, 0)),
            pl.BlockSpec((1, d), lambda i, j: (0, 0)),
            pl.BlockSpec((d, tn), lambda i, j: (0, j)),
        ],
        out_specs=pl.BlockSpec((tm, tn), lambda i, j: (i, j)),
        out_shape=jax.ShapeDtypeStruct((n, cols), F32),
        scratch_shapes=[pltpu.VMEM((tm, d), BF16)],
        compiler_params=_params(("parallel", "arbitrary")),
        name="in_proj",
    )(x2, g.reshape(1, d), w_in)


REGROUP = 4


def _attention_body(slope_ref, q_ref, k_ref, v_ref, o_ref, qh, kh, vh, acc, mx):
    seq = q_ref.shape[1]
    run = seq // REGROUP
    pair = pl.program_id(1)
    qi = lax.broadcasted_iota(jnp.int32, (ATTN_BLOCK, 2 * ATTN_BLOCK), 0)
    ki = lax.broadcasted_iota(jnp.int32, (ATTN_BLOCK, 2 * ATTN_BLOCK), 1)
    rel2 = qi + ATTN_BLOCK - ki
    ok2 = (rel2 >= 0) & (rel2 <= ATTN_BLOCK)
    rel1 = (qi - ki)[:, :ATTN_BLOCK]
    ok1 = rel1 >= 0

    nh = LANES // HEAD_DIM
    slopes = [slope_ref[pair * nh + hh] for hh in range(nh)]
    def stage(order, rows_dst, rows_src):
        lane = lax.broadcasted_iota(jnp.int32, (run, LANES), 1)
        qv, vv = q_ref[0, rows_src, :] * (HEAD_DIM ** -0.5), v_ref[0, rows_src, :]
        if order:
            kh[rows_dst, :] = k_ref[0, rows_src, :]
        for hh in range(nh):
            mine = (lane >= hh * HEAD_DIM) & (lane < (hh + 1) * HEAD_DIM)
            qh[order, hh, rows_dst, :] = jnp.where(mine, qv, 0.0)
            vh[order, hh, rows_dst, :] = jnp.where(mine, vv, 1.0)

    for c in range(REGROUP):
        stage(0, slice(c * run, (c + 1) * run), pl.ds(c * run, run))
        stage(1, slice(c * run, (c + 1) * run), pl.ds(c, run, stride=REGROUP))

    def run_units(units):
        loaded = [(qh[order, hh, qs, :].astype(BF16),
                   (kh[ks, :] if order else k_ref[0, ks, :]).astype(BF16),
                   vh[order, hh, ks, :].astype(BF16)) for hh, br, order, qs, ks, bias in units]
        scores = [lax.dot_general(q, k, (((1,), (1,)), ((), ())), preferred_element_type=F32)
                  + u[5] for (q, k, v), u in zip(loaded, units)]
        maxes = [jnp.max(s, axis=1, keepdims=True) for s in scores]
        probs = [jnp.exp(s - m).astype(BF16) for s, m in zip(scores, maxes)]
        results = [(jnp.dot(p, v, preferred_element_type=F32), m)
                   for p, (q, k, v), m in zip(probs, loaded, maxes)]
        for (a, m), (hh, br, order, qs, ks, bias) in zip(results, units):
            acc[hh, br, qs, :] = a
            mx[hh, br, qs, :] = jnp.broadcast_to(m, (ATTN_BLOCK, LANES))

    def bias_pair(ok, rel, d):
        return [jnp.where(ok, -sl * (rel * d).astype(F32), -jnp.inf) for sl in slopes]

    def contiguous_branch(br, order, d, n_runs, per_trip):
        nb = seq // (n_runs * ATTN_BLOCK)
        bias1, bias2 = bias_pair(ok1, rel1, d), bias_pair(ok2, rel2, d)
        run_units([(hh, br, order, pl.ds(r * nb * ATTN_BLOCK, ATTN_BLOCK),
                    pl.ds(r * nb * ATTN_BLOCK, ATTN_BLOCK), bias1[hh])
                   for r in range(n_runs) for hh in range(nh)])

        def later(g, carry):
            units = []
            for u in range(per_trip):
                idx = g * per_trip + u
                blk = idx // (nb - 1) * nb + idx % (nb - 1) + 1
                start = pl.multiple_of(blk * ATTN_BLOCK, ATTN_BLOCK)
                units += [(hh, br, order, pl.ds(start, ATTN_BLOCK),
                           pl.ds(start - ATTN_BLOCK, 2 * ATTN_BLOCK), bias2[hh])
                          for hh in range(nh)]
            run_units(units)
            return carry

        lax.fori_loop(0, n_runs * (nb - 1) // per_trip, later, 0)

    contiguous_branch(0, 0, DILATIONS[0], 1, UNITS_PER_TRIP[0])
    contiguous_branch(1, 1, DILATIONS[1], REGROUP, UNITS_PER_TRIP[1])

    d2 = DILATIONS[2]
    sub = d2 // REGROUP
    bias16 = bias_pair(ok1, rel1, d2)

    def strided(g, carry):
        units = []
        for u in range(UNITS_PER_TRIP[2]):
            idx = g * UNITS_PER_TRIP[2] + u
            rows = pl.ds(idx // sub * run + idx % sub, ATTN_BLOCK, stride=sub)
            units += [(hh, 2, 1, rows, rows, bias16[hh]) for hh in range(nh)]
        run_units(units)
        return carry

    lax.fori_loop(0, d2 // UNITS_PER_TRIP[2], strided, 0)

    def merge(c, carry):
        start = c // (run // ATTN_BLOCK) + c % (run // ATTN_BLOCK) * (ATTN_BLOCK * REGROUP)
        nat = pl.ds(start, ATTN_BLOCK, stride=REGROUP)
        reg = pl.ds(pl.multiple_of(c * ATTN_BLOCK, ATTN_BLOCK), ATTN_BLOCK)
        outs = []
        for hh in range(nh):
            m0, m1, m2 = mx[hh, 0, nat, :], mx[hh, 1, reg, :], mx[hh, 2, reg, :]
            mt = jnp.maximum(jnp.maximum(m0, m1), m2)
            tot = (jnp.exp(m0 - mt) * acc[hh, 0, nat, :] + jnp.exp(m1 - mt) * acc[hh, 1, reg, :]
                   + jnp.exp(m2 - mt) * acc[hh, 2, reg, :])
            den = pltpu.roll(tot, HEAD_DIM, axis=1)
            outs.append(tot / den)
        lane_r = lax.broadcasted_iota(jnp.int32, (ATTN_BLOCK, LANES), 1)
        o_ref[0, nat, :] = jnp.where(lane_r < HEAD_DIM, outs[0], outs[1])
        return carry

    lax.fori_loop(0, seq // ATTN_BLOCK, merge, 0, unroll=2)


def _attention(proj3, slopes):
    b, seq, _ = proj3.shape
    npair = ATTN_WIDTH // LANES
    nh = LANES // HEAD_DIM
    blk = (1, seq, LANES)
    return pl.pallas_call(
        _attention_body,
        grid_spec=pltpu.PrefetchScalarGridSpec(
            num_scalar_prefetch=1,
            grid=(b, npair),
            in_specs=[
                pl.BlockSpec(blk, lambda i, j, s: (i, 0, j)),
                pl.BlockSpec(blk, lambda i, j, s: (i, 0, npair + j)),
                pl.BlockSpec(blk, lambda i, j, s: (i, 0, 2 * npair + j)),
            ],
            out_specs=pl.BlockSpec(blk, lambda i, j, s: (i, 0, j)),
            scratch_shapes=[
                pltpu.VMEM((2, nh, seq, LANES), F32),
                pltpu.VMEM((seq, LANES), F32),
                pltpu.VMEM((2, nh, seq, LANES), F32),
                pltpu.VMEM((nh, len(DILATIONS), seq, LANES), F32),
                pltpu.VMEM((nh, len(DILATIONS), seq, LANES), F32),
            ],
        ),
        out_shape=jax.ShapeDtypeStruct((b, seq, ATTN_WIDTH), F32),
        compiler_params=_params(("parallel", "parallel")),
        name="attention",
    )(slopes, proj3, proj3, proj3)


def _rglru_body(xr_ref, gr_ref, cw_ref, cb_ref, w_ref, ba_ref, bx_ref, lam_ref, o_ref,
                xp, a_s, b_s):
    seq = xr_ref.shape[1]
    pad = SUBLANES
    xp[0:pad, :] = jnp.zeros((pad, LANES), F32)
    xp[pad:pad + seq, :] = xr_ref[0]
    lam = lam_ref[...]
    sp = jnp.maximum(-lam, 0.0) + jnp.log(1.0 + jnp.exp(-jnp.abs(lam)))
    w_hi = w_ref[0].astype(BF16)
    w_lo = (w_ref[0] - w_hi.astype(F32)).astype(BF16)
    rows = 256

    def gates(c, carry):
        base = pl.multiple_of(c * rows, rows)
        xc = cb_ref[...] + cw_ref[0:1, :] * xp[pl.ds(base + pad - 3, rows), :]
        for i in range(1, CONV_WIDTH):
            xc = xc + cw_ref[i:i + 1, :] * xp[pl.ds(base + pad - 3 + i, rows), :]
        hi = xc.astype(BF16)
        lo = (xc - hi.astype(F32)).astype(BF16)
        pre = (jnp.dot(hi, w_hi, preferred_element_type=F32)
               + jnp.dot(lo, w_hi, preferred_element_type=F32)
               + jnp.dot(hi, w_lo, preferred_element_type=F32))
        r = jax.nn.sigmoid(pre[:, :LANES] + ba_ref[...])
        ig = jax.nn.sigmoid(pre[:, LANES:] + bx_ref[...])
        log_a = -LRU_C * r * sp
        a = jnp.exp(log_a)
        t = jnp.tanh(log_a)
        b = jnp.sqrt(-2.0 * t / (1.0 - t)) * (ig * xc)
        a_s[pl.ds(base, rows), :] = a
        b_s[pl.ds(base, rows), :] = b
        return carry

    lax.fori_loop(0, seq // rows, gates, 0)

    row = lax.broadcasted_iota(jnp.int32, (SUBLANES, LANES), 0)

    def scan(c, h_prev):
        sl = pl.ds(pl.multiple_of(c * SUBLANES, SUBLANES), SUBLANES)
        a = a_s[sl, :]
        b = b_s[sl, :]
        for s in (1, 2, 4):
            keep = row >= s
            a_sh = jnp.where(keep, pltpu.roll(a, s, axis=0), 1.0)
            b_sh = jnp.where(keep, pltpu.roll(b, s, axis=0), 0.0)
            b = a * b_sh + b
            a = a * a_sh
        h = a * h_prev + b
        o_ref[0, sl, :] = h * jax.nn.gelu(gr_ref[0, sl, :])
        return jnp.broadcast_to(h[SUBLANES - 1:SUBLANES, :], (SUBLANES, LANES))

    lax.fori_loop(0, seq // SUBLANES, scan, jnp.zeros((SUBLANES, LANES), F32), unroll=8)


def _rglru(proj3, conv_w, conv_b, w_bd, b_a, b_x, lam):
    b, seq, _ = proj3.shape
    nt = LRU_WIDTH // LANES
    xr0 = 3 * ATTN_WIDTH // LANES
    gr0 = xr0 + nt
    blk = (1, seq, LANES)
    vec = lambda: pl.BlockSpec((1, LANES), lambda i, j: (0, j))
    return pl.pallas_call(
        _rglru_body,
        grid=(b, nt),
        in_specs=[
            pl.BlockSpec(blk, lambda i, j: (i, 0, xr0 + j)),
            pl.BlockSpec(blk, lambda i, j: (i, 0, gr0 + j)),
            pl.BlockSpec((CONV_WIDTH, LANES), lambda i, j: (0, j)),
            vec(),
            pl.BlockSpec((1, LANES, 2 * LANES), lambda i, j: (j, 0, 0)),
            vec(), vec(), vec(),
        ],
        out_specs=pl.BlockSpec(blk, lambda i, j: (i, 0, j)),
        out_shape=jax.ShapeDtypeStruct((b, seq, LRU_WIDTH), F32),
        scratch_shapes=[
            pltpu.VMEM((seq + SUBLANES, LANES), F32),
            pltpu.VMEM((seq, LANES), F32),
            pltpu.VMEM((seq, LANES), F32),
        ],
        compiler_params=_params(("parallel", "parallel")),
        name="rglru",
    )(proj3, proj3, conv_w, conv_b.reshape(1, -1), w_bd, b_a.reshape(1, -1),
      b_x.reshape(1, -1), lam.reshape(1, -1))


def _out_proj_body(at_ref, rc_ref, ga_ref, gr_ref, x_ref, w_ref, o_ref, h_ref):
    wa = at_ref.shape[1]

    @pl.when(pl.program_id(1) == 0)
    def _():
        h_ref[:, :wa] = _rms(at_ref[...], ga_ref[...]).astype(BF16)
        h_ref[:, wa:] = _rms(rc_ref[...], gr_ref[...]).astype(BF16)

    o_ref[...] = x_ref[...] + jnp.dot(h_ref[...], w_ref[...].astype(BF16),
                                      preferred_element_type=F32)


def _out_proj(attn2, rec2, g_attn, g_rec, x2, w_out, tm=512, tn=1024):
    n, d = x2.shape
    wa, wr = attn2.shape[1], rec2.shape[1]
    return pl.pallas_call(
        _out_proj_body,
        grid=(n // tm, d // tn),
        in_specs=[
            pl.BlockSpec((tm, wa), lambda i, j: (i, 0)),
            pl.BlockSpec((tm, wr), lambda i, j: (i, 0)),
            pl.BlockSpec((1, wa), lambda i, j: (0, 0)),
            pl.BlockSpec((1, wr), lambda i, j: (0, 0)),
            pl.BlockSpec((tm, tn), lambda i, j: (i, j)),
            pl.BlockSpec((d, tn), lambda i, j: (0, j)),
        ],
        out_specs=pl.BlockSpec((tm, tn), lambda i, j: (i, j)),
        out_shape=jax.ShapeDtypeStruct((n, d), F32),
        scratch_shapes=[pltpu.VMEM((tm, d), BF16)],
        compiler_params=_params(("parallel", "arbitrary")),
        name="out_proj",
    )(attn2, rec2, g_attn.reshape(1, wa), g_rec.reshape(1, wr), x2, w_out)


def _router_body(x_ref, g_ref, wt_ref, b_ref, xn_ref, idx_ref, gate_ref, rank_ref, cnt_ref,
                 base):
    tm = x_ref.shape[0]

    @pl.when(pl.program_id(0) == 0)
    def _():
        base[...] = jnp.zeros_like(base)

    xn = _rms(x_ref[...], g_ref[...])
    xn_ref[...] = xn
    x_hi = xn.astype(BF16)
    x_lo = (xn - x_hi.astype(F32)).astype(BF16)
    w = wt_ref[...]
    w_hi = w.astype(BF16)
    w_lo = (w - w_hi.astype(F32)).astype(BF16)
    nt = (((1,), (1,)), ((), ()))
    logits = (lax.dot_general(w_hi, x_hi, nt, preferred_element_type=F32)
              + lax.dot_general(w_hi, x_lo, nt, preferred_element_type=F32)
              + lax.dot_general(w_lo, x_hi, nt, preferred_element_type=F32)
              + b_ref[...])

    eid = lax.broadcasted_iota(jnp.int32, (N_EXPERTS, tm), 0)
    work = logits
    vals, hots = [], []
    for _ in range(TOP_K):
        best = jnp.max(work, axis=0, keepdims=True)
        pick = jnp.min(jnp.where(work == best, eid, N_EXPERTS), axis=0, keepdims=True)
        hot = eid == pick
        vals.append(best)
        hots.append(hot)
        work = jnp.where(hot, -jnp.inf, work)
        idx_ref[len(vals) - 1:len(vals), :] = pick

    ex = [jnp.exp(v - vals[0]) for v in vals]
    den = ex[0] + ex[1] + ex[2] + ex[3]
    for k in range(TOP_K):
        gate_ref[k:k + 1, :] = ex[k] / den

    chosen = (hots[0] | hots[1] | hots[2] | hots[3])
    si = lax.broadcasted_iota(jnp.int32, (tm, tm), 0)
    ti = lax.broadcasted_iota(jnp.int32, (tm, tm), 1)
    before = (si < ti).astype(BF16)
    prefix = jnp.dot(chosen.astype(BF16), before, preferred_element_type=F32)
    slot = base[:, 0:1] + prefix
    for k in range(TOP_K):
        rank_ref[k:k + 1, :] = jnp.sum(jnp.where(hots[k], slot, 0.0), axis=0,
                                       keepdims=True).astype(jnp.int32)
    base[...] = base[...] + jnp.sum(chosen.astype(F32), axis=1, keepdims=True)
    cnt_ref[...] = base[...].astype(jnp.int32)


def _router(x2, g, w_router_t, b_router, tm=256):
    n, d = x2.shape
    row = lambda: pl.BlockSpec((TOP_K, tm), lambda i: (0, i))
    return pl.pallas_call(
        _router_body,
        grid=(n // tm,),
        in_specs=[
            pl.BlockSpec((tm, d), lambda i: (i, 0)),
            pl.BlockSpec((1, d), lambda i: (0, 0)),
            pl.BlockSpec((N_EXPERTS, d), lambda i: (0, 0)),
            pl.BlockSpec((N_EXPERTS, 1), lambda i: (0, 0)),
        ],
        out_specs=[
            pl.BlockSpec((tm, d), lambda i: (i, 0)),
            row(), row(), row(),
            pl.BlockSpec((N_EXPERTS, LANES), lambda i: (0, 0)),
        ],
        out_shape=[
            jax.ShapeDtypeStruct((n, d), F32),
            jax.ShapeDtypeStruct((TOP_K, n), jnp.int32),
            jax.ShapeDtypeStruct((TOP_K, n), F32),
            jax.ShapeDtypeStruct((TOP_K, n), jnp.int32),
            jax.ShapeDtypeStruct((N_EXPERTS, LANES), jnp.int32),
        ],
        scratch_shapes=[pltpu.VMEM((N_EXPERTS, LANES), F32)],
        compiler_params=_params(("arbitrary",)),
        name="router",
    )(x2, g.reshape(1, d), w_router_t, b_router.reshape(N_EXPERTS, 1))


def _scatter_body(dest_ref, pad_ref, xn_ref, o_ref, zrow, sem, *, n_tokens, pad_per_step):
    tm = xn_ref.shape[0]
    i = pl.program_id(0)

    @pl.when(i == 0)
    def _():
        zrow[...] = jnp.zeros_like(zrow)

    def row_copy(t, k):
        d = dest_ref[k * n_tokens + i * tm + t]
        return pltpu.make_async_copy(xn_ref.at[pl.ds(t, 1), :], o_ref.at[pl.ds(d, 1), :], sem)

    def pad_copy(q):
        d = pad_ref[i * pad_per_step + q]
        return pltpu.make_async_copy(zrow.at[pl.ds(0, 1), :], o_ref.at[pl.ds(d, 1), :], sem)

    def issue(t, c):
        for k in range(TOP_K):
            row_copy(t, k).start(priority=k % 2)
        return c

    lax.fori_loop(0, tm, issue, 0)

    def issue_pad(q, c):
        pad_copy(q).start()
        return c

    lax.fori_loop(0, pad_per_step, issue_pad, 0)

    def drain(t, c):
        for k in range(TOP_K):
            row_copy(t, k).wait()
        return c

    lax.fori_loop(0, tm, drain, 0)

    def drain_pad(q, c):
        pad_copy(q).wait()
        return c

    lax.fori_loop(0, pad_per_step, drain_pad, 0)


def _scatter(dest_flat, pad_dest, xn, n_rows, tm=256):
    n, d = xn.shape
    steps = n // tm
    pad_per_step = pad_dest.shape[0] // steps
    return pl.pallas_call(
        functools.partial(_scatter_body, n_tokens=n, pad_per_step=pad_per_step),
        grid_spec=pltpu.PrefetchScalarGridSpec(
            num_scalar_prefetch=2,
            grid=(steps,),
            in_specs=[pl.BlockSpec((tm, d), lambda i, *_: (i, 0))],
            out_specs=pl.BlockSpec(memory_space=pl.ANY),
            scratch_shapes=[pltpu.VMEM((SUBLANES, d), F32), pltpu.SemaphoreType.DMA(())],
        ),
        out_shape=jax.ShapeDtypeStruct((n_rows, d), F32),
        compiler_params=_params(("arbitrary",)),
        name="scatter",
    )(dest_flat, pad_dest, xn)


def _gate_up_body(meta_ref, x_ref, w_ref, b_ref, o_ref, wbf):
    i = pl.program_id(1)
    n_valid = meta_ref[pl.num_programs(1)]
    ic = jnp.minimum(i, n_valid - 1)
    changed = (i == 0) | (meta_ref[ic] != meta_ref[jnp.maximum(ic - 1, 0)])
    kdim, tn = wbf.shape
    rows = 256

    @pl.when(changed & (i < n_valid))
    def _():
        def cast(c, carry):
            sl = pl.ds(pl.multiple_of(c * rows, rows), rows)
            wbf[sl, :] = w_ref[0, sl, :].astype(BF16)
            return carry

        lax.fori_loop(0, kdim // rows, cast, 0)

    @pl.when(i < n_valid)
    def _():
        x = x_ref[...].astype(BF16)
        even = lax.broadcasted_iota(jnp.int32, (x.shape[0], LANES), 1) % 2 == 0
        for c in range(tn // (2 * LANES)):
            c0 = c * 2 * LANES
            gu = jnp.dot(x, wbf[:, c0:c0 + 2 * LANES], preferred_element_type=F32)
            gu = gu + b_ref[0, :, c0:c0 + 2 * LANES]
            v1, v2 = gu[:, :LANES], gu[:, LANES:]
            gate = jnp.where(even, v1, pltpu.roll(v2, 1, axis=1))
            up = jnp.where(even, pltpu.roll(v1, LANES - 1, axis=1), v2)
            gate = jnp.minimum(gate, SWIGLU_LIMIT)
            up = jnp.clip(up, -SWIGLU_LIMIT, SWIGLU_LIMIT)
            act = gate * jax.nn.sigmoid(SWIGLU_ALPHA * gate) * (up + 1.0)
            o_ref[:, c * LANES:(c + 1) * LANES] = act.astype(BF16)

    @pl.when(i >= n_valid)
    def _():
        o_ref[...] = jnp.zeros_like(o_ref)


def _gate_up(meta, x_sorted, w_gate_up, b_gate_up, tn=1024):
    p, d = x_sorted.shape
    nblk = p // MOE_BLOCK
    cols = w_gate_up.shape[2]

    def blk(i, m):
        return jnp.minimum(i, m[nblk] - 1)

    return pl.pallas_call(
        _gate_up_body,
        grid_spec=pltpu.PrefetchScalarGridSpec(
            num_scalar_prefetch=1,
            grid=(cols // tn, nblk),
            in_specs=[
                pl.BlockSpec((MOE_BLOCK, d), lambda j, i, m: (blk(i, m), 0)),
                pl.BlockSpec((1, d, tn), lambda j, i, m: (m[blk(i, m)], 0, j)),
                pl.BlockSpec((1, 1, tn), lambda j, i, m: (m[blk(i, m)], 0, j)),
            ],
            out_specs=pl.BlockSpec((MOE_BLOCK, tn // 2), lambda j, i, m: (i, j)),
            scratch_shapes=[pltpu.VMEM((d, tn), BF16)],
        ),
        out_shape=jax.ShapeDtypeStruct((p, cols // 2), BF16),
        compiler_params=_params(("arbitrary", "arbitrary")),
        name="gate_up",
    )(meta, x_sorted, w_gate_up, b_gate_up.reshape(N_EXPERTS, 1, cols))


def _down_body(meta_ref, a_ref, w_ref, b_ref, o_ref, wbf, tmp):
    i = pl.program_id(1)
    n_valid = meta_ref[pl.num_programs(1)]
    ic = jnp.minimum(i, n_valid - 1)
    changed = (i == 0) | (meta_ref[ic] != meta_ref[jnp.maximum(ic - 1, 0)])
    kdim = wbf.shape[0]
    half = LANES // 2

    @pl.when(changed & (i < n_valid))
    def _():
        def permute(g, carry):
            base = pl.multiple_of(g * LANES, LANES)
            for c in range(tmp.shape[0]):
                cs = slice(c * LANES, (c + 1) * LANES)
                tmp[c, pl.ds(0, half, stride=2), :] = w_ref[0, pl.ds(base, half), cs]
                tmp[c, pl.ds(1, half, stride=2), :] = w_ref[0, pl.ds(base + half, half), cs]
                wbf[pl.ds(base, LANES), cs] = tmp[c].astype(BF16)
            return carry

        lax.fori_loop(0, kdim // LANES, permute, 0)

    @pl.when(i < n_valid)
    def _():
        o_ref[...] = jnp.dot(a_ref[...], wbf[...], preferred_element_type=F32) + b_ref[0]

    @pl.when(i >= n_valid)
    def _():
        o_ref[...] = jnp.zeros_like(o_ref)


def _down(meta, act, w_down, b_down, tn=1024):
    p, f = act.shape
    nblk = p // MOE_BLOCK
    d = w_down.shape[2]

    def blk(i, m):
        return jnp.minimum(i, m[nblk] - 1)

    return pl.pallas_call(
        _down_body,
        grid_spec=pltpu.PrefetchScalarGridSpec(
            num_scalar_prefetch=1,
            grid=(d // tn, nblk),
            in_specs=[
                pl.BlockSpec((MOE_BLOCK, f), lambda j, i, m: (blk(i, m), 0)),
                pl.BlockSpec((1, f, tn), lambda j, i, m: (m[blk(i, m)], 0, j)),
                pl.BlockSpec((1, 1, tn), lambda j, i, m: (m[blk(i, m)], 0, j)),
            ],
            out_specs=pl.BlockSpec((MOE_BLOCK, tn), lambda j, i, m: (i, j)),
            scratch_shapes=[pltpu.VMEM((f, tn), BF16),
                            pltpu.VMEM((tn // LANES, LANES, LANES), F32)],
        ),
        out_shape=jax.ShapeDtypeStruct((p, d), F32),
        compiler_params=_params(("arbitrary", "arbitrary")),
        name="down",
    )(meta, act, w_down, b_down.reshape(N_EXPERTS, 1, d))


def _combine_body(dest_ref, y_ref, x_ref, gate_ref, g_ref, o_ref, ybuf, sem, *, n_tokens):
    tm = x_ref.shape[0]
    i = pl.program_id(0)

    def row_copy(t, k):
        d = dest_ref[k * n_tokens + i * tm + t]
        return pltpu.make_async_copy(y_ref.at[pl.ds(d, 1), :], ybuf.at[k, pl.ds(t, 1), :], sem)

    def issue(t, c):
        for k in range(TOP_K):
            row_copy(t, k).start(priority=k % 2)
        return c

    lax.fori_loop(0, tm, issue, 0)

    def drain(t, c):
        for k in range(TOP_K):
            row_copy(t, k).wait()
        return c

    lax.fori_loop(0, tm, drain, 0)

    acc = x_ref[...]
    for k in range(TOP_K):
        acc = acc + gate_ref[:, k:k + 1] * ybuf[k]
    o_ref[...] = _rms(acc, g_ref[...])


def _combine(dest_flat, y_buf, x2, gates, g, tm=256):
    n, d = x2.shape
    return pl.pallas_call(
        functools.partial(_combine_body, n_tokens=n),
        grid_spec=pltpu.PrefetchScalarGridSpec(
            num_scalar_prefetch=1,
            grid=(n // tm,),
            in_specs=[
                pl.BlockSpec(memory_space=pl.ANY),
                pl.BlockSpec((tm, d), lambda i, *_: (i, 0)),
                pl.BlockSpec((tm, TOP_K), lambda i, *_: (i, 0)),
                pl.BlockSpec((1, d), lambda i, *_: (0, 0)),
            ],
            out_specs=pl.BlockSpec((tm, d), lambda i, *_: (i, 0)),
            scratch_shapes=[pltpu.VMEM((TOP_K, tm, d), F32), pltpu.SemaphoreType.DMA(())],
        ),
        out_shape=jax.ShapeDtypeStruct((n, d), F32),
        compiler_params=_params(("arbitrary",)),
        name="combine",
    )(dest_flat, y_buf, x2, gates, g.reshape(1, d))


def _cumsum_sublanes(x):
    row = lax.broadcasted_iota(jnp.int32, x.shape, 0)
    s = 1
    while s < x.shape[0]:
        x = x + jnp.where(row >= s, pltpu.roll(x, s, axis=0), 0)
        s *= 2
    return x


def _tables_body(cnt_ref, idx_ref, rank_ref, dest_ref, meta_ref, pad_ref, *, nblk):
    cnt = cnt_ref[...]
    padded = (cnt + (MOE_BLOCK - 1)) // MOE_BLOCK * MOE_BLOCK
    pend = _cumsum_sublanes(padded)
    pstart = pend - padded
    total = pend[N_EXPERTS - 1:N_EXPERTS, 0:1]

    def lookup(table, sel):
        eid = lax.broadcasted_iota(jnp.int32, (N_EXPERTS, sel.shape[1]), 0)
        return jnp.sum(jnp.where(eid == sel, table[:, 0:1], 0), axis=0, keepdims=True)

    def count_le(table, v):
        return jnp.sum((table[:, 0:1] <= v).astype(jnp.int32), axis=0, keepdims=True)

    width = 2048
    for k in range(TOP_K):
        for c in range(idx_ref.shape[1] // width):
            sl = slice(c * width, (c + 1) * width)
            dest_ref[k:k + 1, sl] = lookup(pstart, idx_ref[k:k + 1, sl]) + rank_ref[k:k + 1, sl]

    blk = lax.broadcasted_iota(jnp.int32, meta_ref.shape, 1)
    expert = jnp.minimum(count_le(pend, blk * MOE_BLOCK), N_EXPERTS - 1)
    meta_ref[...] = jnp.where(blk == nblk, total // MOE_BLOCK, expert)

    gap = padded - cnt
    gap_end = _cumsum_sublanes(gap)
    q = lax.broadcasted_iota(jnp.int32, pad_ref.shape, 1)
    e = count_le(gap_end, q)
    inside = lookup(pstart + cnt - (gap_end - gap), e) + q
    tail = total + q - gap_end[N_EXPERTS - 1:N_EXPERTS, 0:1]
    pad_ref[...] = jnp.where(e < N_EXPERTS, inside, tail)


def _tables(cnt, idx_t, rank_t, n_rows):
    nblk = n_rows // MOE_BLOCK
    n_pad = n_rows - idx_t.size
    meta_w = (nblk + 1 + LANES - 1) // LANES * LANES
    dest, meta, pad_dest = pl.pallas_call(
        functools.partial(_tables_body, nblk=nblk),
        out_shape=[
            jax.ShapeDtypeStruct(idx_t.shape, jnp.int32),
            jax.ShapeDtypeStruct((1, meta_w), jnp.int32),
            jax.ShapeDtypeStruct((1, n_pad), jnp.int32),
        ],
        name="tables",
    )(cnt, idx_t, rank_t)
    return dest.reshape(-1), meta.reshape(-1), pad_dest.reshape(-1)


def kernel(x, norm_mix, w_in, conv_w, conv_b, w_a, b_a, w_x, b_x, lru_lambda, attn_out_norm, lru_out_norm, w_out, norm_ffn, w_router, b_router, w_gate_up, b_gate_up, w_down, b_down, norm_final):
    b, seq, d = x.shape
    n = b * seq
    x2 = x.reshape(n, d)

    proj = _in_proj(x2, norm_mix, w_in)
    proj3 = proj.reshape(b, seq, IN_COLS)

    slopes = jnp.asarray(2.0 ** (-8.0 * np.arange(1, N_HEADS + 1) / N_HEADS), F32)
    attn = _attention(proj3, slopes)

    def pair_blocks(w):
        w4 = w.reshape(-1, 2, LRU_BLOCK, LRU_BLOCK)
        z = jnp.zeros_like(w4[:, 0])
        top = jnp.concatenate([w4[:, 0], z], axis=2)
        bot = jnp.concatenate([z, w4[:, 1]], axis=2)
        return jnp.concatenate([top, bot], axis=1)

    w_bd = jnp.concatenate([pair_blocks(w_a), pair_blocks(w_x)], axis=2)
    rec = _rglru(proj3, conv_w, conv_b, w_bd, b_a, b_x, lru_lambda)

    x_mid = _out_proj(attn.reshape(n, ATTN_WIDTH), rec.reshape(n, LRU_WIDTH),
                      attn_out_norm, lru_out_norm, x2, w_out)

    xn, idx_t, gate_t, rank_t, cnt = _router(x_mid, norm_ffn, w_router.T, b_router)
    n_rows = n * TOP_K + N_EXPERTS * MOE_BLOCK
    dest_flat, meta, pad_dest = _tables(cnt, idx_t, rank_t, n_rows)

    x_sorted = _scatter(dest_flat, pad_dest, xn, n_rows)
    act = _gate_up(meta, x_sorted, w_gate_up, b_gate_up)
    y_buf = _down(meta, act, w_down, b_down)
    out = _combine(dest_flat, y_buf, x_mid, gate_t.T, norm_final)
    return out.reshape(b, seq, d)
```

```python
import functools

import numpy as np
import jax
import jax.numpy as jnp
from jax import lax
from jax.experimental import pallas as pl
from jax.experimental.pallas import tpu as pltpu

F32 = jnp.float32
BF16 = jnp.bfloat16

D_MODEL = 2048
HEAD_DIM = 64
N_HEADS = 16
ATTN_WIDTH = N_HEADS * HEAD_DIM
LRU_WIDTH = D_MODEL - ATTN_WIDTH
LRU_BLOCK = 64
CONV_WIDTH = 4
LRU_C = 8.0
IN_COLS = 3 * ATTN_WIDTH + 2 * LRU_WIDTH
DILATIONS = (1, 4, 16)
ATTN_BLOCK = 128
UNITS_PER_TRIP = (5, 6, 8)
N_EXPERTS = 32
TOP_K = 4
D_FF = D_MODEL
SWIGLU_LIMIT = 7.0
SWIGLU_ALPHA = 1.702
MOE_BLOCK = 256
ITEM_ROWS = 8 * MOE_BLOCK
META_STRIDE = 128
EPS = 1e-6

LANES = 128
SUBLANES = 8
VMEM_LIMIT = 56 * 1024 * 1024


def _params(sem, vmem=VMEM_LIMIT):
    return pltpu.CompilerParams(dimension_semantics=sem, vmem_limit_bytes=vmem)


def _rms(xf, g):
    return xf * lax.rsqrt(jnp.mean(xf * xf, axis=-1, keepdims=True) + EPS) * g


def _in_proj_body(x_ref, g_ref, w_ref, o_ref, h_ref):
    @pl.when(pl.program_id(1) == 0)
    def _():
        h_ref[...] = _rms(x_ref[...], g_ref[...]).astype(BF16)

    o_ref[...] = jnp.dot(h_ref[...], w_ref[...].astype(BF16), preferred_element_type=F32)


def _in_proj(x2, g, w_in, tm=1024, tn=512):
    n, d = x2.shape
    cols = w_in.shape[1]
    return pl.pallas_call(
        _in_proj_body,
        grid=(n // tm, cols // tn),
        in_specs=[
            pl.BlockSpec((tm, d), lambda i, j: (i, 0)),
            pl.BlockSpec((1, d), lambda i, j: (0, 0)),
            pl.BlockSpec((d, tn), lambda i, j: (0, j)),
        ],
        out_specs=pl.BlockSpec((tm, tn), lambda i, j: (i, j)),
        out_shape=jax.ShapeDtypeStruct((n, cols), F32),
        scratch_shapes=[pltpu.VMEM((tm, d), BF16)],
        compiler_params=_params(("parallel", "arbitrary")),
        name="in_proj",
    )(x2, g.reshape(1, d), w_in)


REGROUP = 4


def _attention_body(slope_ref, q_ref, k_ref, v_ref, o_ref, qh, kh, vh, acc, mx):
    seq = q_ref.shape[1]
    run = seq // REGROUP
    pair = pl.program_id(1)
    qi = lax.broadcasted_iota(jnp.int32, (ATTN_BLOCK, 2 * ATTN_BLOCK), 0)
    ki = lax.broadcasted_iota(jnp.int32, (ATTN_BLOCK, 2 * ATTN_BLOCK), 1)
    rel2 = qi + ATTN_BLOCK - ki
    ok2 = (rel2 >= 0) & (rel2 <= ATTN_BLOCK)
    rel1 = (qi - ki)[:, :ATTN_BLOCK]
    ok1 = rel1 >= 0

    nh = LANES // HEAD_DIM
    slopes = [slope_ref[pair * nh + hh] for hh in range(nh)]
    def stage(order, rows_dst, rows_src):
        lane = lax.broadcasted_iota(jnp.int32, (run, LANES), 1)
        qv, vv = q_ref[0, rows_src, :] * (HEAD_DIM ** -0.5), v_ref[0, rows_src, :]
        if order:
            kh[rows_dst, :] = k_ref[0, rows_src, :]
        for hh in range(nh):
            mine = (lane >= hh * HEAD_DIM) & (lane < (hh + 1) * HEAD_DIM)
            qh[order, hh, rows_dst, :] = jnp.where(mine, qv, 0.0)
            vh[order, hh, rows_dst, :] = jnp.where(mine, vv, 1.0)

    for c in range(REGROUP):
        stage(0, slice(c * run, (c + 1) * run), pl.ds(c * run, run))
        stage(1, slice(c * run, (c + 1) * run), pl.ds(c, run, stride=REGROUP))

    def run_units(units):
        loaded = [(qh[order, hh, qs, :].astype(BF16),
                   (kh[ks, :] if order else k_ref[0, ks, :]).astype(BF16),
                   vh[order, hh, ks, :].astype(BF16)) for hh, br, order, qs, ks, bias in units]
        scores = [lax.dot_general(q, k, (((1,), (1,)), ((), ())), preferred_element_type=F32)
                  + u[5] for (q, k, v), u in zip(loaded, units)]
        maxes = [jnp.max(s, axis=1, keepdims=True) for s in scores]
        probs = [jnp.exp(s - m).astype(BF16) for s, m in zip(scores, maxes)]
        results = [(jnp.dot(p, v, preferred_element_type=F32), m)
                   for p, (q, k, v), m in zip(probs, loaded, maxes)]
        for (a, m), (hh, br, order, qs, ks, bias) in zip(results, units):
            acc[hh, br, qs, :] = a
            mx[hh, br, qs, :] = jnp.broadcast_to(m, (ATTN_BLOCK, LANES))

    def bias_pair(ok, rel, d):
        return [jnp.where(ok, -sl * (rel * d).astype(F32), -jnp.inf) for sl in slopes]

    def contiguous_branch(br, order, d, n_runs, per_trip):
        nb = seq // (n_runs * ATTN_BLOCK)
        bias1, bias2 = bias_pair(ok1, rel1, d), bias_pair(ok2, rel2, d)
        run_units([(hh, br, order, pl.ds(r * nb * ATTN_BLOCK, ATTN_BLOCK),
                    pl.ds(r * nb * ATTN_BLOCK, ATTN_BLOCK), bias1[hh])
                   for r in range(n_runs) for hh in range(nh)])

        def later(g, carry):
            units = []
            for u in range(per_trip):
                idx = g * per_trip + u
                blk = idx // (nb - 1) * nb + idx % (nb - 1) + 1
                start = pl.multiple_of(blk * ATTN_BLOCK, ATTN_BLOCK)
                units += [(hh, br, order, pl.ds(start, ATTN_BLOCK),
                           pl.ds(start - ATTN_BLOCK, 2 * ATTN_BLOCK), bias2[hh])
                          for hh in range(nh)]
            run_units(units)
            return carry

        lax.fori_loop(0, n_runs * (nb - 1) // per_trip, later, 0)

    contiguous_branch(0, 0, DILATIONS[0], 1, UNITS_PER_TRIP[0])
    contiguous_branch(1, 1, DILATIONS[1], REGROUP, UNITS_PER_TRIP[1])

    d2 = DILATIONS[2]
    sub = d2 // REGROUP
    bias16 = bias_pair(ok1, rel1, d2)

    def strided(g, carry):
        units = []
        for u in range(UNITS_PER_TRIP[2]):
            idx = g * UNITS_PER_TRIP[2] + u
            rows = pl.ds(idx // sub * run + idx % sub, ATTN_BLOCK, stride=sub)
            units += [(hh, 2, 1, rows, rows, bias16[hh]) for hh in range(nh)]
        run_units(units)
        return carry

    lax.fori_loop(0, d2 // UNITS_PER_TRIP[2], strided, 0)

    def merge(c, carry):
        start = c // (run // ATTN_BLOCK) + c % (run // ATTN_BLOCK) * (ATTN_BLOCK * REGROUP)
        nat = pl.ds(start, ATTN_BLOCK, stride=REGROUP)
        reg = pl.ds(pl.multiple_of(c * ATTN_BLOCK, ATTN_BLOCK), ATTN_BLOCK)
        outs = []
        for hh in range(nh):
            m0, m1, m2 = mx[hh, 0, nat, :], mx[hh, 1, reg, :], mx[hh, 2, reg, :]
            mt = jnp.maximum(jnp.maximum(m0, m1), m2)
            tot = (jnp.exp(m0 - mt) * acc[hh, 0, nat, :] + jnp.exp(m1 - mt) * acc[hh, 1, reg, :]
                   + jnp.exp(m2 - mt) * acc[hh, 2, reg, :])
            den = pltpu.roll(tot, HEAD_DIM, axis=1)
            outs.append(tot / den)
        lane_r = lax.broadcasted_iota(jnp.int32, (ATTN_BLOCK, LANES), 1)
        o_ref[0, nat, :] = jnp.where(lane_r < HEAD_DIM, outs[0], outs[1])
        return carry

    lax.fori_loop(0, seq // ATTN_BLOCK, merge, 0, unroll=2)


def _attention(proj3, slopes):
    b, seq, _ = proj3.shape
    npair = ATTN_WIDTH // LANES
    nh = LANES // HEAD_DIM
    blk = (1, seq, LANES)
    return pl.pallas_call(
        _attention_body,
        grid_spec=pltpu.PrefetchScalarGridSpec(
            num_scalar_prefetch=1,
            grid=(b, npair),
            in_specs=[
                pl.BlockSpec(blk, lambda i, j, s: (i, 0, j)),
                pl.BlockSpec(blk, lambda i, j, s: (i, 0, npair + j)),
                pl.BlockSpec(blk, lambda i, j, s: (i, 0, 2 * npair + j)),
            ],
            out_specs=pl.BlockSpec(blk, lambda i, j, s: (i, 0, j)),
            scratch_shapes=[
                pltpu.VMEM((2, nh, seq, LANES), F32),
                pltpu.VMEM((seq, LANES), F32),
                pltpu.VMEM((2, nh, seq, LANES), F32),
                pltpu.VMEM((nh, len(DILATIONS), seq, LANES), F32),
                pltpu.VMEM((nh, len(DILATIONS), seq, LANES), F32),
            ],
        ),
        out_shape=jax.ShapeDtypeStruct((b, seq, ATTN_WIDTH), F32),
        compiler_params=_params(("parallel", "parallel")),
        name="attention",
    )(slopes, proj3, proj3, proj3)


def _rglru_body(xr_ref, gr_ref, cw_ref, cb_ref, w_ref, ba_ref, bx_ref, lam_ref, o_ref,
                xp, a_s, b_s):
    seq = xr_ref.shape[1]
    pad = SUBLANES
    xp[0:pad, :] = jnp.zeros((pad, LANES), F32)
    xp[pad:pad + seq, :] = xr_ref[0]
    lam = lam_ref[...]
    sp = jnp.maximum(-lam, 0.0) + jnp.log(1.0 + jnp.exp(-jnp.abs(lam)))
    w_hi = w_ref[0].astype(BF16)
    w_lo = (w_ref[0] - w_hi.astype(F32)).astype(BF16)
    rows = 256

    def gates(c, carry):
        base = pl.multiple_of(c * rows, rows)
        xc = cb_ref[...] + cw_ref[0:1, :] * xp[pl.ds(base + pad - 3, rows), :]
        for i in range(1, CONV_WIDTH):
            xc = xc + cw_ref[i:i + 1, :] * xp[pl.ds(base + pad - 3 + i, rows), :]
        hi = xc.astype(BF16)
        lo = (xc - hi.astype(F32)).astype(BF16)
        pre = (jnp.dot(hi, w_hi, preferred_element_type=F32)
               + jnp.dot(lo, w_hi, preferred_element_type=F32)
               + jnp.dot(hi, w_lo, preferred_element_type=F32))
        r = jax.nn.sigmoid(pre[:, :LANES] + ba_ref[...])
        ig = jax.nn.sigmoid(pre[:, LANES:] + bx_ref[...])
        log_a = -LRU_C * r * sp
        a = jnp.exp(log_a)
        t = jnp.tanh(log_a)
        b = jnp.sqrt(-2.0 * t / (1.0 - t)) * (ig * xc)
        a_s[pl.ds(base, rows), :] = a
        b_s[pl.ds(base, rows), :] = b
        return carry

    lax.fori_loop(0, seq // rows, gates, 0)

    row = lax.broadcasted_iota(jnp.int32, (SUBLANES, LANES), 0)

    def scan(c, h_prev):
        sl = pl.ds(pl.multiple_of(c * SUBLANES, SUBLANES), SUBLANES)
        a = a_s[sl, :]
        b = b_s[sl, :]
        for s in (1, 2, 4):
            keep = row >= s
            a_sh = jnp.where(keep, pltpu.roll(a, s, axis=0), 1.0)
            b_sh = jnp.where(keep, pltpu.roll(b, s, axis=0), 0.0)
            b = a * b_sh + b
            a = a * a_sh
        h = a * h_prev + b
        o_ref[0, sl, :] = h * jax.nn.gelu(gr_ref[0, sl, :])
        return jnp.broadcast_to(h[SUBLANES - 1:SUBLANES, :], (SUBLANES, LANES))

    lax.fori_loop(0, seq // SUBLANES, scan, jnp.zeros((SUBLANES, LANES), F32), unroll=8)


def _rglru(proj3, conv_w, conv_b, w_bd, b_a, b_x, lam):
    b, seq, _ = proj3.shape
    nt = LRU_WIDTH // LANES
    xr0 = 3 * ATTN_WIDTH // LANES
    gr0 = xr0 + nt
    blk = (1, seq, LANES)
    vec = lambda: pl.BlockSpec((1, LANES), lambda i, j: (0, j))
    return pl.pallas_call(
        _rglru_body,
        grid=(b, nt),
        in_specs=[
            pl.BlockSpec(blk, lambda i, j: (i, 0, xr0 + j)),
            pl.BlockSpec(blk, lambda i, j: (i, 0, gr0 + j)),
            pl.BlockSpec((CONV_WIDTH, LANES), lambda i, j: (0, j)),
            vec(),
            pl.BlockSpec((1, LANES, 2 * LANES), lambda i, j: (j, 0, 0)),
            vec(), vec(), vec(),
        ],
        out_specs=pl.BlockSpec(blk, lambda i, j: (i, 0, j)),
        out_shape=jax.ShapeDtypeStruct((b, seq, LRU_WIDTH), F32),
        scratch_shapes=[
            pltpu.VMEM((seq + SUBLANES, LANES), F32),
            pltpu.VMEM((seq, LANES), F32),
            pltpu.VMEM((seq, LANES), F32),
        ],
        compiler_params=_params(("parallel", "parallel")),
        name="rglru",
    )(proj3, proj3, conv_w, conv_b.reshape(1, -1), w_bd, b_a.reshape(1, -1),
      b_x.reshape(1, -1), lam.reshape(1, -1))


def _out_proj_body(at_ref, rc_ref, ga_ref, gr_ref, x_ref, w_ref, o_ref, h_ref):
    wa = at_ref.shape[1]

    @pl.when(pl.program_id(1) == 0)
    def _():
        h_ref[:, :wa] = _rms(at_ref[...], ga_ref[...]).astype(BF16)
        h_ref[:, wa:] = _rms(rc_ref[...], gr_ref[...]).astype(BF16)

    o_ref[...] = x_ref[...] + jnp.dot(h_ref[...], w_ref[...].astype(BF16),
                                      preferred_element_type=F32)


def _out_proj(attn2, rec2, g_attn, g_rec, x2, w_out, tm=512, tn=1024):
    n, d = x2.shape
    wa, wr = attn2.shape[1], rec2.shape[1]
    return pl.pallas_call(
        _out_proj_body,
        grid=(n // tm, d // tn),
        in_specs=[
            pl.BlockSpec((tm, wa), lambda i, j: (i, 0)),
            pl.BlockSpec((tm, wr), lambda i, j: (i, 0)),
            pl.BlockSpec((1, wa), lambda i, j: (0, 0)),
            pl.BlockSpec((1, wr), lambda i, j: (0, 0)),
            pl.BlockSpec((tm, tn), lambda i, j: (i, j)),
            pl.BlockSpec((d, tn), lambda i, j: (0, j)),
        ],
        out_specs=pl.BlockSpec((tm, tn), lambda i, j: (i, j)),
        out_shape=jax.ShapeDtypeStruct((n, d), F32),
        scratch_shapes=[pltpu.VMEM((tm, d), BF16)],
        compiler_params=_params(("parallel", "arbitrary")),
        name="out_proj",
    )(attn2, rec2, g_attn.reshape(1, wa), g_rec.reshape(1, wr), x2, w_out)


def _router_body(x_ref, g_ref, wt_ref, b_ref, xn_ref, idx_ref, gate_ref, rank_ref, cnt_ref,
                 base):
    tm = x_ref.shape[0]

    @pl.when(pl.program_id(0) == 0)
    def _():
        base[...] = jnp.zeros_like(base)

    xn = _rms(x_ref[...], g_ref[...])
    xn_ref[...] = xn
    x_hi = xn.astype(BF16)
    x_lo = (xn - x_hi.astype(F32)).astype(BF16)
    w = wt_ref[...]
    w_hi = w.astype(BF16)
    w_lo = (w - w_hi.astype(F32)).astype(BF16)
    nt = (((1,), (1,)), ((), ()))
    logits = (lax.dot_general(w_hi, x_hi, nt, preferred_element_type=F32)
              + lax.dot_general(w_hi, x_lo, nt, preferred_element_type=F32)
              + lax.dot_general(w_lo, x_hi, nt, preferred_element_type=F32)
              + b_ref[...])

    eid = lax.broadcasted_iota(jnp.int32, (N_EXPERTS, tm), 0)
    work = logits
    vals, hots = [], []
    for _ in range(TOP_K):
        best = jnp.max(work, axis=0, keepdims=True)
        pick = jnp.min(jnp.where(work == best, eid, N_EXPERTS), axis=0, keepdims=True)
        hot = eid == pick
        vals.append(best)
        hots.append(hot)
        work = jnp.where(hot, -jnp.inf, work)
        idx_ref[len(vals) - 1:len(vals), :] = pick

    ex = [jnp.exp(v - vals[0]) for v in vals]
    den = ex[0] + ex[1] + ex[2] + ex[3]
    for k in range(TOP_K):
        gate_ref[k:k + 1, :] = ex[k] / den

    chosen = (hots[0] | hots[1] | hots[2] | hots[3])
    si = lax.broadcasted_iota(jnp.int32, (tm, tm), 0)
    ti = lax.broadcasted_iota(jnp.int32, (tm, tm), 1)
    before = (si < ti).astype(BF16)
    prefix = jnp.dot(chosen.astype(BF16), before, preferred_element_type=F32)
    slot = base[:, 0:1] + prefix
    for k in range(TOP_K):
        rank_ref[k:k + 1, :] = jnp.sum(jnp.where(hots[k], slot, 0.0), axis=0,
                                       keepdims=True).astype(jnp.int32)
    base[...] = base[...] + jnp.sum(chosen.astype(F32), axis=1, keepdims=True)
    cnt_ref[...] = base[...].astype(jnp.int32)


def _router(x2, g, w_router_t, b_router, tm=256):
    n, d = x2.shape
    row = lambda: pl.BlockSpec((TOP_K, tm), lambda i: (0, i))
    return pl.pallas_call(
        _router_body,
        grid=(n // tm,),
        in_specs=[
            pl.BlockSpec((tm, d), lambda i: (i, 0)),
            pl.BlockSpec((1, d), lambda i: (0, 0)),
            pl.BlockSpec((N_EXPERTS, d), lambda i: (0, 0)),
            pl.BlockSpec((N_EXPERTS, 1), lambda i: (0, 0)),
        ],
        out_specs=[
            pl.BlockSpec((tm, d), lambda i: (i, 0)),
            row(), row(), row(),
            pl.BlockSpec((N_EXPERTS, LANES), lambda i: (0, 0)),
        ],
        out_shape=[
            jax.ShapeDtypeStruct((n, d), F32),
            jax.ShapeDtypeStruct((TOP_K, n), jnp.int32),
            jax.ShapeDtypeStruct((TOP_K, n), F32),
            jax.ShapeDtypeStruct((TOP_K, n), jnp.int32),
            jax.ShapeDtypeStruct((N_EXPERTS, LANES), jnp.int32),
        ],
        scratch_shapes=[pltpu.VMEM((N_EXPERTS, LANES), F32)],
        compiler_params=_params(("arbitrary",)),
        name="router",
    )(x2, g.reshape(1, d), w_router_t, b_router.reshape(N_EXPERTS, 1))


def _scatter_body(dest_ref, pad_ref, xn_ref, o_ref, zrow, sem, *, n_tokens, pad_per_step):
    tm = xn_ref.shape[0]
    i = pl.program_id(0)

    @pl.when(i == 0)
    def _():
        zrow[...] = jnp.zeros_like(zrow)

    def row_copy(t, k):
        d = dest_ref[k * n_tokens + i * tm + t]
        return pltpu.make_async_copy(xn_ref.at[pl.ds(t, 1), :], o_ref.at[pl.ds(d, 1), :], sem)

    def pad_copy(q):
        d = pad_ref[i * pad_per_step + q]
        return pltpu.make_async_copy(zrow.at[pl.ds(0, 1), :], o_ref.at[pl.ds(d, 1), :], sem)

    def issue(t, c):
        for k in range(TOP_K):
            row_copy(t, k).start(priority=k % 2)
        return c

    lax.fori_loop(0, tm, issue, 0)

    def issue_pad(q, c):
        pad_copy(q).start()
        return c

    lax.fori_loop(0, pad_per_step, issue_pad, 0)

    def drain(t, c):
        for k in range(TOP_K):
            row_copy(t, k).wait()
        return c

    lax.fori_loop(0, tm, drain, 0)

    def drain_pad(q, c):
        pad_copy(q).wait()
        return c

    lax.fori_loop(0, pad_per_step, drain_pad, 0)


def _scatter(dest_flat, pad_dest, xn, n_rows, tm=256):
    n, d = xn.shape
    steps = n // tm
    pad_per_step = pad_dest.shape[0] // steps
    return pl.pallas_call(
        functools.partial(_scatter_body, n_tokens=n, pad_per_step=pad_per_step),
        grid_spec=pltpu.PrefetchScalarGridSpec(
            num_scalar_prefetch=2,
            grid=(steps,),
            in_specs=[pl.BlockSpec((tm, d), lambda i, *_: (i, 0))],
            out_specs=pl.BlockSpec(memory_space=pl.ANY),
            scratch_shapes=[pltpu.VMEM((SUBLANES, d), F32), pltpu.SemaphoreType.DMA(())],
        ),
        out_shape=jax.ShapeDtypeStruct((n_rows, d), F32),
        compiler_params=_params(("arbitrary",)),
        name="scatter",
    )(dest_flat, pad_dest, xn)


def _moe_body(meta_ref, x_hbm, w_ref, b_ref, o_hbm, xbf, stage, wbf, obuf, sem_in, sem_out,
              pend, *, prep_w, epilogue):
    t, j = pl.program_id(0), pl.program_id(1)
    row0 = meta_ref[META_STRIDE + t]
    nsub = meta_ref[2 * META_STRIDE + t]
    zero = meta_ref[3 * META_STRIDE + t]
    tn_out = obuf.shape[2]
    col0 = pl.multiple_of(j * tn_out, tn_out)

    @pl.when((t == 0) & (j == 0))
    def _():
        pend[0] = 0
        pend[1] = 0

    def in_copy(c, slot):
        src = x_hbm.at[pl.ds(pl.multiple_of(row0 + c * MOE_BLOCK, MOE_BLOCK), MOE_BLOCK), :]
        if stage is None:
            dst = xbf.at[pl.ds(pl.multiple_of(c * MOE_BLOCK, MOE_BLOCK), MOE_BLOCK), :]
        else:
            dst = stage.at[slot]
        return pltpu.make_async_copy(src, dst, sem_in.at[slot])

    @pl.when((j == 0) & (nsub > 0) & (zero == 0))
    def _load_rows():
        if stage is None:
            def start(c, carry):
                in_copy(c, 0).start()
                return carry

            def wait(c, carry):
                in_copy(c, 0).wait()
                return carry

            lax.fori_loop(0, nsub, start, 0)
            lax.fori_loop(0, nsub, wait, 0)
        else:
            in_copy(0, 0).start()

            def body(c, carry):
                slot = c % 2

                @pl.when(c + 1 < nsub)
                def _():
                    in_copy(c + 1, 1 - slot).start()

                in_copy(c, slot).wait()
                rows = pl.ds(pl.multiple_of(c * MOE_BLOCK, MOE_BLOCK), MOE_BLOCK)
                xbf[rows, :] = stage[slot].astype(BF16)
                return carry

            lax.fori_loop(0, nsub, body, 0)

    def out_copy(s, slot):
        rows = pl.ds(pl.multiple_of(row0 + s * MOE_BLOCK, MOE_BLOCK), MOE_BLOCK)
        return pltpu.make_async_copy(obuf.at[slot], o_hbm.at[rows, pl.ds(col0, tn_out)],
                                     sem_out.at[slot])

    def emit(s, value_fn):
        slot = s % 2

        @pl.when(pend[slot] == 1)
        def _():
            out_copy(s, slot).wait()

        obuf[slot] = value_fn()
        out_copy(s, slot).start()
        pend[slot] = 1

    @pl.when((nsub > 0) & (zero == 0))
    def _compute():
        prep_w(w_ref, wbf)

        def sub(s, carry):
            rows = pl.ds(pl.multiple_of(s * MOE_BLOCK, MOE_BLOCK), MOE_BLOCK)
            emit(s, lambda: epilogue(xbf[rows, :], wbf, b_ref))
            return carry

        lax.fori_loop(0, nsub, sub, 0)

    @pl.when((nsub > 0) & (zero == 1))
    def _zeros():
        def sub(s, carry):
            emit(s, lambda: jnp.zeros(obuf.shape[1:], obuf.dtype))
            return carry

        lax.fori_loop(0, nsub, sub, 0)

    @pl.when((t == pl.num_programs(0) - 1) & (j == pl.num_programs(1) - 1))
    def _drain():
        for slot in range(2):
            @pl.when(pend[slot] == 1)
            def _():
                out_copy(0, slot).wait()


def _moe_call(name, meta, x, w, b, out_cols, out_dtype, tn, tn_out, prep_w, epilogue, tmp_shape):
    p, kdim = x.shape
    n_items = _max_items(p)
    n_tiles = w.shape[2] // tn
    needs_stage = x.dtype != BF16

    def wmap(t, j, m):
        live = (m[2 * META_STRIDE + t] > 0) & (m[3 * META_STRIDE + t] == 0)
        return (m[t], 0, jnp.where(live, j, n_tiles - 1))

    scratch = [pltpu.VMEM((ITEM_ROWS, kdim), BF16)]
    if needs_stage:
        scratch.append(pltpu.VMEM((2, MOE_BLOCK, kdim), x.dtype))
    scratch += [
        pltpu.VMEM((kdim, tn), BF16),
        pltpu.VMEM((2, MOE_BLOCK, tn_out), out_dtype),
        pltpu.SemaphoreType.DMA((2,)),
        pltpu.SemaphoreType.DMA((2,)),
        pltpu.SMEM((2,), jnp.int32),
    ]
    if tmp_shape is not None:
        scratch.append(pltpu.VMEM(tmp_shape, F32))

    def body(meta_ref, x_hbm, w_ref, b_ref, o_hbm, xbf, *rest):
        rest = list(rest)
        stage = rest.pop(0) if needs_stage else None
        wbf, obuf, sem_in, sem_out, pend = rest[:5]
        tmp = rest[5] if tmp_shape is not None else None
        _moe_body(meta_ref, x_hbm, w_ref, b_ref, o_hbm, xbf, stage, wbf, obuf, sem_in, sem_out,
                  pend, prep_w=functools.partial(prep_w, tmp=tmp), epilogue=epilogue)

    return pl.pallas_call(
        body,
        grid_spec=pltpu.PrefetchScalarGridSpec(
            num_scalar_prefetch=1,
            grid=(n_items, n_tiles),
            in_specs=[
                pl.BlockSpec(memory_space=pl.ANY),
                pl.BlockSpec((1, kdim, tn), wmap),
                pl.BlockSpec((1, 1, tn), wmap),
            ],
            out_specs=pl.BlockSpec(memory_space=pl.ANY),
            scratch_shapes=scratch,
        ),
        out_shape=jax.ShapeDtypeStruct((p, out_cols), out_dtype),
        compiler_params=_params(("arbitrary", "arbitrary")),
        name=name,
    )(meta, x, w, b.reshape(N_EXPERTS, 1, -1))


def _cast_weights(w_ref, wbf, tmp=None, rows=256):
    def cast(c, carry):
        sl = pl.ds(pl.multiple_of(c * rows, rows), rows)
        wbf[sl, :] = w_ref[0, sl, :].astype(BF16)
        return carry

    lax.fori_loop(0, wbf.shape[0] // rows, cast, 0)


def _swiglu_tile(x, wbf, b_ref):
    even = lax.broadcasted_iota(jnp.int32, (x.shape[0], LANES), 1) % 2 == 0
    cols = []
    for c in range(wbf.shape[1] // (2 * LANES)):
        c0 = c * 2 * LANES
        gu = jnp.dot(x, wbf[:, c0:c0 + 2 * LANES], preferred_element_type=F32)
        gu = gu + b_ref[0, :, c0:c0 + 2 * LANES]
        v1, v2 = gu[:, :LANES], gu[:, LANES:]
        gate = jnp.where(even, v1, pltpu.roll(v2, 1, axis=1))
        up = jnp.where(even, pltpu.roll(v1, LANES - 1, axis=1), v2)
        gate = jnp.minimum(gate, SWIGLU_LIMIT)
        up = jnp.clip(up, -SWIGLU_LIMIT, SWIGLU_LIMIT)
        cols.append((gate * jax.nn.sigmoid(SWIGLU_ALPHA * gate) * (up + 1.0)).astype(BF16))
    return jnp.concatenate(cols, axis=1)


def _gate_up(meta, x_sorted, w_gate_up, b_gate_up, tn=1024):
    return _moe_call("gate_up", meta, x_sorted, w_gate_up, b_gate_up, w_gate_up.shape[2] // 2,
                     BF16, tn, tn // 2, _cast_weights, _swiglu_tile, None)


def _permute_weights(w_ref, wbf, tmp):
    half = LANES // 2

    def permute(g, carry):
        base = pl.multiple_of(g * LANES, LANES)
        for c in range(tmp.shape[0]):
            cs = slice(c * LANES, (c + 1) * LANES)
            tmp[c, pl.ds(0, half, stride=2), :] = w_ref[0, pl.ds(base, half), cs]
            tmp[c, pl.ds(1, half, stride=2), :] = w_ref[0, pl.ds(base + half, half), cs]
            wbf[pl.ds(base, LANES), cs] = tmp[c].astype(BF16)
        return carry

    lax.fori_loop(0, wbf.shape[0] // LANES, permute, 0)


def _linear_tile(a, wbf, b_ref):
    return jnp.dot(a, wbf[...], preferred_element_type=F32) + b_ref[0]


def _down(meta, act, w_down, b_down, tn=1024):
    return _moe_call("down", meta, act, w_down, b_down, w_down.shape[2], F32, tn, tn,
                     _permute_weights, _linear_tile, (tn // LANES, LANES, LANES))


def _combine_body(dest_ref, y_ref, x_ref, gate_ref, g_ref, o_ref, ybuf, sem, *, n_tokens):
    tm = x_ref.shape[0]
    i = pl.program_id(0)

    def row_copy(t, k):
        d = dest_ref[k * n_tokens + i * tm + t]
        return pltpu.make_async_copy(y_ref.at[pl.ds(d, 1), :], ybuf.at[k, pl.ds(t, 1), :], sem)

    def issue(t, c):
        for k in range(TOP_K):
            row_copy(t, k).start(priority=k % 2)
        return c

    lax.fori_loop(0, tm, issue, 0)

    def drain(t, c):
        for k in range(TOP_K):
            row_copy(t, k).wait()
        return c

    lax.fori_loop(0, tm, drain, 0)

    acc = x_ref[...]
    for k in range(TOP_K):
        acc = acc + gate_ref[:, k:k + 1] * ybuf[k]
    o_ref[...] = _rms(acc, g_ref[...])


def _combine(dest_flat, y_buf, x2, gates, g, tm=256):
    n, d = x2.shape
    return pl.pallas_call(
        functools.partial(_combine_body, n_tokens=n),
        grid_spec=pltpu.PrefetchScalarGridSpec(
            num_scalar_prefetch=1,
            grid=(n // tm,),
            in_specs=[
                pl.BlockSpec(memory_space=pl.ANY),
                pl.BlockSpec((tm, d), lambda i, *_: (i, 0)),
                pl.BlockSpec((tm, TOP_K), lambda i, *_: (i, 0)),
                pl.BlockSpec((1, d), lambda i, *_: (0, 0)),
            ],
            out_specs=pl.BlockSpec((tm, d), lambda i, *_: (i, 0)),
            scratch_shapes=[pltpu.VMEM((TOP_K, tm, d), F32), pltpu.SemaphoreType.DMA(())],
        ),
        out_shape=jax.ShapeDtypeStruct((n, d), F32),
        compiler_params=_params(("arbitrary",)),
        name="combine",
    )(dest_flat, y_buf, x2, gates, g.reshape(1, d))


def _cumsum_sublanes(x):
    row = lax.broadcasted_iota(jnp.int32, x.shape, 0)
    s = 1
    while s < x.shape[0]:
        x = x + jnp.where(row >= s, pltpu.roll(x, s, axis=0), 0)
        s *= 2
    return x


def _tables_body(cnt_ref, idx_ref, rank_ref, dest_ref, meta_ref, pad_ref, *, n_rows):
    cnt = cnt_ref[...]
    padded = (cnt + (MOE_BLOCK - 1)) // MOE_BLOCK * MOE_BLOCK
    pend = _cumsum_sublanes(padded)
    pstart = pend - padded
    total = pend[N_EXPERTS - 1:N_EXPERTS, 0:1]

    def lookup(table, sel):
        eid = lax.broadcasted_iota(jnp.int32, (N_EXPERTS, sel.shape[1]), 0)
        return jnp.sum(jnp.where(eid == sel, table[:, 0:1], 0), axis=0, keepdims=True)

    def count_le(table, v):
        return jnp.sum((table[:, 0:1] <= v).astype(jnp.int32), axis=0, keepdims=True)

    width = 2048
    for k in range(TOP_K):
        for c in range(idx_ref.shape[1] // width):
            sl = slice(c * width, (c + 1) * width)
            dest_ref[k:k + 1, sl] = lookup(pstart, idx_ref[k:k + 1, sl]) + rank_ref[k:k + 1, sl]

    n_it = (padded + (ITEM_ROWS - 1)) // ITEM_ROWS
    it_end = _cumsum_sublanes(n_it)
    t = lax.broadcasted_iota(jnp.int32, (1, META_STRIDE), 1)
    e = count_le(it_end, t)
    ec = jnp.minimum(e, N_EXPERTS - 1)
    k = t - lookup(it_end - n_it, ec)
    rows_real = jnp.minimum(lookup(padded, ec) - k * ITEM_ROWS, ITEM_ROWS)
    row0_zero = total + (t - it_end[N_EXPERTS - 1:N_EXPERTS, 0:1]) * ITEM_ROWS
    rows_zero = jnp.clip(n_rows - row0_zero, 0, ITEM_ROWS)
    real = e < N_EXPERTS
    meta_ref[:, 0:META_STRIDE] = ec
    meta_ref[:, META_STRIDE:2 * META_STRIDE] = jnp.where(
        real, lookup(pstart, ec) + k * ITEM_ROWS, jnp.minimum(row0_zero, n_rows - MOE_BLOCK))
    meta_ref[:, 2 * META_STRIDE:3 * META_STRIDE] = jnp.where(real, rows_real, rows_zero) // MOE_BLOCK
    meta_ref[:, 3 * META_STRIDE:4 * META_STRIDE] = jnp.where(real, 0, 1)

    gap = padded - cnt
    gap_end = _cumsum_sublanes(gap)
    q = lax.broadcasted_iota(jnp.int32, pad_ref.shape, 1)
    eq = count_le(gap_end, q)
    inside = lookup(pstart + cnt - (gap_end - gap), eq) + q
    tail = total + q - gap_end[N_EXPERTS - 1:N_EXPERTS, 0:1]
    pad_ref[...] = jnp.where(eq < N_EXPERTS, inside, tail)


def _max_items(n_rows):
    return N_EXPERTS + n_rows // ITEM_ROWS + (N_EXPERTS * MOE_BLOCK) // ITEM_ROWS + 1


def _tables(cnt, idx_t, rank_t, n_rows):
    n_pad = n_rows - idx_t.size
    assert _max_items(n_rows) <= META_STRIDE
    dest, meta, pad_dest = pl.pallas_call(
        functools.partial(_tables_body, n_rows=n_rows),
        out_shape=[
            jax.ShapeDtypeStruct(idx_t.shape, jnp.int32),
            jax.ShapeDtypeStruct((1, 4 * META_STRIDE), jnp.int32),
            jax.ShapeDtypeStruct((1, n_pad), jnp.int32),
        ],
        name="tables",
    )(cnt, idx_t, rank_t)
    return dest.reshape(-1), meta.reshape(-1), pad_dest.reshape(-1)


def kernel(x, norm_mix, w_in, conv_w, conv_b, w_a, b_a, w_x, b_x, lru_lambda, attn_out_norm, lru_out_norm, w_out, norm_ffn, w_router, b_router, w_gate_up, b_gate_up, w_down, b_down, norm_final):
    b, seq, d = x.shape
    n = b * seq
    x2 = x.reshape(n, d)

    proj = _in_proj(x2, norm_mix, w_in)
    proj3 = proj.reshape(b, seq, IN_COLS)

    slopes = jnp.asarray(2.0 ** (-8.0 * np.arange(1, N_HEADS + 1) / N_HEADS), F32)
    attn = _attention(proj3, slopes)

    def pair_blocks(w):
        w4 = w.reshape(-1, 2, LRU_BLOCK, LRU_BLOCK)
        z = jnp.zeros_like(w4[:, 0])
        top = jnp.concatenate([w4[:, 0], z], axis=2)
        bot = jnp.concatenate([z, w4[:, 1]], axis=2)
        return jnp.concatenate([top, bot], axis=1)

    w_bd = jnp.concatenate([pair_blocks(w_a), pair_blocks(w_x)], axis=2)
    rec = _rglru(proj3, conv_w, conv_b, w_bd, b_a, b_x, lru_lambda)

    x_mid = _out_proj(attn.reshape(n, ATTN_WIDTH), rec.reshape(n, LRU_WIDTH),
                      attn_out_norm, lru_out_norm, x2, w_out)

    xn, idx_t, gate_t, rank_t, cnt = _router(x_mid, norm_ffn, w_router.T, b_router)
    n_rows = n * TOP_K + N_EXPERTS * MOE_BLOCK
    dest_flat, meta, pad_dest = _tables(cnt, idx_t, rank_t, n_rows)

    x_sorted = _scatter(dest_flat, pad_dest, xn, n_rows)
    act = _gate_up(meta, x_sorted, w_gate_up, b_gate_up)
    y_buf = _down(meta, act, w_down, b_down)
    out = _combine(dest_flat, y_buf, x_mid, gate_t.T, norm_final)
    return out.reshape(b, seq, d)
```

```python
import functools

import numpy as np
import jax
import jax.numpy as jnp
from jax import lax
from jax.experimental import pallas as pl
from jax.experimental.pallas import tpu as pltpu

F32 = jnp.float32
BF16 = jnp.bfloat16

D_MODEL = 2048
HEAD_DIM = 64
N_HEADS = 16
ATTN_WIDTH = N_HEADS * HEAD_DIM
LRU_WIDTH = D_MODEL - ATTN_WIDTH
LRU_BLOCK = 64
CONV_WIDTH = 4
LRU_C = 8.0
IN_COLS = 3 * ATTN_WIDTH + 2 * LRU_WIDTH
DILATIONS = (1, 4, 16)
ATTN_BLOCK = 128
UNITS_PER_TRIP = (5, 6, 8)
N_EXPERTS = 32
TOP_K = 4
D_FF = D_MODEL
SWIGLU_LIMIT = 7.0
SWIGLU_ALPHA = 1.702
MOE_BLOCK = 256
ITEM_ROWS = 8 * MOE_BLOCK
META_STRIDE = 128
EPS = 1e-6

LANES = 128
SUBLANES = 8
VMEM_LIMIT = 56 * 1024 * 1024


def _params(sem, vmem=VMEM_LIMIT, unchecked_dma=False):
    return pltpu.CompilerParams(dimension_semantics=sem, vmem_limit_bytes=vmem,
                                disable_bounds_checks=unchecked_dma)


def _rms(xf, g):
    return xf * lax.rsqrt(jnp.mean(xf * xf, axis=-1, keepdims=True) + EPS) * g


def _in_proj_body(x_ref, g_ref, w_ref, o_ref, h_ref):
    @pl.when(pl.program_id(1) == 0)
    def _():
        h_ref[...] = _rms(x_ref[...], g_ref[...]).astype(BF16)

    o_ref[...] = jnp.dot(h_ref[...], w_ref[...].astype(BF16), preferred_element_type=F32)


def _in_proj(x2, g, w_in, tm=1024, tn=512):
    n, d = x2.shape
    cols = w_in.shape[1]
    return pl.pallas_call(
        _in_proj_body,
        grid=(n // tm, cols // tn),
        in_specs=[
            pl.BlockSpec((tm, d), lambda i, j: (i, 0)),
            pl.BlockSpec((1, d), lambda i, j: (0, 0)),
            pl.BlockSpec((d, tn), lambda i, j: (0, j)),
        ],
        out_specs=pl.BlockSpec((tm, tn), lambda i, j: (i, j)),
        out_shape=jax.ShapeDtypeStruct((n, cols), F32),
        scratch_shapes=[pltpu.VMEM((tm, d), BF16)],
        compiler_params=_params(("parallel", "arbitrary")),
        name="in_proj",
    )(x2, g.reshape(1, d), w_in)


REGROUP = 4


def _attention_body(slope_ref, q_ref, k_ref, v_ref, o_ref, qh, kh, vh, acc, mx):
    seq = q_ref.shape[1]
    run = seq // REGROUP
    pair = pl.program_id(1)
    qi = lax.broadcasted_iota(jnp.int32, (ATTN_BLOCK, 2 * ATTN_BLOCK), 0)
    ki = lax.broadcasted_iota(jnp.int32, (ATTN_BLOCK, 2 * ATTN_BLOCK), 1)
    rel2 = qi + ATTN_BLOCK - ki
    ok2 = (rel2 >= 0) & (rel2 <= ATTN_BLOCK)
    rel1 = (qi - ki)[:, :ATTN_BLOCK]
    ok1 = rel1 >= 0

    nh = LANES // HEAD_DIM
    slopes = [slope_ref[pair * nh + hh] for hh in range(nh)]
    def stage(order, rows_dst, rows_src):
        lane = lax.broadcasted_iota(jnp.int32, (run, LANES), 1)
        qv, vv = q_ref[0, rows_src, :] * (HEAD_DIM ** -0.5), v_ref[0, rows_src, :]
        if order:
            kh[rows_dst, :] = k_ref[0, rows_src, :]
        for hh in range(nh):
            mine = (lane >= hh * HEAD_DIM) & (lane < (hh + 1) * HEAD_DIM)
            qh[order, hh, rows_dst, :] = jnp.where(mine, qv, 0.0)
            vh[order, hh, rows_dst, :] = jnp.where(mine, vv, 1.0)

    for c in range(REGROUP):
        stage(0, slice(c * run, (c + 1) * run), pl.ds(c * run, run))
        stage(1, slice(c * run, (c + 1) * run), pl.ds(c, run, stride=REGROUP))

    def run_units(units):
        loaded = [(qh[order, hh, qs, :].astype(BF16),
                   (kh[ks, :] if order else k_ref[0, ks, :]).astype(BF16),
                   vh[order, hh, ks, :].astype(BF16)) for hh, br, order, qs, ks, bias in units]
        scores = [lax.dot_general(q, k, (((1,), (1,)), ((), ())), preferred_element_type=F32)
                  + u[5] for (q, k, v), u in zip(loaded, units)]
        maxes = [jnp.max(s, axis=1, keepdims=True) for s in scores]
        probs = [jnp.exp(s - m).astype(BF16) for s, m in zip(scores, maxes)]
        results = [(jnp.dot(p, v, preferred_element_type=F32), m)
                   for p, (q, k, v), m in zip(probs, loaded, maxes)]
        for (a, m), (hh, br, order, qs, ks, bias) in zip(results, units):
            acc[hh, br, qs, :] = a
            mx[hh, br, qs, :] = jnp.broadcast_to(m, (ATTN_BLOCK, LANES))

    def bias_pair(ok, rel, d):
        return [jnp.where(ok, -sl * (rel * d).astype(F32), -jnp.inf) for sl in slopes]

    def contiguous_branch(br, order, d, n_runs, per_trip):
        nb = seq // (n_runs * ATTN_BLOCK)
        bias1, bias2 = bias_pair(ok1, rel1, d), bias_pair(ok2, rel2, d)
        run_units([(hh, br, order, pl.ds(r * nb * ATTN_BLOCK, ATTN_BLOCK),
                    pl.ds(r * nb * ATTN_BLOCK, ATTN_BLOCK), bias1[hh])
                   for r in range(n_runs) for hh in range(nh)])

        def later(g, carry):
            units = []
            for u in range(per_trip):
                idx = g * per_trip + u
                blk = idx // (nb - 1) * nb + idx % (nb - 1) + 1
                start = pl.multiple_of(blk * ATTN_BLOCK, ATTN_BLOCK)
                units += [(hh, br, order, pl.ds(start, ATTN_BLOCK),
                           pl.ds(start - ATTN_BLOCK, 2 * ATTN_BLOCK), bias2[hh])
                          for hh in range(nh)]
            run_units(units)
            return carry

        lax.fori_loop(0, n_runs * (nb - 1) // per_trip, later, 0)

    contiguous_branch(0, 0, DILATIONS[0], 1, UNITS_PER_TRIP[0])
    contiguous_branch(1, 1, DILATIONS[1], REGROUP, UNITS_PER_TRIP[1])

    d2 = DILATIONS[2]
    sub = d2 // REGROUP
    bias16 = bias_pair(ok1, rel1, d2)

    def strided(g, carry):
        units = []
        for u in range(UNITS_PER_TRIP[2]):
            idx = g * UNITS_PER_TRIP[2] + u
            rows = pl.ds(idx // sub * run + idx % sub, ATTN_BLOCK, stride=sub)
            units += [(hh, 2, 1, rows, rows, bias16[hh]) for hh in range(nh)]
        run_units(units)
        return carry

    lax.fori_loop(0, d2 // UNITS_PER_TRIP[2], strided, 0)

    def merge(c, carry):
        start = c // (run // ATTN_BLOCK) + c % (run // ATTN_BLOCK) * (ATTN_BLOCK * REGROUP)
        nat = pl.ds(start, ATTN_BLOCK, stride=REGROUP)
        reg = pl.ds(pl.multiple_of(c * ATTN_BLOCK, ATTN_BLOCK), ATTN_BLOCK)
        outs = []
        for hh in range(nh):
            m0, m1, m2 = mx[hh, 0, nat, :], mx[hh, 1, reg, :], mx[hh, 2, reg, :]
            mt = jnp.maximum(jnp.maximum(m0, m1), m2)
            tot = (jnp.exp(m0 - mt) * acc[hh, 0, nat, :] + jnp.exp(m1 - mt) * acc[hh, 1, reg, :]
                   + jnp.exp(m2 - mt) * acc[hh, 2, reg, :])
            den = pltpu.roll(tot, HEAD_DIM, axis=1)
            outs.append(tot / den)
        lane_r = lax.broadcasted_iota(jnp.int32, (ATTN_BLOCK, LANES), 1)
        o_ref[0, nat, :] = jnp.where(lane_r < HEAD_DIM, outs[0], outs[1])
        return carry

    lax.fori_loop(0, seq // ATTN_BLOCK, merge, 0, unroll=2)


def _attention(proj3, slopes):
    b, seq, _ = proj3.shape
    npair = ATTN_WIDTH // LANES
    nh = LANES // HEAD_DIM
    blk = (1, seq, LANES)
    return pl.pallas_call(
        _attention_body,
        grid_spec=pltpu.PrefetchScalarGridSpec(
            num_scalar_prefetch=1,
            grid=(b, npair),
            in_specs=[
                pl.BlockSpec(blk, lambda i, j, s: (i, 0, j)),
                pl.BlockSpec(blk, lambda i, j, s: (i, 0, npair + j)),
                pl.BlockSpec(blk, lambda i, j, s: (i, 0, 2 * npair + j)),
            ],
            out_specs=pl.BlockSpec(blk, lambda i, j, s: (i, 0, j)),
            scratch_shapes=[
                pltpu.VMEM((2, nh, seq, LANES), F32),
                pltpu.VMEM((seq, LANES), F32),
                pltpu.VMEM((2, nh, seq, LANES), F32),
                pltpu.VMEM((nh, len(DILATIONS), seq, LANES), F32),
                pltpu.VMEM((nh, len(DILATIONS), seq, LANES), F32),
            ],
        ),
        out_shape=jax.ShapeDtypeStruct((b, seq, ATTN_WIDTH), F32),
        compiler_params=_params(("parallel", "parallel")),
        name="attention",
    )(slopes, proj3, proj3, proj3)


def _rglru_body(xr_ref, gr_ref, cw_ref, cb_ref, w_ref, ba_ref, bx_ref, lam_ref, o_ref,
                xp, a_s, b_s):
    seq = xr_ref.shape[1]
    pad = SUBLANES
    xp[0:pad, :] = jnp.zeros((pad, LANES), F32)
    xp[pad:pad + seq, :] = xr_ref[0]
    lam = lam_ref[...]
    sp = jnp.maximum(-lam, 0.0) + jnp.log(1.0 + jnp.exp(-jnp.abs(lam)))
    w_hi = w_ref[0].astype(BF16)
    w_lo = (w_ref[0] - w_hi.astype(F32)).astype(BF16)
    rows = 256

    def gates(c, carry):
        base = pl.multiple_of(c * rows, rows)
        xc = cb_ref[...] + cw_ref[0:1, :] * xp[pl.ds(base + pad - 3, rows), :]
        for i in range(1, CONV_WIDTH):
            xc = xc + cw_ref[i:i + 1, :] * xp[pl.ds(base + pad - 3 + i, rows), :]
        hi = xc.astype(BF16)
        lo = (xc - hi.astype(F32)).astype(BF16)
        pre = (jnp.dot(hi, w_hi, preferred_element_type=F32)
               + jnp.dot(lo, w_hi, preferred_element_type=F32)
               + jnp.dot(hi, w_lo, preferred_element_type=F32))
        r = jax.nn.sigmoid(pre[:, :LANES] + ba_ref[...])
        ig = jax.nn.sigmoid(pre[:, LANES:] + bx_ref[...])
        log_a = -LRU_C * r * sp
        a = jnp.exp(log_a)
        t = jnp.tanh(log_a)
        b = jnp.sqrt(-2.0 * t / (1.0 - t)) * (ig * xc)
        a_s[pl.ds(base, rows), :] = a
        b_s[pl.ds(base, rows), :] = b
        return carry

    lax.fori_loop(0, seq // rows, gates, 0)

    row = lax.broadcasted_iota(jnp.int32, (SUBLANES, LANES), 0)

    def scan(c, h_prev):
        sl = pl.ds(pl.multiple_of(c * SUBLANES, SUBLANES), SUBLANES)
        a = a_s[sl, :]
        b = b_s[sl, :]
        for s in (1, 2, 4):
            keep = row >= s
            a_sh = jnp.where(keep, pltpu.roll(a, s, axis=0), 1.0)
            b_sh = jnp.where(keep, pltpu.roll(b, s, axis=0), 0.0)
            b = a * b_sh + b
            a = a * a_sh
        h = a * h_prev + b
        o_ref[0, sl, :] = h * jax.nn.gelu(gr_ref[0, sl, :])
        return jnp.broadcast_to(h[SUBLANES - 1:SUBLANES, :], (SUBLANES, LANES))

    lax.fori_loop(0, seq // SUBLANES, scan, jnp.zeros((SUBLANES, LANES), F32), unroll=8)


def _rglru(proj3, conv_w, conv_b, w_bd, b_a, b_x, lam):
    b, seq, _ = proj3.shape
    nt = LRU_WIDTH // LANES
    xr0 = 3 * ATTN_WIDTH // LANES
    gr0 = xr0 + nt
    blk = (1, seq, LANES)
    vec = lambda: pl.BlockSpec((1, LANES), lambda i, j: (0, j))
    return pl.pallas_call(
        _rglru_body,
        grid=(b, nt),
        in_specs=[
            pl.BlockSpec(blk, lambda i, j: (i, 0, xr0 + j)),
            pl.BlockSpec(blk, lambda i, j: (i, 0, gr0 + j)),
            pl.BlockSpec((CONV_WIDTH, LANES), lambda i, j: (0, j)),
            vec(),
            pl.BlockSpec((1, LANES, 2 * LANES), lambda i, j: (j, 0, 0)),
            vec(), vec(), vec(),
        ],
        out_specs=pl.BlockSpec(blk, lambda i, j: (i, 0, j)),
        out_shape=jax.ShapeDtypeStruct((b, seq, LRU_WIDTH), F32),
        scratch_shapes=[
            pltpu.VMEM((seq + SUBLANES, LANES), F32),
            pltpu.VMEM((seq, LANES), F32),
            pltpu.VMEM((seq, LANES), F32),
        ],
        compiler_params=_params(("parallel", "parallel")),
        name="rglru",
    )(proj3, proj3, conv_w, conv_b.reshape(1, -1), w_bd, b_a.reshape(1, -1),
      b_x.reshape(1, -1), lam.reshape(1, -1))


def _out_proj_body(at_ref, rc_ref, ga_ref, gr_ref, x_ref, w_ref, o_ref, h_ref):
    wa = at_ref.shape[1]

    @pl.when(pl.program_id(1) == 0)
    def _():
        h_ref[:, :wa] = _rms(at_ref[...], ga_ref[...]).astype(BF16)
        h_ref[:, wa:] = _rms(rc_ref[...], gr_ref[...]).astype(BF16)

    o_ref[...] = x_ref[...] + jnp.dot(h_ref[...], w_ref[...].astype(BF16),
                                      preferred_element_type=F32)


def _out_proj(attn2, rec2, g_attn, g_rec, x2, w_out, tm=512, tn=1024):
    n, d = x2.shape
    wa, wr = attn2.shape[1], rec2.shape[1]
    return pl.pallas_call(
        _out_proj_body,
        grid=(n // tm, d // tn),
        in_specs=[
            pl.BlockSpec((tm, wa), lambda i, j: (i, 0)),
            pl.BlockSpec((tm, wr), lambda i, j: (i, 0)),
            pl.BlockSpec((1, wa), lambda i, j: (0, 0)),
            pl.BlockSpec((1, wr), lambda i, j: (0, 0)),
            pl.BlockSpec((tm, tn), lambda i, j: (i, j)),
            pl.BlockSpec((d, tn), lambda i, j: (0, j)),
        ],
        out_specs=pl.BlockSpec((tm, tn), lambda i, j: (i, j)),
        out_shape=jax.ShapeDtypeStruct((n, d), F32),
        scratch_shapes=[pltpu.VMEM((tm, d), BF16)],
        compiler_params=_params(("parallel", "arbitrary")),
        name="out_proj",
    )(attn2, rec2, g_attn.reshape(1, wa), g_rec.reshape(1, wr), x2, w_out)


def _router_body(x_ref, g_ref, wt_ref, b_ref, xn_ref, idx_ref, gate_ref, rank_ref, cnt_ref,
                 base):
    tm = x_ref.shape[0]

    @pl.when(pl.program_id(0) == 0)
    def _():
        base[...] = jnp.zeros_like(base)

    xn = _rms(x_ref[...], g_ref[...])
    xn_ref[...] = xn
    x_hi = xn.astype(BF16)
    x_lo = (xn - x_hi.astype(F32)).astype(BF16)
    w = wt_ref[...]
    w_hi = w.astype(BF16)
    w_lo = (w - w_hi.astype(F32)).astype(BF16)
    nt = (((1,), (1,)), ((), ()))
    logits = (lax.dot_general(w_hi, x_hi, nt, preferred_element_type=F32)
              + lax.dot_general(w_hi, x_lo, nt, preferred_element_type=F32)
              + lax.dot_general(w_lo, x_hi, nt, preferred_element_type=F32)
              + b_ref[...])

    eid = lax.broadcasted_iota(jnp.int32, (N_EXPERTS, tm), 0)
    work = logits
    vals, hots = [], []
    for _ in range(TOP_K):
        best = jnp.max(work, axis=0, keepdims=True)
        pick = jnp.min(jnp.where(work == best, eid, N_EXPERTS), axis=0, keepdims=True)
        hot = eid == pick
        vals.append(best)
        hots.append(hot)
        work = jnp.where(hot, -jnp.inf, work)
        idx_ref[len(vals) - 1:len(vals), :] = pick

    ex = [jnp.exp(v - vals[0]) for v in vals]
    den = ex[0] + ex[1] + ex[2] + ex[3]
    for k in range(TOP_K):
        gate_ref[k:k + 1, :] = ex[k] / den

    chosen = (hots[0] | hots[1] | hots[2] | hots[3])
    si = lax.broadcasted_iota(jnp.int32, (tm, tm), 0)
    ti = lax.broadcasted_iota(jnp.int32, (tm, tm), 1)
    before = (si < ti).astype(BF16)
    prefix = jnp.dot(chosen.astype(BF16), before, preferred_element_type=F32)
    slot = base[:, 0:1] + prefix
    for k in range(TOP_K):
        rank_ref[k:k + 1, :] = jnp.sum(jnp.where(hots[k], slot, 0.0), axis=0,
                                       keepdims=True).astype(jnp.int32)
    base[...] = base[...] + jnp.sum(chosen.astype(F32), axis=1, keepdims=True)
    cnt_ref[...] = base[...].astype(jnp.int32)


def _router(x2, g, w_router_t, b_router, tm=256):
    n, d = x2.shape
    row = lambda: pl.BlockSpec((TOP_K, tm), lambda i: (0, i))
    return pl.pallas_call(
        _router_body,
        grid=(n // tm,),
        in_specs=[
            pl.BlockSpec((tm, d), lambda i: (i, 0)),
            pl.BlockSpec((1, d), lambda i: (0, 0)),
            pl.BlockSpec((N_EXPERTS, d), lambda i: (0, 0)),
            pl.BlockSpec((N_EXPERTS, 1), lambda i: (0, 0)),
        ],
        out_specs=[
            pl.BlockSpec((tm, d), lambda i: (i, 0)),
            row(), row(), row(),
            pl.BlockSpec((N_EXPERTS, LANES), lambda i: (0, 0)),
        ],
        out_shape=[
            jax.ShapeDtypeStruct((n, d), F32),
            jax.ShapeDtypeStruct((TOP_K, n), jnp.int32),
            jax.ShapeDtypeStruct((TOP_K, n), F32),
            jax.ShapeDtypeStruct((TOP_K, n), jnp.int32),
            jax.ShapeDtypeStruct((N_EXPERTS, LANES), jnp.int32),
        ],
        scratch_shapes=[pltpu.VMEM((N_EXPERTS, LANES), F32)],
        compiler_params=_params(("arbitrary",)),
        name="router",
    )(x2, g.reshape(1, d), w_router_t, b_router.reshape(N_EXPERTS, 1))


def _scatter_body(dest_ref, pad_ref, xn_ref, o_ref, zrow, sem, *, n_tokens, pad_per_step):
    tm = xn_ref.shape[0]
    i = pl.program_id(0)

    @pl.when(i == 0)
    def _():
        zrow[...] = jnp.zeros_like(zrow)

    def row_copy(t, k):
        d = dest_ref[k * n_tokens + i * tm + t]
        return pltpu.make_async_copy(xn_ref.at[pl.ds(t, 1), :], o_ref.at[pl.ds(d, 1), :], sem)

    def pad_copy(q):
        d = pad_ref[i * pad_per_step + q]
        return pltpu.make_async_copy(zrow.at[pl.ds(0, 1), :], o_ref.at[pl.ds(d, 1), :], sem)

    def issue(t, c):
        for k in range(TOP_K):
            row_copy(t, k).start(priority=k % 2)
        return c

    lax.fori_loop(0, tm, issue, 0)

    def issue_pad(q, c):
        pad_copy(q).start()
        return c

    lax.fori_loop(0, pad_per_step, issue_pad, 0)

    def drain(t, c):
        for k in range(TOP_K):
            row_copy(t, k).wait()
        return c

    lax.fori_loop(0, tm, drain, 0)

    def drain_pad(q, c):
        pad_copy(q).wait()
        return c

    lax.fori_loop(0, pad_per_step, drain_pad, 0)


def _scatter(dest_flat, pad_dest, xn, n_rows, tm=256):
    n, d = xn.shape
    steps = n // tm
    pad_per_step = pad_dest.shape[0] // steps
    return pl.pallas_call(
        functools.partial(_scatter_body, n_tokens=n, pad_per_step=pad_per_step),
        grid_spec=pltpu.PrefetchScalarGridSpec(
            num_scalar_prefetch=2,
            grid=(steps,),
            in_specs=[pl.BlockSpec((tm, d), lambda i, *_: (i, 0))],
            out_specs=pl.BlockSpec(memory_space=pl.ANY),
            scratch_shapes=[pltpu.VMEM((SUBLANES, d), F32), pltpu.SemaphoreType.DMA(())],
        ),
        out_shape=jax.ShapeDtypeStruct((n_rows, d), F32),
        compiler_params=_params(("arbitrary",), unchecked_dma=True),
        name="scatter",
    )(dest_flat, pad_dest, xn)


def _moe_body(meta_ref, x_hbm, w_ref, b_ref, o_hbm, xbf, stage, wbf, obuf, sem_in, sem_out,
              pend, *, prep_w, epilogue):
    t, j = pl.program_id(0), pl.program_id(1)
    row0 = meta_ref[META_STRIDE + t]
    nsub = meta_ref[2 * META_STRIDE + t]
    zero = meta_ref[3 * META_STRIDE + t]
    tn_out = obuf.shape[2]
    col0 = pl.multiple_of(j * tn_out, tn_out)

    @pl.when((t == 0) & (j == 0))
    def _():
        pend[0] = 0
        pend[1] = 0

    def in_copy(c, slot):
        src = x_hbm.at[pl.ds(pl.multiple_of(row0 + c * MOE_BLOCK, MOE_BLOCK), MOE_BLOCK), :]
        if stage is None:
            dst = xbf.at[pl.ds(pl.multiple_of(c * MOE_BLOCK, MOE_BLOCK), MOE_BLOCK), :]
        else:
            dst = stage.at[slot]
        return pltpu.make_async_copy(src, dst, sem_in.at[slot])

    @pl.when((j == 0) & (nsub > 0) & (zero == 0))
    def _load_rows():
        if stage is None:
            def start(c, carry):
                in_copy(c, 0).start()
                return carry

            def wait(c, carry):
                in_copy(c, 0).wait()
                return carry

            lax.fori_loop(0, nsub, start, 0)
            lax.fori_loop(0, nsub, wait, 0)
        else:
            in_copy(0, 0).start()

            def body(c, carry):
                slot = c % 2

                @pl.when(c + 1 < nsub)
                def _():
                    in_copy(c + 1, 1 - slot).start()

                in_copy(c, slot).wait()
                rows = pl.ds(pl.multiple_of(c * MOE_BLOCK, MOE_BLOCK), MOE_BLOCK)
                xbf[rows, :] = stage[slot].astype(BF16)
                return carry

            lax.fori_loop(0, nsub, body, 0)

    def out_copy(s, slot):
        rows = pl.ds(pl.multiple_of(row0 + s * MOE_BLOCK, MOE_BLOCK), MOE_BLOCK)
        return pltpu.make_async_copy(obuf.at[slot], o_hbm.at[rows, pl.ds(col0, tn_out)],
                                     sem_out.at[slot])

    def emit(s, value_fn):
        slot = s % 2

        @pl.when(pend[slot] == 1)
        def _():
            out_copy(s, slot).wait()

        obuf[slot] = value_fn()
        out_copy(s, slot).start()
        pend[slot] = 1

    @pl.when((nsub > 0) & (zero == 0))
    def _compute():
        prep_w(w_ref, wbf)

        def sub(s, carry):
            rows = pl.ds(pl.multiple_of(s * MOE_BLOCK, MOE_BLOCK), MOE_BLOCK)
            emit(s, lambda: epilogue(xbf[rows, :], wbf, b_ref))
            return carry

        lax.fori_loop(0, nsub, sub, 0)

    @pl.when((nsub > 0) & (zero == 1))
    def _zeros():
        def sub(s, carry):
            emit(s, lambda: jnp.zeros(obuf.shape[1:], obuf.dtype))
            return carry

        lax.fori_loop(0, nsub, sub, 0)

    @pl.when((t == pl.num_programs(0) - 1) & (j == pl.num_programs(1) - 1))
    def _drain():
        for slot in range(2):
            @pl.when(pend[slot] == 1)
            def _():
                out_copy(0, slot).wait()


def _moe_call(name, meta, x, w, b, out_cols, out_dtype, tn, tn_out, prep_w, epilogue, tmp_shape):
    p, kdim = x.shape
    n_items = _max_items(p)
    n_tiles = w.shape[2] // tn
    needs_stage = x.dtype != BF16

    def wmap(t, j, m):
        live = (m[2 * META_STRIDE + t] > 0) & (m[3 * META_STRIDE + t] == 0)
        return (m[t], 0, jnp.where(live, j, n_tiles - 1))

    scratch = [pltpu.VMEM((ITEM_ROWS, kdim), BF16)]
    if needs_stage:
        scratch.append(pltpu.VMEM((2, MOE_BLOCK, kdim), x.dtype))
    scratch += [
        pltpu.VMEM((kdim, tn), BF16),
        pltpu.VMEM((2, MOE_BLOCK, tn_out), out_dtype),
        pltpu.SemaphoreType.DMA((2,)),
        pltpu.SemaphoreType.DMA((2,)),
        pltpu.SMEM((2,), jnp.int32),
    ]
    if tmp_shape is not None:
        scratch.append(pltpu.VMEM(tmp_shape, F32))

    def body(meta_ref, x_hbm, w_ref, b_ref, o_hbm, xbf, *rest):
        rest = list(rest)
        stage = rest.pop(0) if needs_stage else None
        wbf, obuf, sem_in, sem_out, pend = rest[:5]
        tmp = rest[5] if tmp_shape is not None else None
        _moe_body(meta_ref, x_hbm, w_ref, b_ref, o_hbm, xbf, stage, wbf, obuf, sem_in, sem_out,
                  pend, prep_w=functools.partial(prep_w, tmp=tmp), epilogue=epilogue)

    return pl.pallas_call(
        body,
        grid_spec=pltpu.PrefetchScalarGridSpec(
            num_scalar_prefetch=1,
            grid=(n_items, n_tiles),
            in_specs=[
                pl.BlockSpec(memory_space=pl.ANY),
                pl.BlockSpec((1, kdim, tn), wmap),
                pl.BlockSpec((1, 1, tn), wmap),
            ],
            out_specs=pl.BlockSpec(memory_space=pl.ANY),
            scratch_shapes=scratch,
        ),
        out_shape=jax.ShapeDtypeStruct((p, out_cols), out_dtype),
        compiler_params=_params(("arbitrary", "arbitrary")),
        name=name,
    )(meta, x, w, b.reshape(N_EXPERTS, 1, -1))


def _cast_weights(w_ref, wbf, tmp=None, rows=256):
    def cast(c, carry):
        sl = pl.ds(pl.multiple_of(c * rows, rows), rows)
        wbf[sl, :] = w_ref[0, sl, :].astype(BF16)
        return carry

    lax.fori_loop(0, wbf.shape[0] // rows, cast, 0)


def _swiglu_tile(x, wbf, b_ref):
    even = lax.broadcasted_iota(jnp.int32, (x.shape[0], LANES), 1) % 2 == 0
    cols = []
    for c in range(wbf.shape[1] // (2 * LANES)):
        c0 = c * 2 * LANES
        gu = jnp.dot(x, wbf[:, c0:c0 + 2 * LANES], preferred_element_type=F32)
        gu = gu + b_ref[0, :, c0:c0 + 2 * LANES]
        v1, v2 = gu[:, :LANES], gu[:, LANES:]
        gate = jnp.where(even, v1, pltpu.roll(v2, 1, axis=1))
        up = jnp.where(even, pltpu.roll(v1, LANES - 1, axis=1), v2)
        gate = jnp.minimum(gate, SWIGLU_LIMIT)
        up = jnp.clip(up, -SWIGLU_LIMIT, SWIGLU_LIMIT)
        cols.append((gate * jax.nn.sigmoid(SWIGLU_ALPHA * gate) * (up + 1.0)).astype(BF16))
    return jnp.concatenate(cols, axis=1)


def _gate_up(meta, x_sorted, w_gate_up, b_gate_up, tn=1024):
    return _moe_call("gate_up", meta, x_sorted, w_gate_up, b_gate_up, w_gate_up.shape[2] // 2,
                     BF16, tn, tn // 2, _cast_weights, _swiglu_tile, None)


def _permute_weights(w_ref, wbf, tmp):
    half = LANES // 2

    def permute(g, carry):
        base = pl.multiple_of(g * LANES, LANES)
        for c in range(tmp.shape[0]):
            cs = slice(c * LANES, (c + 1) * LANES)
            tmp[c, pl.ds(0, half, stride=2), :] = w_ref[0, pl.ds(base, half), cs]
            tmp[c, pl.ds(1, half, stride=2), :] = w_ref[0, pl.ds(base + half, half), cs]
            wbf[pl.ds(base, LANES), cs] = tmp[c].astype(BF16)
        return carry

    lax.fori_loop(0, wbf.shape[0] // LANES, permute, 0)


def _linear_tile(a, wbf, b_ref):
    return jnp.dot(a, wbf[...], preferred_element_type=F32) + b_ref[0]


def _down(meta, act, w_down, b_down, tn=1024):
    return _moe_call("down", meta, act, w_down, b_down, w_down.shape[2], F32, tn, tn,
                     _permute_weights, _linear_tile, (tn // LANES, LANES, LANES))


def _combine_body(dest_ref, y_ref, x_ref, gate_ref, g_ref, o_ref, ybuf, sem, *, n_tokens):
    tm = x_ref.shape[0]
    i = pl.program_id(0)

    def row_copy(t, k):
        d = dest_ref[k * n_tokens + i * tm + t]
        return pltpu.make_async_copy(y_ref.at[pl.ds(d, 1), :], ybuf.at[k, pl.ds(t, 1), :], sem)

    def issue(t, c):
        for k in range(TOP_K):
            row_copy(t, k).start(priority=k % 2)
        return c

    lax.fori_loop(0, tm, issue, 0)

    def drain(t, c):
        for k in range(TOP_K):
            row_copy(t, k).wait()
        return c

    lax.fori_loop(0, tm, drain, 0)

    acc = x_ref[...]
    for k in range(TOP_K):
        acc = acc + gate_ref[:, k:k + 1] * ybuf[k]
    o_ref[...] = _rms(acc, g_ref[...])


def _combine(dest_flat, y_buf, x2, gates, g, tm=256):
    n, d = x2.shape
    return pl.pallas_call(
        functools.partial(_combine_body, n_tokens=n),
        grid_spec=pltpu.PrefetchScalarGridSpec(
            num_scalar_prefetch=1,
            grid=(n // tm,),
            in_specs=[
                pl.BlockSpec(memory_space=pl.ANY),
                pl.BlockSpec((tm, d), lambda i, *_: (i, 0)),
                pl.BlockSpec((tm, TOP_K), lambda i, *_: (i, 0)),
                pl.BlockSpec((1, d), lambda i, *_: (0, 0)),
            ],
            out_specs=pl.BlockSpec((tm, d), lambda i, *_: (i, 0)),
            scratch_shapes=[pltpu.VMEM((TOP_K, tm, d), F32), pltpu.SemaphoreType.DMA(())],
        ),
        out_shape=jax.ShapeDtypeStruct((n, d), F32),
        compiler_params=_params(("arbitrary",), unchecked_dma=True),
        name="combine",
    )(dest_flat, y_buf, x2, gates, g.reshape(1, d))


def _cumsum_sublanes(x):
    row = lax.broadcasted_iota(jnp.int32, x.shape, 0)
    s = 1
    while s < x.shape[0]:
        x = x + jnp.where(row >= s, pltpu.roll(x, s, axis=0), 0)
        s *= 2
    return x


def _tables_body(cnt_ref, idx_ref, rank_ref, dest_ref, meta_ref, pad_ref, *, n_rows):
    cnt = cnt_ref[...]
    padded = (cnt + (MOE_BLOCK - 1)) // MOE_BLOCK * MOE_BLOCK
    pend = _cumsum_sublanes(padded)
    pstart = pend - padded
    total = pend[N_EXPERTS - 1:N_EXPERTS, 0:1]

    def lookup(table, sel):
        eid = lax.broadcasted_iota(jnp.int32, (N_EXPERTS, sel.shape[1]), 0)
        return jnp.sum(jnp.where(eid == sel, table[:, 0:1], 0), axis=0, keepdims=True)

    def count_le(table, v):
        return jnp.sum((table[:, 0:1] <= v).astype(jnp.int32), axis=0, keepdims=True)

    width = 2048
    for k in range(TOP_K):
        for c in range(idx_ref.shape[1] // width):
            sl = slice(c * width, (c + 1) * width)
            dest_ref[k:k + 1, sl] = lookup(pstart, idx_ref[k:k + 1, sl]) + rank_ref[k:k + 1, sl]

    n_it = (padded + (ITEM_ROWS - 1)) // ITEM_ROWS
    it_end = _cumsum_sublanes(n_it)
    t = lax.broadcasted_iota(jnp.int32, (1, META_STRIDE), 1)
    e = count_le(it_end, t)
    ec = jnp.minimum(e, N_EXPERTS - 1)
    k = t - lookup(it_end - n_it, ec)
    rows_real = jnp.minimum(lookup(padded, ec) - k * ITEM_ROWS, ITEM_ROWS)
    row0_zero = total + (t - it_end[N_EXPERTS - 1:N_EXPERTS, 0:1]) * ITEM_ROWS
    rows_zero = jnp.clip(n_rows - row0_zero, 0, ITEM_ROWS)
    real = e < N_EXPERTS
    meta_ref[:, 0:META_STRIDE] = ec
    meta_ref[:, META_STRIDE:2 * META_STRIDE] = jnp.where(
        real, lookup(pstart, ec) + k * ITEM_ROWS, jnp.minimum(row0_zero, n_rows - MOE_BLOCK))
    meta_ref[:, 2 * META_STRIDE:3 * META_STRIDE] = jnp.where(real, rows_real, rows_zero) // MOE_BLOCK
    meta_ref[:, 3 * META_STRIDE:4 * META_STRIDE] = jnp.where(real, 0, 1)

    gap = padded - cnt
    gap_end = _cumsum_sublanes(gap)
    q = lax.broadcasted_iota(jnp.int32, pad_ref.shape, 1)
    eq = count_le(gap_end, q)
    inside = lookup(pstart + cnt - (gap_end - gap), eq) + q
    tail = total + q - gap_end[N_EXPERTS - 1:N_EXPERTS, 0:1]
    pad_ref[...] = jnp.where(eq < N_EXPERTS, inside, tail)


def _max_items(n_rows):
    return N_EXPERTS + n_rows // ITEM_ROWS + (N_EXPERTS * MOE_BLOCK) // ITEM_ROWS + 1


def _tables(cnt, idx_t, rank_t, n_rows):
    n_pad = n_rows - idx_t.size
    assert _max_items(n_rows) <= META_STRIDE
    dest, meta, pad_dest = pl.pallas_call(
        functools.partial(_tables_body, n_rows=n_rows),
        out_shape=[
            jax.ShapeDtypeStruct(idx_t.shape, jnp.int32),
            jax.ShapeDtypeStruct((1, 4 * META_STRIDE), jnp.int32),
            jax.ShapeDtypeStruct((1, n_pad), jnp.int32),
        ],
        name="tables",
    )(cnt, idx_t, rank_t)
    return dest.reshape(-1), meta.reshape(-1), pad_dest.reshape(-1)


def kernel(x, norm_mix, w_in, conv_w, conv_b, w_a, b_a, w_x, b_x, lru_lambda, attn_out_norm, lru_out_norm, w_out, norm_ffn, w_router, b_router, w_gate_up, b_gate_up, w_down, b_down, norm_final):
    b, seq, d = x.shape
    n = b * seq
    x2 = x.reshape(n, d)

    proj = _in_proj(x2, norm_mix, w_in)
    proj3 = proj.reshape(b, seq, IN_COLS)

    slopes = jnp.asarray(2.0 ** (-8.0 * np.arange(1, N_HEADS + 1) / N_HEADS), F32)
    attn = _attention(proj3, slopes)

    def pair_blocks(w):
        w4 = w.reshape(-1, 2, LRU_BLOCK, LRU_BLOCK)
        z = jnp.zeros_like(w4[:, 0])
        top = jnp.concatenate([w4[:, 0], z], axis=2)
        bot = jnp.concatenate([z, w4[:, 1]], axis=2)
        return jnp.concatenate([top, bot], axis=1)

    w_bd = jnp.concatenate([pair_blocks(w_a), pair_blocks(w_x)], axis=2)
    rec = _rglru(proj3, conv_w, conv_b, w_bd, b_a, b_x, lru_lambda)

    x_mid = _out_proj(attn.reshape(n, ATTN_WIDTH), rec.reshape(n, LRU_WIDTH),
                      attn_out_norm, lru_out_norm, x2, w_out)

    xn, idx_t, gate_t, rank_t, cnt = _router(x_mid, norm_ffn, w_router.T, b_router)
    n_rows = n * TOP_K + N_EXPERTS * MOE_BLOCK
    dest_flat, meta, pad_dest = _tables(cnt, idx_t, rank_t, n_rows)

    x_sorted = _scatter(dest_flat, pad_dest, xn, n_rows)
    act = _gate_up(meta, x_sorted, w_gate_up, b_gate_up)
    y_buf = _down(meta, act, w_down, b_down)
    out = _combine(dest_flat, y_buf, x_mid, gate_t.T, norm_final)
    return out.reshape(b, seq, d)
```

```python
import functools

import numpy as np
import jax
import jax.numpy as jnp
from jax import lax
from jax.experimental import pallas as pl
from jax.experimental.pallas import tpu as pltpu

F32 = jnp.float32
BF16 = jnp.bfloat16

D_MODEL = 2048
HEAD_DIM = 64
N_HEADS = 16
ATTN_WIDTH = N_HEADS * HEAD_DIM
LRU_WIDTH = D_MODEL - ATTN_WIDTH
LRU_BLOCK = 64
CONV_WIDTH = 4
LRU_C = 8.0
IN_COLS = 3 * ATTN_WIDTH + 2 * LRU_WIDTH
DILATIONS = (1, 4, 16)
ATTN_BLOCK = 128
UNITS_PER_TRIP = (5, 6, 8)
N_EXPERTS = 32
TOP_K = 4
D_FF = D_MODEL
SWIGLU_LIMIT = 7.0
SWIGLU_ALPHA = 1.702
MOE_BLOCK = 256
ITEM_ROWS = 8 * MOE_BLOCK
META_STRIDE = 128
TAIL_SLOT = 2
EPS = 1e-6

LANES = 128
SUBLANES = 8
VMEM_LIMIT = 56 * 1024 * 1024


def _params(sem, vmem=VMEM_LIMIT):
    return pltpu.CompilerParams(dimension_semantics=sem, vmem_limit_bytes=vmem)


def _rms(xf, g):
    return xf * lax.rsqrt(jnp.mean(xf * xf, axis=-1, keepdims=True) + EPS) * g


def _in_proj_body(x_ref, g_ref, w_ref, o_ref, h_ref):
    @pl.when(pl.program_id(1) == 0)
    def _():
        h_ref[...] = _rms(x_ref[...], g_ref[...]).astype(BF16)

    o_ref[...] = jnp.dot(h_ref[...], w_ref[...].astype(BF16), preferred_element_type=F32)


def _in_proj(x2, g, w_in, tm=1024, tn=512):
    n, d = x2.shape
    cols = w_in.shape[1]
    return pl.pallas_call(
        _in_proj_body,
        grid=(n // tm, cols // tn),
        in_specs=[
            pl.BlockSpec((tm, d), lambda i, j: (i, 0)),
            pl.BlockSpec((1, d), lambda i, j: (0, 0)),
            pl.BlockSpec((d, tn), lambda i, j: (0, j)),
        ],
        out_specs=pl.BlockSpec((tm, tn), lambda i, j: (i, j)),
        out_shape=jax.ShapeDtypeStruct((n, cols), F32),
        scratch_shapes=[pltpu.VMEM((tm, d), BF16)],
        compiler_params=_params(("parallel", "arbitrary")),
        name="in_proj",
    )(x2, g.reshape(1, d), w_in)


REGROUP = 4


def _attention_body(slope_ref, q_ref, k_ref, v_ref, o_ref, qh, kh, vh, acc, mx):
    seq = q_ref.shape[1]
    run = seq // REGROUP
    pair = pl.program_id(1)
    qi = lax.broadcasted_iota(jnp.int32, (ATTN_BLOCK, 2 * ATTN_BLOCK), 0)
    ki = lax.broadcasted_iota(jnp.int32, (ATTN_BLOCK, 2 * ATTN_BLOCK), 1)
    rel2 = qi + ATTN_BLOCK - ki
    ok2 = (rel2 >= 0) & (rel2 <= ATTN_BLOCK)
    rel1 = (qi - ki)[:, :ATTN_BLOCK]
    ok1 = rel1 >= 0

    nh = LANES // HEAD_DIM
    slopes = [slope_ref[pair * nh + hh] for hh in range(nh)]
    def stage(order, rows_dst, rows_src):
        lane = lax.broadcasted_iota(jnp.int32, (run, LANES), 1)
        qv, vv = q_ref[0, rows_src, :] * (HEAD_DIM ** -0.5), v_ref[0, rows_src, :]
        if order:
            kh[rows_dst, :] = k_ref[0, rows_src, :]
        for hh in range(nh):
            mine = (lane >= hh * HEAD_DIM) & (lane < (hh + 1) * HEAD_DIM)
            qh[order, hh, rows_dst, :] = jnp.where(mine, qv, 0.0)
            vh[order, hh, rows_dst, :] = jnp.where(mine, vv, 1.0)

    for c in range(REGROUP):
        stage(0, slice(c * run, (c + 1) * run), pl.ds(c * run, run))
        stage(1, slice(c * run, (c + 1) * run), pl.ds(c, run, stride=REGROUP))

    def run_units(units):
        loaded = [(qh[order, hh, qs, :].astype(BF16),
                   (kh[ks, :] if order else k_ref[0, ks, :]).astype(BF16),
                   vh[order, hh, ks, :].astype(BF16)) for hh, br, order, qs, ks, bias in units]
        scores = [lax.dot_general(q, k, (((1,), (1,)), ((), ())), preferred_element_type=F32)
                  + u[5] for (q, k, v), u in zip(loaded, units)]
        maxes = [jnp.max(s, axis=1, keepdims=True) for s in scores]
        probs = [jnp.exp(s - m).astype(BF16) for s, m in zip(scores, maxes)]
        results = [(jnp.dot(p, v, preferred_element_type=F32), m)
                   for p, (q, k, v), m in zip(probs, loaded, maxes)]
        for (a, m), (hh, br, order, qs, ks, bias) in zip(results, units):
            acc[hh, br, qs, :] = a
            mx[hh, br, qs, :] = jnp.broadcast_to(m, (ATTN_BLOCK, LANES))

    def bias_pair(ok, rel, d):
        return [jnp.where(ok, -sl * (rel * d).astype(F32), -jnp.inf) for sl in slopes]

    def contiguous_branch(br, order, d, n_runs, per_trip):
        nb = seq // (n_runs * ATTN_BLOCK)
        bias1, bias2 = bias_pair(ok1, rel1, d), bias_pair(ok2, rel2, d)
        run_units([(hh, br, order, pl.ds(r * nb * ATTN_BLOCK, ATTN_BLOCK),
                    pl.ds(r * nb * ATTN_BLOCK, ATTN_BLOCK), bias1[hh])
                   for r in range(n_runs) for hh in range(nh)])

        def later(g, carry):
            units = []
            for u in range(per_trip):
                idx = g * per_trip + u
                blk = idx // (nb - 1) * nb + idx % (nb - 1) + 1
                start = pl.multiple_of(blk * ATTN_BLOCK, ATTN_BLOCK)
                units += [(hh, br, order, pl.ds(start, ATTN_BLOCK),
                           pl.ds(start - ATTN_BLOCK, 2 * ATTN_BLOCK), bias2[hh])
                          for hh in range(nh)]
            run_units(units)
            return carry

        lax.fori_loop(0, n_runs * (nb - 1) // per_trip, later, 0)

    contiguous_branch(0, 0, DILATIONS[0], 1, UNITS_PER_TRIP[0])
    contiguous_branch(1, 1, DILATIONS[1], REGROUP, UNITS_PER_TRIP[1])

    d2 = DILATIONS[2]
    sub = d2 // REGROUP
    bias16 = bias_pair(ok1, rel1, d2)

    def strided(g, carry):
        units = []
        for u in range(UNITS_PER_TRIP[2]):
            idx = g * UNITS_PER_TRIP[2] + u
            rows = pl.ds(idx // sub * run + idx % sub, ATTN_BLOCK, stride=sub)
            units += [(hh, 2, 1, rows, rows, bias16[hh]) for hh in range(nh)]
        run_units(units)
        return carry

    lax.fori_loop(0, d2 // UNITS_PER_TRIP[2], strided, 0)

    def merge(c, carry):
        start = c // (run // ATTN_BLOCK) + c % (run // ATTN_BLOCK) * (ATTN_BLOCK * REGROUP)
        nat = pl.ds(start, ATTN_BLOCK, stride=REGROUP)
        reg = pl.ds(pl.multiple_of(c * ATTN_BLOCK, ATTN_BLOCK), ATTN_BLOCK)
        outs = []
        for hh in range(nh):
            m0, m1, m2 = mx[hh, 0, nat, :], mx[hh, 1, reg, :], mx[hh, 2, reg, :]
            mt = jnp.maximum(jnp.maximum(m0, m1), m2)
            tot = (jnp.exp(m0 - mt) * acc[hh, 0, nat, :] + jnp.exp(m1 - mt) * acc[hh, 1, reg, :]
                   + jnp.exp(m2 - mt) * acc[hh, 2, reg, :])
            den = pltpu.roll(tot, HEAD_DIM, axis=1)
            outs.append(tot / den)
        lane_r = lax.broadcasted_iota(jnp.int32, (ATTN_BLOCK, LANES), 1)
        o_ref[0, nat, :] = jnp.where(lane_r < HEAD_DIM, outs[0], outs[1])
        return carry

    lax.fori_loop(0, seq // ATTN_BLOCK, merge, 0, unroll=2)


def _attention(proj3, slopes):
    b, seq, _ = proj3.shape
    npair = ATTN_WIDTH // LANES
    nh = LANES // HEAD_DIM
    blk = (1, seq, LANES)
    return pl.pallas_call(
        _attention_body,
        grid_spec=pltpu.PrefetchScalarGridSpec(
            num_scalar_prefetch=1,
            grid=(b, npair),
            in_specs=[
                pl.BlockSpec(blk, lambda i, j, s: (i, 0, j)),
                pl.BlockSpec(blk, lambda i, j, s: (i, 0, npair + j)),
                pl.BlockSpec(blk, lambda i, j, s: (i, 0, 2 * npair + j)),
            ],
            out_specs=pl.BlockSpec(blk, lambda i, j, s: (i, 0, j)),
            scratch_shapes=[
                pltpu.VMEM((2, nh, seq, LANES), F32),
                pltpu.VMEM((seq, LANES), F32),
                pltpu.VMEM((2, nh, seq, LANES), F32),
                pltpu.VMEM((nh, len(DILATIONS), seq, LANES), F32),
                pltpu.VMEM((nh, len(DILATIONS), seq, LANES), F32),
            ],
        ),
        out_shape=jax.ShapeDtypeStruct((b, seq, ATTN_WIDTH), F32),
        compiler_params=_params(("parallel", "parallel")),
        name="attention",
    )(slopes, proj3, proj3, proj3)


def _rglru_body(xr_ref, gr_ref, cw_ref, cb_ref, w_ref, ba_ref, bx_ref, lam_ref, o_ref,
                xp, a_s, b_s):
    seq = xr_ref.shape[1]
    pad = SUBLANES
    xp[0:pad, :] = jnp.zeros((pad, LANES), F32)
    xp[pad:pad + seq, :] = xr_ref[0]
    lam = lam_ref[...]
    sp = jnp.maximum(-lam, 0.0) + jnp.log(1.0 + jnp.exp(-jnp.abs(lam)))
    w_hi = w_ref[0].astype(BF16)
    w_lo = (w_ref[0] - w_hi.astype(F32)).astype(BF16)
    rows = 256

    def gates(c, carry):
        base = pl.multiple_of(c * rows, rows)
        xc = cb_ref[...] + cw_ref[0:1, :] * xp[pl.ds(base + pad - 3, rows), :]
        for i in range(1, CONV_WIDTH):
            xc = xc + cw_ref[i:i + 1, :] * xp[pl.ds(base + pad - 3 + i, rows), :]
        hi = xc.astype(BF16)
        lo = (xc - hi.astype(F32)).astype(BF16)
        pre = (jnp.dot(hi, w_hi, preferred_element_type=F32)
               + jnp.dot(lo, w_hi, preferred_element_type=F32)
               + jnp.dot(hi, w_lo, preferred_element_type=F32))
        r = jax.nn.sigmoid(pre[:, :LANES] + ba_ref[...])
        ig = jax.nn.sigmoid(pre[:, LANES:] + bx_ref[...])
        log_a = -LRU_C * r * sp
        a = jnp.exp(log_a)
        t = jnp.tanh(log_a)
        b = jnp.sqrt(-2.0 * t / (1.0 - t)) * (ig * xc)
        a_s[pl.ds(base, rows), :] = a
        b_s[pl.ds(base, rows), :] = b
        return carry

    lax.fori_loop(0, seq // rows, gates, 0)

    row = lax.broadcasted_iota(jnp.int32, (SUBLANES, LANES), 0)

    def scan(c, h_prev):
        sl = pl.ds(pl.multiple_of(c * SUBLANES, SUBLANES), SUBLANES)
        a = a_s[sl, :]
        b = b_s[sl, :]
        for s in (1, 2, 4):
            keep = row >= s
            a_sh = jnp.where(keep, pltpu.roll(a, s, axis=0), 1.0)
            b_sh = jnp.where(keep, pltpu.roll(b, s, axis=0), 0.0)
            b = a * b_sh + b
            a = a * a_sh
        h = a * h_prev + b
        o_ref[0, sl, :] = h * jax.nn.gelu(gr_ref[0, sl, :])
        return jnp.broadcast_to(h[SUBLANES - 1:SUBLANES, :], (SUBLANES, LANES))

    lax.fori_loop(0, seq // SUBLANES, scan, jnp.zeros((SUBLANES, LANES), F32), unroll=8)


def _rglru(proj3, conv_w, conv_b, w_bd, b_a, b_x, lam):
    b, seq, _ = proj3.shape
    nt = LRU_WIDTH // LANES
    xr0 = 3 * ATTN_WIDTH // LANES
    gr0 = xr0 + nt
    blk = (1, seq, LANES)
    vec = lambda: pl.BlockSpec((1, LANES), lambda i, j: (0, j))
    return pl.pallas_call(
        _rglru_body,
        grid=(b, nt),
        in_specs=[
            pl.BlockSpec(blk, lambda i, j: (i, 0, xr0 + j)),
            pl.BlockSpec(blk, lambda i, j: (i, 0, gr0 + j)),
            pl.BlockSpec((CONV_WIDTH, LANES), lambda i, j: (0, j)),
            vec(),
            pl.BlockSpec((1, LANES, 2 * LANES), lambda i, j: (j, 0, 0)),
            vec(), vec(), vec(),
        ],
        out_specs=pl.BlockSpec(blk, lambda i, j: (i, 0, j)),
        out_shape=jax.ShapeDtypeStruct((b, seq, LRU_WIDTH), F32),
        scratch_shapes=[
            pltpu.VMEM((seq + SUBLANES, LANES), F32),
            pltpu.VMEM((seq, LANES), F32),
            pltpu.VMEM((seq, LANES), F32),
        ],
        compiler_params=_params(("parallel", "parallel")),
        name="rglru",
    )(proj3, proj3, conv_w, conv_b.reshape(1, -1), w_bd, b_a.reshape(1, -1),
      b_x.reshape(1, -1), lam.reshape(1, -1))


def _out_proj_body(at_ref, rc_ref, ga_ref, gr_ref, x_ref, w_ref, o_ref, h_ref):
    wa = at_ref.shape[1]

    @pl.when(pl.program_id(1) == 0)
    def _():
        h_ref[:, :wa] = _rms(at_ref[...], ga_ref[...]).astype(BF16)
        h_ref[:, wa:] = _rms(rc_ref[...], gr_ref[...]).astype(BF16)

    o_ref[...] = x_ref[...] + jnp.dot(h_ref[...], w_ref[...].astype(BF16),
                                      preferred_element_type=F32)


def _out_proj(attn2, rec2, g_attn, g_rec, x2, w_out, tm=512, tn=1024):
    n, d = x2.shape
    wa, wr = attn2.shape[1], rec2.shape[1]
    return pl.pallas_call(
        _out_proj_body,
        grid=(n // tm, d // tn),
        in_specs=[
            pl.BlockSpec((tm, wa), lambda i, j: (i, 0)),
            pl.BlockSpec((tm, wr), lambda i, j: (i, 0)),
            pl.BlockSpec((1, wa), lambda i, j: (0, 0)),
            pl.BlockSpec((1, wr), lambda i, j: (0, 0)),
            pl.BlockSpec((tm, tn), lambda i, j: (i, j)),
            pl.BlockSpec((d, tn), lambda i, j: (0, j)),
        ],
        out_specs=pl.BlockSpec((tm, tn), lambda i, j: (i, j)),
        out_shape=jax.ShapeDtypeStruct((n, d), F32),
        scratch_shapes=[pltpu.VMEM((tm, d), BF16)],
        compiler_params=_params(("parallel", "arbitrary")),
        name="out_proj",
    )(attn2, rec2, g_attn.reshape(1, wa), g_rec.reshape(1, wr), x2, w_out)


def _router_body(x_ref, g_ref, wt_ref, b_ref, xn_ref, idx_ref, gate_ref, rank_ref, cnt_ref,
                 base):
    tm = x_ref.shape[0]

    @pl.when(pl.program_id(0) == 0)
    def _():
        base[...] = jnp.zeros_like(base)

    xn = _rms(x_ref[...], g_ref[...])
    xn_ref[...] = xn
    x_hi = xn.astype(BF16)
    x_lo = (xn - x_hi.astype(F32)).astype(BF16)
    w = wt_ref[...]
    w_hi = w.astype(BF16)
    w_lo = (w - w_hi.astype(F32)).astype(BF16)
    nt = (((1,), (1,)), ((), ()))
    logits = (lax.dot_general(w_hi, x_hi, nt, preferred_element_type=F32)
              + lax.dot_general(w_hi, x_lo, nt, preferred_element_type=F32)
              + lax.dot_general(w_lo, x_hi, nt, preferred_element_type=F32)
              + b_ref[...])

    eid = lax.broadcasted_iota(jnp.int32, (N_EXPERTS, tm), 0)
    work = logits
    vals, hots = [], []
    for _ in range(TOP_K):
        best = jnp.max(work, axis=0, keepdims=True)
        pick = jnp.min(jnp.where(work == best, eid, N_EXPERTS), axis=0, keepdims=True)
        hot = eid == pick
        vals.append(best)
        hots.append(hot)
        work = jnp.where(hot, -jnp.inf, work)
        idx_ref[len(vals) - 1:len(vals), :] = pick

    ex = [jnp.exp(v - vals[0]) for v in vals]
    den = ex[0] + ex[1] + ex[2] + ex[3]
    for k in range(TOP_K):
        gate_ref[k:k + 1, :] = ex[k] / den

    chosen = (hots[0] | hots[1] | hots[2] | hots[3])
    si = lax.broadcasted_iota(jnp.int32, (tm, tm), 0)
    ti = lax.broadcasted_iota(jnp.int32, (tm, tm), 1)
    before = (si < ti).astype(BF16)
    prefix = jnp.dot(chosen.astype(BF16), before, preferred_element_type=F32)
    slot = base[:, 0:1] + prefix
    for k in range(TOP_K):
        rank_ref[k:k + 1, :] = jnp.sum(jnp.where(hots[k], slot, 0.0), axis=0,
                                       keepdims=True).astype(jnp.int32)
    base[...] = base[...] + jnp.sum(chosen.astype(F32), axis=1, keepdims=True)
    cnt_ref[...] = base[...].astype(jnp.int32)


def _router(x2, g, w_router_t, b_router, tm=256):
    n, d = x2.shape
    row = lambda: pl.BlockSpec((TOP_K, tm), lambda i: (0, i))
    return pl.pallas_call(
        _router_body,
        grid=(n // tm,),
        in_specs=[
            pl.BlockSpec((tm, d), lambda i: (i, 0)),
            pl.BlockSpec((1, d), lambda i: (0, 0)),
            pl.BlockSpec((N_EXPERTS, d), lambda i: (0, 0)),
            pl.BlockSpec((N_EXPERTS, 1), lambda i: (0, 0)),
        ],
        out_specs=[
            pl.BlockSpec((tm, d), lambda i: (i, 0)),
            row(), row(), row(),
            pl.BlockSpec((N_EXPERTS, LANES), lambda i: (0, 0)),
        ],
        out_shape=[
            jax.ShapeDtypeStruct((n, d), F32),
            jax.ShapeDtypeStruct((TOP_K, n), jnp.int32),
            jax.ShapeDtypeStruct((TOP_K, n), F32),
            jax.ShapeDtypeStruct((TOP_K, n), jnp.int32),
            jax.ShapeDtypeStruct((N_EXPERTS, LANES), jnp.int32),
        ],
        scratch_shapes=[pltpu.VMEM((N_EXPERTS, LANES), F32)],
        compiler_params=_params(("arbitrary",)),
        name="router",
    )(x2, g.reshape(1, d), w_router_t, b_router.reshape(N_EXPERTS, 1))


def _scatter_body(dest_ref, pad_ref, xn_ref, o_ref, zrow, sem, *, n_tokens, pad_per_step):
    tm = xn_ref.shape[0]
    i = pl.program_id(0)

    @pl.when(i == 0)
    def _():
        zrow[...] = jnp.zeros_like(zrow)

    def row_copy(t, k):
        d = dest_ref[k * n_tokens + i * tm + t]
        return pltpu.make_async_copy(xn_ref.at[pl.ds(t, 1), :], o_ref.at[pl.ds(d, 1), :], sem)

    def pad_copy(q):
        d = pad_ref[i * pad_per_step + q]
        return pltpu.make_async_copy(zrow.at[pl.ds(0, 1), :], o_ref.at[pl.ds(d, 1), :], sem)

    def issue(t, c):
        for k in range(TOP_K):
            row_copy(t, k).start(priority=k % 2)
        return c

    lax.fori_loop(0, tm, issue, 0)

    def issue_pad(q, c):
        pad_copy(q).start()
        return c

    lax.fori_loop(0, pad_per_step, issue_pad, 0)

    def drain(t, c):
        for k in range(TOP_K):
            row_copy(t, k).wait()
        return c

    lax.fori_loop(0, tm, drain, 0)

    def drain_pad(q, c):
        pad_copy(q).wait()
        return c

    lax.fori_loop(0, pad_per_step, drain_pad, 0)


def _scatter(dest_flat, pad_dest, xn, n_rows, tm=256):
    n, d = xn.shape
    steps = n // tm
    pad_per_step = pad_dest.shape[0] // steps
    return pl.pallas_call(
        functools.partial(_scatter_body, n_tokens=n, pad_per_step=pad_per_step),
        grid_spec=pltpu.PrefetchScalarGridSpec(
            num_scalar_prefetch=2,
            grid=(steps,),
            in_specs=[pl.BlockSpec((tm, d), lambda i, *_: (i, 0))],
            out_specs=pl.BlockSpec(memory_space=pl.ANY),
            scratch_shapes=[pltpu.VMEM((SUBLANES, d), F32), pltpu.SemaphoreType.DMA(())],
        ),
        out_shape=jax.ShapeDtypeStruct((n_rows, d), F32),
        compiler_params=_params(("arbitrary",)),
        name="scatter",
    )(dest_flat, pad_dest, xn)


def _moe_body(meta_ref, x_hbm, w_ref, b_ref, o_hbm, xbf, stage, wbf, obuf, sem_in, sem_out,
              pend, *, prep_w, epilogue):
    t, j = pl.program_id(0), pl.program_id(1)
    row0 = meta_ref[META_STRIDE + t]
    nsub = meta_ref[2 * META_STRIDE + t]
    zero = meta_ref[3 * META_STRIDE + t]
    tn_out = obuf.shape[2]
    col0 = pl.multiple_of(j * tn_out, tn_out)

    @pl.when((t == 0) & (j == 0))
    def _():
        for k in range(TAIL_SLOT + 1):
            pend[k] = 0

    def in_copy(c, slot):
        src = x_hbm.at[pl.ds(pl.multiple_of(row0 + c * MOE_BLOCK, MOE_BLOCK), MOE_BLOCK), :]
        if stage is None:
            dst = xbf.at[pl.ds(pl.multiple_of(c * MOE_BLOCK, MOE_BLOCK), MOE_BLOCK), :]
        else:
            dst = stage.at[slot]
        return pltpu.make_async_copy(src, dst, sem_in.at[slot])

    def fetch_rows(c):
        slot = c % 2

        @pl.when(c + 1 < nsub)
        def _():
            in_copy(c + 1, 1 - slot).start()

        in_copy(c, slot).wait()
        if stage is not None:
            rows = pl.ds(pl.multiple_of(c * MOE_BLOCK, MOE_BLOCK), MOE_BLOCK)
            xbf[rows, :] = stage[slot].astype(BF16)

    def out_copy(first_block, slot, tail):
        n_rows = MOE_BLOCK if tail else 2 * MOE_BLOCK
        rows = pl.ds(pl.multiple_of(row0 + first_block * MOE_BLOCK, MOE_BLOCK), n_rows)
        return pltpu.make_async_copy(obuf.at[slot, pl.ds(0, n_rows), :],
                                     o_hbm.at[rows, pl.ds(col0, tn_out)], sem_out.at[slot])

    def emit(first_block, slot, tail, value_fn):
        @pl.when(pend[slot] == 1)
        def _():
            out_copy(first_block, slot, tail).wait()

        obuf[slot, pl.ds(0, MOE_BLOCK if tail else 2 * MOE_BLOCK), :] = value_fn()
        out_copy(first_block, slot, tail).start()
        pend[slot] = 1

    def walk(block_fn):
        def pair(p, carry):
            block_fn(2 * p, p % 2, False)
            return carry

        lax.fori_loop(0, nsub // 2, pair, 0)

        @pl.when(nsub % 2 == 1)
        def _():
            block_fn(nsub - 1, TAIL_SLOT, True)

    @pl.when((nsub > 0) & (zero == 0))
    def _compute():
        @pl.when(j == 0)
        def _():
            in_copy(0, 0).start()

        prep_w(w_ref, wbf)

        def block(first_block, slot, tail):
            n_blocks = 1 if tail else 2

            @pl.when(j == 0)
            def _():
                for k in range(n_blocks):
                    fetch_rows(first_block + k)

            rows = pl.ds(pl.multiple_of(first_block * MOE_BLOCK, MOE_BLOCK), n_blocks * MOE_BLOCK)
            emit(first_block, slot, tail, lambda: epilogue(xbf[rows, :], wbf, b_ref))

        walk(block)

    @pl.when((nsub > 0) & (zero == 1))
    def _zeros():
        walk(lambda first_block, slot, tail: emit(
            first_block, slot, tail,
            lambda: jnp.zeros(((1 if tail else 2) * MOE_BLOCK, tn_out), obuf.dtype)))

    @pl.when((t == pl.num_programs(0) - 1) & (j == pl.num_programs(1) - 1))
    def _drain():
        for k in range(TAIL_SLOT + 1):
            @pl.when(pend[k] == 1)
            def _():
                out_copy(0, k, k == TAIL_SLOT).wait()


def _moe_call(name, meta, x, w, b, out_cols, out_dtype, tn, tn_out, prep_w, epilogue, tmp_shape):
    p, kdim = x.shape
    n_items = _max_items(p)
    n_tiles = w.shape[2] // tn
    needs_stage = x.dtype != BF16

    def wmap(t, j, m):
        live = (m[2 * META_STRIDE + t] > 0) & (m[3 * META_STRIDE + t] == 0)
        return (m[t], 0, jnp.where(live, j, n_tiles - 1))

    scratch = [pltpu.VMEM((ITEM_ROWS, kdim), BF16)]
    if needs_stage:
        scratch.append(pltpu.VMEM((2, MOE_BLOCK, kdim), x.dtype))
    scratch += [
        pltpu.VMEM((kdim, tn), BF16),
        pltpu.VMEM((TAIL_SLOT + 1, 2 * MOE_BLOCK, tn_out), out_dtype),
        pltpu.SemaphoreType.DMA((2,)),
        pltpu.SemaphoreType.DMA((TAIL_SLOT + 1,)),
        pltpu.SMEM((TAIL_SLOT + 1,), jnp.int32),
    ]
    if tmp_shape is not None:
        scratch.append(pltpu.VMEM(tmp_shape, F32))

    def body(meta_ref, x_hbm, w_ref, b_ref, o_hbm, xbf, *rest):
        rest = list(rest)
        stage = rest.pop(0) if needs_stage else None
        wbf, obuf, sem_in, sem_out, pend = rest[:5]
        tmp = rest[5] if tmp_shape is not None else None
        _moe_body(meta_ref, x_hbm, w_ref, b_ref, o_hbm, xbf, stage, wbf, obuf, sem_in, sem_out,
                  pend, prep_w=functools.partial(prep_w, tmp=tmp), epilogue=epilogue)

    return pl.pallas_call(
        body,
        grid_spec=pltpu.PrefetchScalarGridSpec(
            num_scalar_prefetch=1,
            grid=(n_items, n_tiles),
            in_specs=[
                pl.BlockSpec(memory_space=pl.ANY),
                pl.BlockSpec((1, kdim, tn), wmap),
                pl.BlockSpec((1, 1, tn), wmap),
            ],
            out_specs=pl.BlockSpec(memory_space=pl.ANY),
            scratch_shapes=scratch,
        ),
        out_shape=jax.ShapeDtypeStruct((p, out_cols), out_dtype),
        compiler_params=_params(("arbitrary", "arbitrary")),
        name=name,
    )(meta, x, w, b.reshape(N_EXPERTS, 1, -1))


def _cast_weights(w_ref, wbf, tmp=None, rows=256):
    def cast(c, carry):
        sl = pl.ds(pl.multiple_of(c * rows, rows), rows)
        wbf[sl, :] = w_ref[0, sl, :].astype(BF16)
        return carry

    lax.fori_loop(0, wbf.shape[0] // rows, cast, 0)


def _swiglu_tile(x, wbf, b_ref):
    even = lax.broadcasted_iota(jnp.int32, (x.shape[0], LANES), 1) % 2 == 0
    cols = []
    for c in range(wbf.shape[1] // (2 * LANES)):
        c0 = c * 2 * LANES
        gu = jnp.dot(x, wbf[:, c0:c0 + 2 * LANES], preferred_element_type=F32)
        gu = gu + b_ref[0, :, c0:c0 + 2 * LANES]
        v1, v2 = gu[:, :LANES], gu[:, LANES:]
        gate = jnp.where(even, v1, pltpu.roll(v2, 1, axis=1))
        up = jnp.where(even, pltpu.roll(v1, LANES - 1, axis=1), v2)
        gate = jnp.minimum(gate, SWIGLU_LIMIT)
        up = jnp.clip(up, -SWIGLU_LIMIT, SWIGLU_LIMIT)
        cols.append((gate * jax.nn.sigmoid(SWIGLU_ALPHA * gate) * (up + 1.0)).astype(BF16))
    return jnp.concatenate(cols, axis=1)


def _gate_up(meta, x_sorted, w_gate_up, b_gate_up, tn=1024):
    return _moe_call("gate_up", meta, x_sorted, w_gate_up, b_gate_up, w_gate_up.shape[2] // 2,
                     BF16, tn, tn // 2, _cast_weights, _swiglu_tile, None)


def _permute_weights(w_ref, wbf, tmp):
    half = LANES // 2

    def permute(g, carry):
        base = pl.multiple_of(g * LANES, LANES)
        for c in range(tmp.shape[0]):
            cs = slice(c * LANES, (c + 1) * LANES)
            tmp[c, pl.ds(0, half, stride=2), :] = w_ref[0, pl.ds(base, half), cs]
            tmp[c, pl.ds(1, half, stride=2), :] = w_ref[0, pl.ds(base + half, half), cs]
            wbf[pl.ds(base, LANES), cs] = tmp[c].astype(BF16)
        return carry

    lax.fori_loop(0, wbf.shape[0] // LANES, permute, 0)


def _linear_tile(a, wbf, b_ref):
    return jnp.dot(a, wbf[...], preferred_element_type=F32) + b_ref[0]


def _down(meta, act, w_down, b_down, tn=1024):
    return _moe_call("down", meta, act, w_down, b_down, w_down.shape[2], F32, tn, tn,
                     _permute_weights, _linear_tile, (tn // LANES, LANES, LANES))


def _combine_body(dest_ref, y_ref, x_ref, gate_ref, g_ref, o_ref, ybuf, sem, *, n_tokens):
    tm = x_ref.shape[0]
    i = pl.program_id(0)

    def row_copy(t, k):
        d = dest_ref[k * n_tokens + i * tm + t]
        return pltpu.make_async_copy(y_ref.at[pl.ds(d, 1), :], ybuf.at[k, pl.ds(t, 1), :], sem)

    def issue(t, c):
        for k in range(TOP_K):
            row_copy(t, k).start(priority=k % 2)
        return c

    lax.fori_loop(0, tm, issue, 0)

    def drain(t, c):
        for k in range(TOP_K):
            row_copy(t, k).wait()
        return c

    lax.fori_loop(0, tm, drain, 0)

    acc = x_ref[...]
    for k in range(TOP_K):
        acc = acc + gate_ref[:, k:k + 1] * ybuf[k]
    o_ref[...] = _rms(acc, g_ref[...])


def _combine(dest_flat, y_buf, x2, gates, g, tm=256):
    n, d = x2.shape
    return pl.pallas_call(
        functools.partial(_combine_body, n_tokens=n),
        grid_spec=pltpu.PrefetchScalarGridSpec(
            num_scalar_prefetch=1,
            grid=(n // tm,),
            in_specs=[
                pl.BlockSpec(memory_space=pl.ANY),
                pl.BlockSpec((tm, d), lambda i, *_: (i, 0)),
                pl.BlockSpec((tm, TOP_K), lambda i, *_: (i, 0)),
                pl.BlockSpec((1, d), lambda i, *_: (0, 0)),
            ],
            out_specs=pl.BlockSpec((tm, d), lambda i, *_: (i, 0)),
            scratch_shapes=[pltpu.VMEM((TOP_K, tm, d), F32), pltpu.SemaphoreType.DMA(())],
        ),
        out_shape=jax.ShapeDtypeStruct((n, d), F32),
        compiler_params=_params(("arbitrary",)),
        name="combine",
    )(dest_flat, y_buf, x2, gates, g.reshape(1, d))


def _cumsum_sublanes(x):
    row = lax.broadcasted_iota(jnp.int32, x.shape, 0)
    s = 1
    while s < x.shape[0]:
        x = x + jnp.where(row >= s, pltpu.roll(x, s, axis=0), 0)
        s *= 2
    return x


def _tables_body(cnt_ref, idx_ref, rank_ref, dest_ref, meta_ref, pad_ref, *, n_rows):
    cnt = cnt_ref[...]
    padded = (cnt + (MOE_BLOCK - 1)) // MOE_BLOCK * MOE_BLOCK
    pend = _cumsum_sublanes(padded)
    pstart = pend - padded
    total = pend[N_EXPERTS - 1:N_EXPERTS, 0:1]

    def lookup(table, sel):
        eid = lax.broadcasted_iota(jnp.int32, (N_EXPERTS, sel.shape[1]), 0)
        return jnp.sum(jnp.where(eid == sel, table[:, 0:1], 0), axis=0, keepdims=True)

    def count_le(table, v):
        return jnp.sum((table[:, 0:1] <= v).astype(jnp.int32), axis=0, keepdims=True)

    width = 2048
    for k in range(TOP_K):
        for c in range(idx_ref.shape[1] // width):
            sl = slice(c * width, (c + 1) * width)
            dest_ref[k:k + 1, sl] = lookup(pstart, idx_ref[k:k + 1, sl]) + rank_ref[k:k + 1, sl]

    n_it = (padded + (ITEM_ROWS - 1)) // ITEM_ROWS
    it_end = _cumsum_sublanes(n_it)
    t = lax.broadcasted_iota(jnp.int32, (1, META_STRIDE), 1)
    e = count_le(it_end, t)
    ec = jnp.minimum(e, N_EXPERTS - 1)
    k = t - lookup(it_end - n_it, ec)
    rows_real = jnp.minimum(lookup(padded, ec) - k * ITEM_ROWS, ITEM_ROWS)
    row0_zero = total + (t - it_end[N_EXPERTS - 1:N_EXPERTS, 0:1]) * ITEM_ROWS
    rows_zero = jnp.clip(n_rows - row0_zero, 0, ITEM_ROWS)
    real = e < N_EXPERTS
    meta_ref[:, 0:META_STRIDE] = ec
    meta_ref[:, META_STRIDE:2 * META_STRIDE] = jnp.where(
        real, lookup(pstart, ec) + k * ITEM_ROWS, jnp.minimum(row0_zero, n_rows - MOE_BLOCK))
    meta_ref[:, 2 * META_STRIDE:3 * META_STRIDE] = jnp.where(real, rows_real, rows_zero) // MOE_BLOCK
    meta_ref[:, 3 * META_STRIDE:4 * META_STRIDE] = jnp.where(real, 0, 1)

    gap = padded - cnt
    gap_end = _cumsum_sublanes(gap)
    q = lax.broadcasted_iota(jnp.int32, pad_ref.shape, 1)
    eq = count_le(gap_end, q)
    inside = lookup(pstart + cnt - (gap_end - gap), eq) + q
    tail = total + q - gap_end[N_EXPERTS - 1:N_EXPERTS, 0:1]
    pad_ref[...] = jnp.where(eq < N_EXPERTS, inside, tail)


def _max_items(n_rows):
    return N_EXPERTS + n_rows // ITEM_ROWS + (N_EXPERTS * MOE_BLOCK) // ITEM_ROWS + 1


def _tables(cnt, idx_t, rank_t, n_rows):
    n_pad = n_rows - idx_t.size
    assert _max_items(n_rows) <= META_STRIDE
    dest, meta, pad_dest = pl.pallas_call(
        functools.partial(_tables_body, n_rows=n_rows),
        out_shape=[
            jax.ShapeDtypeStruct(idx_t.shape, jnp.int32),
            jax.ShapeDtypeStruct((1, 4 * META_STRIDE), jnp.int32),
            jax.ShapeDtypeStruct((1, n_pad), jnp.int32),
        ],
        name="tables",
    )(cnt, idx_t, rank_t)
    return dest.reshape(-1), meta.reshape(-1), pad_dest.reshape(-1)


def kernel(x, norm_mix, w_in, conv_w, conv_b, w_a, b_a, w_x, b_x, lru_lambda, attn_out_norm, lru_out_norm, w_out, norm_ffn, w_router, b_router, w_gate_up, b_gate_up, w_down, b_down, norm_final):
    b, seq, d = x.shape
    n = b * seq
    x2 = x.reshape(n, d)

    proj = _in_proj(x2, norm_mix, w_in)
    proj3 = proj.reshape(b, seq, IN_COLS)

    slopes = jnp.asarray(2.0 ** (-8.0 * np.arange(1, N_HEADS + 1) / N_HEADS), F32)
    attn = _attention(proj3, slopes)

    def pair_blocks(w):
        w4 = w.reshape(-1, 2, LRU_BLOCK, LRU_BLOCK)
        z = jnp.zeros_like(w4[:, 0])
        top = jnp.concatenate([w4[:, 0], z], axis=2)
        bot = jnp.concatenate([z, w4[:, 1]], axis=2)
        return jnp.concatenate([top, bot], axis=1)

    w_bd = jnp.concatenate([pair_blocks(w_a), pair_blocks(w_x)], axis=2)
    rec = _rglru(proj3, conv_w, conv_b, w_bd, b_a, b_x, lru_lambda)

    x_mid = _out_proj(attn.reshape(n, ATTN_WIDTH), rec.reshape(n, LRU_WIDTH),
                      attn_out_norm, lru_out_norm, x2, w_out)

    xn, idx_t, gate_t, rank_t, cnt = _router(x_mid, norm_ffn, w_router.T, b_router)
    n_rows = n * TOP_K + N_EXPERTS * MOE_BLOCK
    dest_flat, meta, pad_dest = _tables(cnt, idx_t, rank_t, n_rows)

    x_sorted = _scatter(dest_flat, pad_dest, xn, n_rows)
    act = _gate_up(meta, x_sorted, w_gate_up, b_gate_up)
    y_buf = _down(meta, act, w_down, b_down)
    out = _combine(dest_flat, y_buf, x_mid, gate_t.T, norm_final)
    return out.reshape(b, seq, d)
```

```python
import functools

import numpy as np
import jax
import jax.numpy as jnp
from jax import lax
from jax.experimental import pallas as pl
from jax.experimental.pallas import tpu as pltpu

F32 = jnp.float32
BF16 = jnp.bfloat16
U32 = jnp.uint32
HIGH_HALF = np.uint32(0xFFFF0000)

D_MODEL = 2048
HEAD_DIM = 64
N_HEADS = 16
ATTN_WIDTH = N_HEADS * HEAD_DIM
LRU_WIDTH = D_MODEL - ATTN_WIDTH
LRU_BLOCK = 64
CONV_WIDTH = 4
LRU_C = 8.0
IN_COLS = 3 * ATTN_WIDTH + 2 * LRU_WIDTH
DILATIONS = (1, 4, 16)
ATTN_BLOCK = 128
UNITS_PER_TRIP = (5, 6, 8)
N_EXPERTS = 32
TOP_K = 4
D_FF = D_MODEL
SWIGLU_LIMIT = 7.0
SWIGLU_ALPHA = 1.702
MOE_BLOCK = 256
ITEM_ROWS = 8 * MOE_BLOCK
META_STRIDE = 128
TAIL_SLOT = 2
EPS = 1e-6

LANES = 128
SUBLANES = 8
VMEM_LIMIT = 56 * 1024 * 1024


def _params(sem, vmem=VMEM_LIMIT):
    return pltpu.CompilerParams(dimension_semantics=sem, vmem_limit_bytes=vmem)


def _rms(xf, g):
    return xf * lax.rsqrt(jnp.mean(xf * xf, axis=-1, keepdims=True) + EPS) * g


def _pack_halves(x):
    half = x.shape[1] // 2
    lo = lax.bitcast_convert_type(x[:, :half].astype(BF16).astype(F32), U32) >> 16
    hi = lax.bitcast_convert_type(x[:, half:].astype(BF16).astype(F32), U32) & HIGH_HALF
    return lo | hi


def _unpack_halves(u):
    return (lax.bitcast_convert_type(u << 16, F32),
            lax.bitcast_convert_type(u & HIGH_HALF, F32))


def _in_proj_body(x_ref, g_ref, w_ref, o_ref, h_ref):
    @pl.when(pl.program_id(1) == 0)
    def _():
        h_ref[...] = _rms(x_ref[...], g_ref[...]).astype(BF16)

    o_ref[...] = jnp.dot(h_ref[...], w_ref[...].astype(BF16), preferred_element_type=F32)


def _in_proj(x2, g, w_in, tm=1024, tn=512):
    n, d = x2.shape
    cols = w_in.shape[1]
    return pl.pallas_call(
        _in_proj_body,
        grid=(n // tm, cols // tn),
        in_specs=[
            pl.BlockSpec((tm, d), lambda i, j: (i, 0)),
            pl.BlockSpec((1, d), lambda i, j: (0, 0)),
            pl.BlockSpec((d, tn), lambda i, j: (0, j)),
        ],
        out_specs=pl.BlockSpec((tm, tn), lambda i, j: (i, j)),
        out_shape=jax.ShapeDtypeStruct((n, cols), F32),
        scratch_shapes=[pltpu.VMEM((tm, d), BF16)],
        compiler_params=_params(("parallel", "arbitrary")),
        name="in_proj",
    )(x2, g.reshape(1, d), w_in)


REGROUP = 4


def _attention_body(slope_ref, q_ref, k_ref, v_ref, o_ref, qh, kh, vh, acc, mx):
    seq = q_ref.shape[1]
    run = seq // REGROUP
    pair = pl.program_id(1)
    qi = lax.broadcasted_iota(jnp.int32, (ATTN_BLOCK, 2 * ATTN_BLOCK), 0)
    ki = lax.broadcasted_iota(jnp.int32, (ATTN_BLOCK, 2 * ATTN_BLOCK), 1)
    rel2 = qi + ATTN_BLOCK - ki
    ok2 = (rel2 >= 0) & (rel2 <= ATTN_BLOCK)
    rel1 = (qi - ki)[:, :ATTN_BLOCK]
    ok1 = rel1 >= 0

    nh = LANES // HEAD_DIM
    slopes = [slope_ref[pair * nh + hh] for hh in range(nh)]
    def stage(order, rows_dst, rows_src):
        lane = lax.broadcasted_iota(jnp.int32, (run, LANES), 1)
        qv, vv = q_ref[0, rows_src, :] * (HEAD_DIM ** -0.5), v_ref[0, rows_src, :]
        if order:
            kh[rows_dst, :] = k_ref[0, rows_src, :]
        for hh in range(nh):
            mine = (lane >= hh * HEAD_DIM) & (lane < (hh + 1) * HEAD_DIM)
            qh[order, hh, rows_dst, :] = jnp.where(mine, qv, 0.0)
            vh[order, hh, rows_dst, :] = jnp.where(mine, vv, 1.0)

    for c in range(REGROUP):
        stage(0, slice(c * run, (c + 1) * run), pl.ds(c * run, run))
        stage(1, slice(c * run, (c + 1) * run), pl.ds(c, run, stride=REGROUP))

    def run_units(units):
        loaded = [(qh[order, hh, qs, :].astype(BF16),
                   (kh[ks, :] if order else k_ref[0, ks, :]).astype(BF16),
                   vh[order, hh, ks, :].astype(BF16)) for hh, br, order, qs, ks, bias in units]
        scores = [lax.dot_general(q, k, (((1,), (1,)), ((), ())), preferred_element_type=F32)
                  + u[5] for (q, k, v), u in zip(loaded, units)]
        maxes = [jnp.max(s, axis=1, keepdims=True) for s in scores]
        probs = [jnp.exp(s - m).astype(BF16) for s, m in zip(scores, maxes)]
        results = [(jnp.dot(p, v, preferred_element_type=F32), m)
                   for p, (q, k, v), m in zip(probs, loaded, maxes)]
        for (a, m), (hh, br, order, qs, ks, bias) in zip(results, units):
            acc[hh, br, qs, :] = a
            mx[hh, br, qs, :] = jnp.broadcast_to(m, (ATTN_BLOCK, LANES))

    def bias_pair(ok, rel, d):
        return [jnp.where(ok, -sl * (rel * d).astype(F32), -jnp.inf) for sl in slopes]

    def contiguous_branch(br, order, d, n_runs, per_trip):
        nb = seq // (n_runs * ATTN_BLOCK)
        bias1, bias2 = bias_pair(ok1, rel1, d), bias_pair(ok2, rel2, d)
        run_units([(hh, br, order, pl.ds(r * nb * ATTN_BLOCK, ATTN_BLOCK),
                    pl.ds(r * nb * ATTN_BLOCK, ATTN_BLOCK), bias1[hh])
                   for r in range(n_runs) for hh in range(nh)])

        def later(g, carry):
            units = []
            for u in range(per_trip):
                idx = g * per_trip + u
                blk = idx // (nb - 1) * nb + idx % (nb - 1) + 1
                start = pl.multiple_of(blk * ATTN_BLOCK, ATTN_BLOCK)
                units += [(hh, br, order, pl.ds(start, ATTN_BLOCK),
                           pl.ds(start - ATTN_BLOCK, 2 * ATTN_BLOCK), bias2[hh])
                          for hh in range(nh)]
            run_units(units)
            return carry

        lax.fori_loop(0, n_runs * (nb - 1) // per_trip, later, 0)

    contiguous_branch(0, 0, DILATIONS[0], 1, UNITS_PER_TRIP[0])
    contiguous_branch(1, 1, DILATIONS[1], REGROUP, UNITS_PER_TRIP[1])

    d2 = DILATIONS[2]
    sub = d2 // REGROUP
    bias16 = bias_pair(ok1, rel1, d2)

    def strided(g, carry):
        units = []
        for u in range(UNITS_PER_TRIP[2]):
            idx = g * UNITS_PER_TRIP[2] + u
            rows = pl.ds(idx // sub * run + idx % sub, ATTN_BLOCK, stride=sub)
            units += [(hh, 2, 1, rows, rows, bias16[hh]) for hh in range(nh)]
        run_units(units)
        return carry

    lax.fori_loop(0, d2 // UNITS_PER_TRIP[2], strided, 0)

    def merge(c, carry):
        start = c // (run // ATTN_BLOCK) + c % (run // ATTN_BLOCK) * (ATTN_BLOCK * REGROUP)
        nat = pl.ds(start, ATTN_BLOCK, stride=REGROUP)
        reg = pl.ds(pl.multiple_of(c * ATTN_BLOCK, ATTN_BLOCK), ATTN_BLOCK)
        outs = []
        for hh in range(nh):
            m0, m1, m2 = mx[hh, 0, nat, :], mx[hh, 1, reg, :], mx[hh, 2, reg, :]
            mt = jnp.maximum(jnp.maximum(m0, m1), m2)
            tot = (jnp.exp(m0 - mt) * acc[hh, 0, nat, :] + jnp.exp(m1 - mt) * acc[hh, 1, reg, :]
                   + jnp.exp(m2 - mt) * acc[hh, 2, reg, :])
            den = pltpu.roll(tot, HEAD_DIM, axis=1)
            outs.append(tot / den)
        lane_r = lax.broadcasted_iota(jnp.int32, (ATTN_BLOCK, LANES), 1)
        o_ref[0, nat, :] = jnp.where(lane_r < HEAD_DIM, outs[0], outs[1])
        return carry

    lax.fori_loop(0, seq // ATTN_BLOCK, merge, 0, unroll=2)


def _attention(proj3, slopes):
    b, seq, _ = proj3.shape
    npair = ATTN_WIDTH // LANES
    nh = LANES // HEAD_DIM
    blk = (1, seq, LANES)
    return pl.pallas_call(
        _attention_body,
        grid_spec=pltpu.PrefetchScalarGridSpec(
            num_scalar_prefetch=1,
            grid=(b, npair),
            in_specs=[
                pl.BlockSpec(blk, lambda i, j, s: (i, 0, j)),
                pl.BlockSpec(blk, lambda i, j, s: (i, 0, npair + j)),
                pl.BlockSpec(blk, lambda i, j, s: (i, 0, 2 * npair + j)),
            ],
            out_specs=pl.BlockSpec(blk, lambda i, j, s: (i, 0, j)),
            scratch_shapes=[
                pltpu.VMEM((2, nh, seq, LANES), F32),
                pltpu.VMEM((seq, LANES), F32),
                pltpu.VMEM((2, nh, seq, LANES), F32),
                pltpu.VMEM((nh, len(DILATIONS), seq, LANES), F32),
                pltpu.VMEM((nh, len(DILATIONS), seq, LANES), F32),
            ],
        ),
        out_shape=jax.ShapeDtypeStruct((b, seq, ATTN_WIDTH), F32),
        compiler_params=_params(("parallel", "parallel")),
        name="attention",
    )(slopes, proj3, proj3, proj3)


def _rglru_body(xr_ref, gr_ref, cw_ref, cb_ref, w_ref, ba_ref, bx_ref, lam_ref, o_ref,
                xp, a_s, b_s):
    seq = xr_ref.shape[1]
    pad = SUBLANES
    xp[0:pad, :] = jnp.zeros((pad, LANES), F32)
    xp[pad:pad + seq, :] = xr_ref[0]
    lam = lam_ref[...]
    sp = jnp.maximum(-lam, 0.0) + jnp.log(1.0 + jnp.exp(-jnp.abs(lam)))
    w_hi = w_ref[0].astype(BF16)
    w_lo = (w_ref[0] - w_hi.astype(F32)).astype(BF16)
    rows = 256

    def gates(c, carry):
        base = pl.multiple_of(c * rows, rows)
        xc = cb_ref[...] + cw_ref[0:1, :] * xp[pl.ds(base + pad - 3, rows), :]
        for i in range(1, CONV_WIDTH):
            xc = xc + cw_ref[i:i + 1, :] * xp[pl.ds(base + pad - 3 + i, rows), :]
        hi = xc.astype(BF16)
        lo = (xc - hi.astype(F32)).astype(BF16)
        pre = (jnp.dot(hi, w_hi, preferred_element_type=F32)
               + jnp.dot(lo, w_hi, preferred_element_type=F32)
               + jnp.dot(hi, w_lo, preferred_element_type=F32))
        r = jax.nn.sigmoid(pre[:, :LANES] + ba_ref[...])
        ig = jax.nn.sigmoid(pre[:, LANES:] + bx_ref[...])
        log_a = -LRU_C * r * sp
        a = jnp.exp(log_a)
        t = jnp.tanh(log_a)
        b = jnp.sqrt(-2.0 * t / (1.0 - t)) * (ig * xc)
        a_s[pl.ds(base, rows), :] = a
        b_s[pl.ds(base, rows), :] = b
        return carry

    lax.fori_loop(0, seq // rows, gates, 0)

    row = lax.broadcasted_iota(jnp.int32, (SUBLANES, LANES), 0)

    def scan(c, h_prev):
        sl = pl.ds(pl.multiple_of(c * SUBLANES, SUBLANES), SUBLANES)
        a = a_s[sl, :]
        b = b_s[sl, :]
        for s in (1, 2, 4):
            keep = row >= s
            a_sh = jnp.where(keep, pltpu.roll(a, s, axis=0), 1.0)
            b_sh = jnp.where(keep, pltpu.roll(b, s, axis=0), 0.0)
            b = a * b_sh + b
            a = a * a_sh
        h = a * h_prev + b
        o_ref[0, sl, :] = h * jax.nn.gelu(gr_ref[0, sl, :])
        return jnp.broadcast_to(h[SUBLANES - 1:SUBLANES, :], (SUBLANES, LANES))

    lax.fori_loop(0, seq // SUBLANES, scan, jnp.zeros((SUBLANES, LANES), F32), unroll=8)


def _rglru(proj3, conv_w, conv_b, w_bd, b_a, b_x, lam):
    b, seq, _ = proj3.shape
    nt = LRU_WIDTH // LANES
    xr0 = 3 * ATTN_WIDTH // LANES
    gr0 = xr0 + nt
    blk = (1, seq, LANES)
    vec = lambda: pl.BlockSpec((1, LANES), lambda i, j: (0, j))
    return pl.pallas_call(
        _rglru_body,
        grid=(b, nt),
        in_specs=[
            pl.BlockSpec(blk, lambda i, j: (i, 0, xr0 + j)),
            pl.BlockSpec(blk, lambda i, j: (i, 0, gr0 + j)),
            pl.BlockSpec((CONV_WIDTH, LANES), lambda i, j: (0, j)),
            vec(),
            pl.BlockSpec((1, LANES, 2 * LANES), lambda i, j: (j, 0, 0)),
            vec(), vec(), vec(),
        ],
        out_specs=pl.BlockSpec(blk, lambda i, j: (i, 0, j)),
        out_shape=jax.ShapeDtypeStruct((b, seq, LRU_WIDTH), F32),
        scratch_shapes=[
            pltpu.VMEM((seq + SUBLANES, LANES), F32),
            pltpu.VMEM((seq, LANES), F32),
            pltpu.VMEM((seq, LANES), F32),
        ],
        compiler_params=_params(("parallel", "parallel")),
        name="rglru",
    )(proj3, proj3, conv_w, conv_b.reshape(1, -1), w_bd, b_a.reshape(1, -1),
      b_x.reshape(1, -1), lam.reshape(1, -1))


def _out_proj_body(at_ref, rc_ref, ga_ref, gr_ref, x_ref, w_ref, o_ref, h_ref):
    wa = at_ref.shape[1]

    @pl.when(pl.program_id(1) == 0)
    def _():
        h_ref[:, :wa] = _rms(at_ref[...], ga_ref[...]).astype(BF16)
        h_ref[:, wa:] = _rms(rc_ref[...], gr_ref[...]).astype(BF16)

    o_ref[...] = x_ref[...] + jnp.dot(h_ref[...], w_ref[...].astype(BF16),
                                      preferred_element_type=F32)


def _out_proj(attn2, rec2, g_attn, g_rec, x2, w_out, tm=512, tn=1024):
    n, d = x2.shape
    wa, wr = attn2.shape[1], rec2.shape[1]
    return pl.pallas_call(
        _out_proj_body,
        grid=(n // tm, d // tn),
        in_specs=[
            pl.BlockSpec((tm, wa), lambda i, j: (i, 0)),
            pl.BlockSpec((tm, wr), lambda i, j: (i, 0)),
            pl.BlockSpec((1, wa), lambda i, j: (0, 0)),
            pl.BlockSpec((1, wr), lambda i, j: (0, 0)),
            pl.BlockSpec((tm, tn), lambda i, j: (i, j)),
            pl.BlockSpec((d, tn), lambda i, j: (0, j)),
        ],
        out_specs=pl.BlockSpec((tm, tn), lambda i, j: (i, j)),
        out_shape=jax.ShapeDtypeStruct((n, d), F32),
        scratch_shapes=[pltpu.VMEM((tm, d), BF16)],
        compiler_params=_params(("parallel", "arbitrary")),
        name="out_proj",
    )(attn2, rec2, g_attn.reshape(1, wa), g_rec.reshape(1, wr), x2, w_out)


def _router_body(x_ref, g_ref, wt_ref, b_ref, xn_ref, idx_ref, gate_ref, rank_ref, cnt_ref,
                 base):
    tm = x_ref.shape[0]

    @pl.when(pl.program_id(0) == 0)
    def _():
        base[...] = jnp.zeros_like(base)

    xn = _rms(x_ref[...], g_ref[...])
    xn_ref[...] = _pack_halves(xn)
    x_hi = xn.astype(BF16)
    x_lo = (xn - x_hi.astype(F32)).astype(BF16)
    w = wt_ref[...]
    w_hi = w.astype(BF16)
    w_lo = (w - w_hi.astype(F32)).astype(BF16)
    nt = (((1,), (1,)), ((), ()))
    logits = (lax.dot_general(w_hi, x_hi, nt, preferred_element_type=F32)
              + lax.dot_general(w_hi, x_lo, nt, preferred_element_type=F32)
              + lax.dot_general(w_lo, x_hi, nt, preferred_element_type=F32)
              + b_ref[...])

    eid = lax.broadcasted_iota(jnp.int32, (N_EXPERTS, tm), 0)
    work = logits
    vals, hots = [], []
    for _ in range(TOP_K):
        best = jnp.max(work, axis=0, keepdims=True)
        pick = jnp.min(jnp.where(work == best, eid, N_EXPERTS), axis=0, keepdims=True)
        hot = eid == pick
        vals.append(best)
        hots.append(hot)
        work = jnp.where(hot, -jnp.inf, work)
        idx_ref[len(vals) - 1:len(vals), :] = pick

    ex = [jnp.exp(v - vals[0]) for v in vals]
    den = ex[0] + ex[1] + ex[2] + ex[3]
    for k in range(TOP_K):
        gate_ref[k:k + 1, :] = ex[k] / den

    chosen = (hots[0] | hots[1] | hots[2] | hots[3])
    si = lax.broadcasted_iota(jnp.int32, (tm, tm), 0)
    ti = lax.broadcasted_iota(jnp.int32, (tm, tm), 1)
    before = (si < ti).astype(BF16)
    prefix = jnp.dot(chosen.astype(BF16), before, preferred_element_type=F32)
    slot = base[:, 0:1] + prefix
    for k in range(TOP_K):
        rank_ref[k:k + 1, :] = jnp.sum(jnp.where(hots[k], slot, 0.0), axis=0,
                                       keepdims=True).astype(jnp.int32)
    base[...] = base[...] + jnp.sum(chosen.astype(F32), axis=1, keepdims=True)
    cnt_ref[...] = base[...].astype(jnp.int32)


def _router(x2, g, w_router_t, b_router, tm=256):
    n, d = x2.shape
    row = lambda: pl.BlockSpec((TOP_K, tm), lambda i: (0, i))
    return pl.pallas_call(
        _router_body,
        grid=(n // tm,),
        in_specs=[
            pl.BlockSpec((tm, d), lambda i: (i, 0)),
            pl.BlockSpec((1, d), lambda i: (0, 0)),
            pl.BlockSpec((N_EXPERTS, d), lambda i: (0, 0)),
            pl.BlockSpec((N_EXPERTS, 1), lambda i: (0, 0)),
        ],
        out_specs=[
            pl.BlockSpec((tm, d // 2), lambda i: (i, 0)),
            row(), row(), row(),
            pl.BlockSpec((N_EXPERTS, LANES), lambda i: (0, 0)),
        ],
        out_shape=[
            jax.ShapeDtypeStruct((n, d // 2), U32),
            jax.ShapeDtypeStruct((TOP_K, n), jnp.int32),
            jax.ShapeDtypeStruct((TOP_K, n), F32),
            jax.ShapeDtypeStruct((TOP_K, n), jnp.int32),
            jax.ShapeDtypeStruct((N_EXPERTS, LANES), jnp.int32),
        ],
        scratch_shapes=[pltpu.VMEM((N_EXPERTS, LANES), F32)],
        compiler_params=_params(("arbitrary",)),
        name="router",
    )(x2, g.reshape(1, d), w_router_t, b_router.reshape(N_EXPERTS, 1))


def _scatter_body(dest_ref, pad_ref, xn_ref, o_ref, zrow, sem, *, n_tokens, pad_per_step):
    tm = xn_ref.shape[0]
    i = pl.program_id(0)

    @pl.when(i == 0)
    def _():
        zrow[...] = jnp.zeros_like(zrow)

    def row_copy(t, k):
        d = dest_ref[k * n_tokens + i * tm + t]
        return pltpu.make_async_copy(xn_ref.at[pl.ds(t, 1), :], o_ref.at[pl.ds(d, 1), :], sem)

    def pad_copy(q):
        d = pad_ref[i * pad_per_step + q]
        return pltpu.make_async_copy(zrow.at[pl.ds(0, 1), :], o_ref.at[pl.ds(d, 1), :], sem)

    def issue(t, c):
        for k in range(TOP_K):
            row_copy(t, k).start(priority=k % 2)
        return c

    lax.fori_loop(0, tm, issue, 0)

    def issue_pad(q, c):
        pad_copy(q).start()
        return c

    lax.fori_loop(0, pad_per_step, issue_pad, 0)

    def drain(t, c):
        for k in range(TOP_K):
            row_copy(t, k).wait()
        return c

    lax.fori_loop(0, tm, drain, 0)

    def drain_pad(q, c):
        pad_copy(q).wait()
        return c

    lax.fori_loop(0, pad_per_step, drain_pad, 0)


def _scatter(dest_flat, pad_dest, xn, n_rows, tm=256):
    n, d = xn.shape
    steps = n // tm
    pad_per_step = pad_dest.shape[0] // steps
    return pl.pallas_call(
        functools.partial(_scatter_body, n_tokens=n, pad_per_step=pad_per_step),
        grid_spec=pltpu.PrefetchScalarGridSpec(
            num_scalar_prefetch=2,
            grid=(steps,),
            in_specs=[pl.BlockSpec((tm, d), lambda i, *_: (i, 0))],
            out_specs=pl.BlockSpec(memory_space=pl.ANY),
            scratch_shapes=[pltpu.VMEM((SUBLANES, d), xn.dtype), pltpu.SemaphoreType.DMA(())],
        ),
        out_shape=jax.ShapeDtypeStruct((n_rows, d), xn.dtype),
        compiler_params=_params(("arbitrary",)),
        name="scatter",
    )(dest_flat, pad_dest, xn)


def _moe_body(meta_ref, x_hbm, w_ref, b_ref, o_hbm, xbf, stage, wbf, obuf, sem_in, sem_out,
              pend, *, prep_w, epilogue):
    t, j = pl.program_id(0), pl.program_id(1)
    row0 = meta_ref[META_STRIDE + t]
    nsub = meta_ref[2 * META_STRIDE + t]
    zero = meta_ref[3 * META_STRIDE + t]
    tn_out = obuf.shape[2]
    col0 = pl.multiple_of(j * tn_out, tn_out)

    @pl.when((t == 0) & (j == 0))
    def _():
        for k in range(TAIL_SLOT + 1):
            pend[k] = 0

    def in_copy(c, slot):
        src = x_hbm.at[pl.ds(pl.multiple_of(row0 + c * MOE_BLOCK, MOE_BLOCK), MOE_BLOCK), :]
        if stage is None:
            dst = xbf.at[pl.ds(pl.multiple_of(c * MOE_BLOCK, MOE_BLOCK), MOE_BLOCK), :]
        else:
            dst = stage.at[slot]
        return pltpu.make_async_copy(src, dst, sem_in.at[slot])

    def fetch_rows(c):
        slot = c % 2

        @pl.when(c + 1 < nsub)
        def _():
            in_copy(c + 1, 1 - slot).start()

        in_copy(c, slot).wait()
        if stage is not None:
            rows = pl.ds(pl.multiple_of(c * MOE_BLOCK, MOE_BLOCK), MOE_BLOCK)
            lo, hi = _unpack_halves(stage[slot])
            xbf[rows, :lo.shape[1]] = lo.astype(BF16)
            xbf[rows, lo.shape[1]:] = hi.astype(BF16)

    def out_copy(first_block, slot, tail):
        n_rows = MOE_BLOCK if tail else 2 * MOE_BLOCK
        rows = pl.ds(pl.multiple_of(row0 + first_block * MOE_BLOCK, MOE_BLOCK), n_rows)
        return pltpu.make_async_copy(obuf.at[slot, pl.ds(0, n_rows), :],
                                     o_hbm.at[rows, pl.ds(col0, tn_out)], sem_out.at[slot])

    def emit(first_block, slot, tail, value_fn):
        @pl.when(pend[slot] == 1)
        def _():
            out_copy(first_block, slot, tail).wait()

        obuf[slot, pl.ds(0, MOE_BLOCK if tail else 2 * MOE_BLOCK), :] = value_fn()
        out_copy(first_block, slot, tail).start()
        pend[slot] = 1

    def walk(block_fn):
        def pair(p, carry):
            block_fn(2 * p, p % 2, False)
            return carry

        lax.fori_loop(0, nsub // 2, pair, 0)

        @pl.when(nsub % 2 == 1)
        def _():
            block_fn(nsub - 1, TAIL_SLOT, True)

    @pl.when((nsub > 0) & (zero == 0))
    def _compute():
        @pl.when(j == 0)
        def _():
            in_copy(0, 0).start()

        prep_w(w_ref, wbf)

        def block(first_block, slot, tail):
            n_blocks = 1 if tail else 2

            @pl.when(j == 0)
            def _():
                for k in range(n_blocks):
                    fetch_rows(first_block + k)

            rows = pl.ds(pl.multiple_of(first_block * MOE_BLOCK, MOE_BLOCK), n_blocks * MOE_BLOCK)
            emit(first_block, slot, tail, lambda: epilogue(xbf[rows, :], wbf, b_ref))

        walk(block)

    @pl.when((nsub > 0) & (zero == 1))
    def _zeros():
        walk(lambda first_block, slot, tail: emit(
            first_block, slot, tail,
            lambda: jnp.zeros(((1 if tail else 2) * MOE_BLOCK, tn_out), obuf.dtype)))

    @pl.when((t == pl.num_programs(0) - 1) & (j == pl.num_programs(1) - 1))
    def _drain():
        for k in range(TAIL_SLOT + 1):
            @pl.when(pend[k] == 1)
            def _():
                out_copy(0, k, k == TAIL_SLOT).wait()


def _moe_call(name, meta, x, w, b, out_cols, out_dtype, tn, tn_out, prep_w, epilogue, tmp_shape):
    p, kdim = x.shape[0], w.shape[1]
    n_items = _max_items(p)
    n_tiles = w.shape[2] // tn
    needs_stage = x.dtype != BF16

    def wmap(t, j, m):
        live = (m[2 * META_STRIDE + t] > 0) & (m[3 * META_STRIDE + t] == 0)
        return (m[t], 0, jnp.where(live, j, n_tiles - 1))

    scratch = [pltpu.VMEM((ITEM_ROWS, kdim), BF16)]
    if needs_stage:
        scratch.append(pltpu.VMEM((2, MOE_BLOCK, x.shape[1]), x.dtype))
    scratch += [
        pltpu.VMEM((kdim, tn), BF16),
        pltpu.VMEM((TAIL_SLOT + 1, 2 * MOE_BLOCK, tn_out), out_dtype),
        pltpu.SemaphoreType.DMA((2,)),
        pltpu.SemaphoreType.DMA((TAIL_SLOT + 1,)),
        pltpu.SMEM((TAIL_SLOT + 1,), jnp.int32),
    ]
    if tmp_shape is not None:
        scratch.append(pltpu.VMEM(tmp_shape, F32))

    def body(meta_ref, x_hbm, w_ref, b_ref, o_hbm, xbf, *rest):
        rest = list(rest)
        stage = rest.pop(0) if needs_stage else None
        wbf, obuf, sem_in, sem_out, pend = rest[:5]
        tmp = rest[5] if tmp_shape is not None else None
        _moe_body(meta_ref, x_hbm, w_ref, b_ref, o_hbm, xbf, stage, wbf, obuf, sem_in, sem_out,
                  pend, prep_w=functools.partial(prep_w, tmp=tmp), epilogue=epilogue)

    return pl.pallas_call(
        body,
        grid_spec=pltpu.PrefetchScalarGridSpec(
            num_scalar_prefetch=1,
            grid=(n_items, n_tiles),
            in_specs=[
                pl.BlockSpec(memory_space=pl.ANY),
                pl.BlockSpec((1, kdim, tn), wmap),
                pl.BlockSpec((1, 1, tn), wmap),
            ],
            out_specs=pl.BlockSpec(memory_space=pl.ANY),
            scratch_shapes=scratch,
        ),
        out_shape=jax.ShapeDtypeStruct((p, out_cols), out_dtype),
        compiler_params=_params(("arbitrary", "arbitrary")),
        name=name,
    )(meta, x, w, b.reshape(N_EXPERTS, 1, -1))


def _cast_weights(w_ref, wbf, tmp=None, rows=256):
    def cast(c, carry):
        sl = pl.ds(pl.multiple_of(c * rows, rows), rows)
        wbf[sl, :] = w_ref[0, sl, :].astype(BF16)
        return carry

    lax.fori_loop(0, wbf.shape[0] // rows, cast, 0)


def _swiglu_tile(x, wbf, b_ref):
    even = lax.broadcasted_iota(jnp.int32, (x.shape[0], LANES), 1) % 2 == 0
    cols = []
    for c in range(wbf.shape[1] // (2 * LANES)):
        c0 = c * 2 * LANES
        gu = jnp.dot(x, wbf[:, c0:c0 + 2 * LANES], preferred_element_type=F32)
        gu = gu + b_ref[0, :, c0:c0 + 2 * LANES]
        v1, v2 = gu[:, :LANES], gu[:, LANES:]
        gate = jnp.where(even, v1, pltpu.roll(v2, 1, axis=1))
        up = jnp.where(even, pltpu.roll(v1, LANES - 1, axis=1), v2)
        gate = jnp.minimum(gate, SWIGLU_LIMIT)
        up = jnp.clip(up, -SWIGLU_LIMIT, SWIGLU_LIMIT)
        cols.append((gate * jax.nn.sigmoid(SWIGLU_ALPHA * gate) * (up + 1.0)).astype(BF16))
    return jnp.concatenate(cols, axis=1)


def _gate_up(meta, x_sorted, w_gate_up, b_gate_up, tn=1024):
    return _moe_call("gate_up", meta, x_sorted, w_gate_up, b_gate_up, w_gate_up.shape[2] // 2,
                     BF16, tn, tn // 2, _cast_weights, _swiglu_tile, None)


def _permute_weights(w_ref, wbf, tmp):
    half = LANES // 2

    def permute(g, carry):
        base = pl.multiple_of(g * LANES, LANES)
        for c in range(tmp.shape[0]):
            cs = slice(c * LANES, (c + 1) * LANES)
            tmp[c, pl.ds(0, half, stride=2), :] = w_ref[0, pl.ds(base, half), cs]
            tmp[c, pl.ds(1, half, stride=2), :] = w_ref[0, pl.ds(base + half, half), cs]
            wbf[pl.ds(base, LANES), cs] = tmp[c].astype(BF16)
        return carry

    lax.fori_loop(0, wbf.shape[0] // LANES, permute, 0)


def _linear_tile(a, wbf, b_ref):
    return _pack_halves(jnp.dot(a, wbf[...], preferred_element_type=F32) + b_ref[0])


DOWN_TILE = 1024


def _down(meta, act, w_down, b_down, tn=DOWN_TILE):
    return _moe_call("down", meta, act, w_down, b_down, w_down.shape[2] // 2, U32, tn, tn // 2,
                     _permute_weights, _linear_tile, (tn // LANES, LANES, LANES))


def _combine_body(dest_ref, y_ref, x_ref, gate_ref, g_ref, o_ref, ybuf, sem, *, n_tokens,
                  y_tiles):
    tm = x_ref.shape[0]
    i = pl.program_id(0)

    def row_copy(t, k):
        d = dest_ref[k * n_tokens + i * tm + t]
        return pltpu.make_async_copy(y_ref.at[pl.ds(d, 1), :], ybuf.at[k, pl.ds(t, 1), :], sem)

    def issue(t, c):
        for k in range(TOP_K):
            row_copy(t, k).start(priority=k % 2)
        return c

    lax.fori_loop(0, tm, issue, 0)

    def drain(t, c):
        for k in range(TOP_K):
            row_copy(t, k).wait()
        return c

    lax.fori_loop(0, tm, drain, 0)

    acc = x_ref[...]
    tile = x_ref.shape[1] // y_tiles
    for k in range(TOP_K):
        parts = []
        for c in range(y_tiles):
            parts += _unpack_halves(ybuf[k, :, c * tile // 2:(c + 1) * tile // 2])
        acc = acc + gate_ref[:, k:k + 1] * jnp.concatenate(parts, axis=1)
    o_ref[...] = _rms(acc, g_ref[...])


def _combine(dest_flat, y_buf, x2, gates, g, y_tiles, tm=256):
    n, d = x2.shape
    return pl.pallas_call(
        functools.partial(_combine_body, n_tokens=n, y_tiles=y_tiles),
        grid_spec=pltpu.PrefetchScalarGridSpec(
            num_scalar_prefetch=1,
            grid=(n // tm,),
            in_specs=[
                pl.BlockSpec(memory_space=pl.ANY),
                pl.BlockSpec((tm, d), lambda i, *_: (i, 0)),
                pl.BlockSpec((tm, TOP_K), lambda i, *_: (i, 0)),
                pl.BlockSpec((1, d), lambda i, *_: (0, 0)),
            ],
            out_specs=pl.BlockSpec((tm, d), lambda i, *_: (i, 0)),
            scratch_shapes=[pltpu.VMEM((TOP_K, tm) + y_buf.shape[1:], y_buf.dtype),
                            pltpu.SemaphoreType.DMA(())],
        ),
        out_shape=jax.ShapeDtypeStruct((n, d), F32),
        compiler_params=_params(("arbitrary",)),
        name="combine",
    )(dest_flat, y_buf, x2, gates, g.reshape(1, d))


def _cumsum_sublanes(x):
    row = lax.broadcasted_iota(jnp.int32, x.shape, 0)
    s = 1
    while s < x.shape[0]:
        x = x + jnp.where(row >= s, pltpu.roll(x, s, axis=0), 0)
        s *= 2
    return x


def _tables_body(cnt_ref, idx_ref, rank_ref, dest_ref, meta_ref, pad_ref, *, n_rows):
    cnt = cnt_ref[...]
    padded = (cnt + (MOE_BLOCK - 1)) // MOE_BLOCK * MOE_BLOCK
    pend = _cumsum_sublanes(padded)
    pstart = pend - padded
    total = pend[N_EXPERTS - 1:N_EXPERTS, 0:1]

    def lookup(table, sel):
        eid = lax.broadcasted_iota(jnp.int32, (N_EXPERTS, sel.shape[1]), 0)
        return jnp.sum(jnp.where(eid == sel, table[:, 0:1], 0), axis=0, keepdims=True)

    def count_le(table, v):
        return jnp.sum((table[:, 0:1] <= v).astype(jnp.int32), axis=0, keepdims=True)

    width = 2048
    for k in range(TOP_K):
        for c in range(idx_ref.shape[1] // width):
            sl = slice(c * width, (c + 1) * width)
            dest_ref[k:k + 1, sl] = lookup(pstart, idx_ref[k:k + 1, sl]) + rank_ref[k:k + 1, sl]

    n_it = (padded + (ITEM_ROWS - 1)) // ITEM_ROWS
    it_end = _cumsum_sublanes(n_it)
    t = lax.broadcasted_iota(jnp.int32, (1, META_STRIDE), 1)
    e = count_le(it_end, t)
    ec = jnp.minimum(e, N_EXPERTS - 1)
    k = t - lookup(it_end - n_it, ec)
    rows_real = jnp.minimum(lookup(padded, ec) - k * ITEM_ROWS, ITEM_ROWS)
    row0_zero = total + (t - it_end[N_EXPERTS - 1:N_EXPERTS, 0:1]) * ITEM_ROWS
    rows_zero = jnp.clip(n_rows - row0_zero, 0, ITEM_ROWS)
    real = e < N_EXPERTS
    meta_ref[:, 0:META_STRIDE] = ec
    meta_ref[:, META_STRIDE:2 * META_STRIDE] = jnp.where(
        real, lookup(pstart, ec) + k * ITEM_ROWS, jnp.minimum(row0_zero, n_rows - MOE_BLOCK))
    meta_ref[:, 2 * META_STRIDE:3 * META_STRIDE] = jnp.where(real, rows_real, rows_zero) // MOE_BLOCK
    meta_ref[:, 3 * META_STRIDE:4 * META_STRIDE] = jnp.where(real, 0, 1)

    gap = padded - cnt
    gap_end = _cumsum_sublanes(gap)
    q = lax.broadcasted_iota(jnp.int32, pad_ref.shape, 1)
    eq = count_le(gap_end, q)
    inside = lookup(pstart + cnt - (gap_end - gap), eq) + q
    tail = total + q - gap_end[N_EXPERTS - 1:N_EXPERTS, 0:1]
    pad_ref[...] = jnp.where(eq < N_EXPERTS, inside, tail)


def _max_items(n_rows):
    return N_EXPERTS + n_rows // ITEM_ROWS + (N_EXPERTS * MOE_BLOCK) // ITEM_ROWS + 1


def _tables(cnt, idx_t, rank_t, n_rows):
    n_pad = n_rows - idx_t.size
    assert _max_items(n_rows) <= META_STRIDE
    dest, meta, pad_dest = pl.pallas_call(
        functools.partial(_tables_body, n_rows=n_rows),
        out_shape=[
            jax.ShapeDtypeStruct(idx_t.shape, jnp.int32),
            jax.ShapeDtypeStruct((1, 4 * META_STRIDE), jnp.int32),
            jax.ShapeDtypeStruct((1, n_pad), jnp.int32),
        ],
        name="tables",
    )(cnt, idx_t, rank_t)
    return dest.reshape(-1), meta.reshape(-1), pad_dest.reshape(-1)


def kernel(x, norm_mix, w_in, conv_w, conv_b, w_a, b_a, w_x, b_x, lru_lambda, attn_out_norm, lru_out_norm, w_out, norm_ffn, w_router, b_router, w_gate_up, b_gate_up, w_down, b_down, norm_final):
    b, seq, d = x.shape
    n = b * seq
    x2 = x.reshape(n, d)

    proj = _in_proj(x2, norm_mix, w_in)
    proj3 = proj.reshape(b, seq, IN_COLS)

    slopes = jnp.asarray(2.0 ** (-8.0 * np.arange(1, N_HEADS + 1) / N_HEADS), F32)
    attn = _attention(proj3, slopes)

    def pair_blocks(w):
        w4 = w.reshape(-1, 2, LRU_BLOCK, LRU_BLOCK)
        z = jnp.zeros_like(w4[:, 0])
        top = jnp.concatenate([w4[:, 0], z], axis=2)
        bot = jnp.concatenate([z, w4[:, 1]], axis=2)
        return jnp.concatenate([top, bot], axis=1)

    w_bd = jnp.concatenate([pair_blocks(w_a), pair_blocks(w_x)], axis=2)
    rec = _rglru(proj3, conv_w, conv_b, w_bd, b_a, b_x, lru_lambda)

    x_mid = _out_proj(attn.reshape(n, ATTN_WIDTH), rec.reshape(n, LRU_WIDTH),
                      attn_out_norm, lru_out_norm, x2, w_out)

    xn, idx_t, gate_t, rank_t, cnt = _router(x_mid, norm_ffn, w_router.T, b_router)
    n_rows = n * TOP_K + N_EXPERTS * MOE_BLOCK
    dest_flat, meta, pad_dest = _tables(cnt, idx_t, rank_t, n_rows)

    x_sorted = _scatter(dest_flat, pad_dest, xn, n_rows)
    act = _gate_up(meta, x_sorted, w_gate_up, b_gate_up)
    y_buf = _down(meta, act, w_down, b_down)
    out = _combine(dest_flat, y_buf, x_mid, gate_t.T, norm_final, d // DOWN_TILE)
    return out.reshape(b, seq, d)
```

```python
import functools

import numpy as np
import jax
import jax.numpy as jnp
from jax import lax
from jax.experimental import pallas as pl
from jax.experimental.pallas import tpu as pltpu

F32 = jnp.float32
BF16 = jnp.bfloat16
U32 = jnp.uint32
HIGH_HALF = np.uint32(0xFFFF0000)

D_MODEL = 2048
HEAD_DIM = 64
N_HEADS = 16
ATTN_WIDTH = N_HEADS * HEAD_DIM
LRU_WIDTH = D_MODEL - ATTN_WIDTH
LRU_BLOCK = 64
CONV_WIDTH = 4
LRU_C = 8.0
IN_COLS = 3 * ATTN_WIDTH + 2 * LRU_WIDTH
DILATIONS = (1, 4, 16)
ATTN_BLOCK = 128
UNITS_PER_TRIP = (5, 6, 8)
N_EXPERTS = 32
TOP_K = 4
D_FF = D_MODEL
SWIGLU_LIMIT = 7.0
SWIGLU_ALPHA = 1.702
MOE_BLOCK = 128
ITEM_ROWS = 16 * MOE_BLOCK
META_STRIDE = 128
RUN_BLOCKS = 4
SLOT_BLOCKS = (RUN_BLOCKS, RUN_BLOCKS, 2, 1)
EPS = 1e-6

LANES = 128
SUBLANES = 8
VMEM_LIMIT = 56 * 1024 * 1024


def _params(sem, vmem=VMEM_LIMIT):
    return pltpu.CompilerParams(dimension_semantics=sem, vmem_limit_bytes=vmem)


def _rms(xf, g):
    return xf * lax.rsqrt(jnp.mean(xf * xf, axis=-1, keepdims=True) + EPS) * g


def _pack_halves(x):
    half = x.shape[1] // 2
    lo = lax.bitcast_convert_type(x[:, :half].astype(BF16).astype(F32), U32) >> 16
    hi = lax.bitcast_convert_type(x[:, half:].astype(BF16).astype(F32), U32) & HIGH_HALF
    return lo | hi


def _unpack_halves(u):
    return (lax.bitcast_convert_type(u << 16, F32),
            lax.bitcast_convert_type(u & HIGH_HALF, F32))


def _in_proj_body(x_ref, g_ref, w_ref, o_ref, h_ref):
    @pl.when(pl.program_id(1) == 0)
    def _():
        h_ref[...] = _rms(x_ref[...], g_ref[...]).astype(BF16)

    o_ref[...] = jnp.dot(h_ref[...], w_ref[...].astype(BF16), preferred_element_type=F32)


def _in_proj(x2, g, w_in, tm=1024, tn=512):
    n, d = x2.shape
    cols = w_in.shape[1]
    return pl.pallas_call(
        _in_proj_body,
        grid=(n // tm, cols // tn),
        in_specs=[
            pl.BlockSpec((tm, d), lambda i, j: (i, 0)),
            pl.BlockSpec((1, d), lambda i, j: (0, 0)),
            pl.BlockSpec((d, tn), lambda i, j: (0, j)),
        ],
        out_specs=pl.BlockSpec((tm, tn), lambda i, j: (i, j)),
        out_shape=jax.ShapeDtypeStruct((n, cols), F32),
        scratch_shapes=[pltpu.VMEM((tm, d), BF16)],
        compiler_params=_params(("parallel", "arbitrary")),
        name="in_proj",
    )(x2, g.reshape(1, d), w_in)


REGROUP = 4


def _attention_body(slope_ref, q_ref, k_ref, v_ref, o_ref, qh, kh, vh, acc, mx):
    seq = q_ref.shape[1]
    run = seq // REGROUP
    pair = pl.program_id(1)
    qi = lax.broadcasted_iota(jnp.int32, (ATTN_BLOCK, 2 * ATTN_BLOCK), 0)
    ki = lax.broadcasted_iota(jnp.int32, (ATTN_BLOCK, 2 * ATTN_BLOCK), 1)
    rel2 = qi + ATTN_BLOCK - ki
    ok2 = (rel2 >= 0) & (rel2 <= ATTN_BLOCK)
    rel1 = (qi - ki)[:, :ATTN_BLOCK]
    ok1 = rel1 >= 0

    nh = LANES // HEAD_DIM
    slopes = [slope_ref[pair * nh + hh] for hh in range(nh)]
    def stage(order, rows_dst, rows_src):
        lane = lax.broadcasted_iota(jnp.int32, (run, LANES), 1)
        qv, vv = q_ref[0, rows_src, :] * (HEAD_DIM ** -0.5), v_ref[0, rows_src, :]
        if order:
            kh[rows_dst, :] = k_ref[0, rows_src, :]
        for hh in range(nh):
            mine = (lane >= hh * HEAD_DIM) & (lane < (hh + 1) * HEAD_DIM)
            qh[order, hh, rows_dst, :] = jnp.where(mine, qv, 0.0)
            vh[order, hh, rows_dst, :] = jnp.where(mine, vv, 1.0)

    for c in range(REGROUP):
        stage(0, slice(c * run, (c + 1) * run), pl.ds(c * run, run))
        stage(1, slice(c * run, (c + 1) * run), pl.ds(c, run, stride=REGROUP))

    def run_units(units):
        loaded = [(qh[order, hh, qs, :].astype(BF16),
                   (kh[ks, :] if order else k_ref[0, ks, :]).astype(BF16),
                   vh[order, hh, ks, :].astype(BF16)) for hh, br, order, qs, ks, bias in units]
        scores = [lax.dot_general(q, k, (((1,), (1,)), ((), ())), preferred_element_type=F32)
                  + u[5] for (q, k, v), u in zip(loaded, units)]
        maxes = [jnp.max(s, axis=1, keepdims=True) for s in scores]
        probs = [jnp.exp(s - m).astype(BF16) for s, m in zip(scores, maxes)]
        results = [(jnp.dot(p, v, preferred_element_type=F32), m)
                   for p, (q, k, v), m in zip(probs, loaded, maxes)]
        for (a, m), (hh, br, order, qs, ks, bias) in zip(results, units):
            acc[hh, br, qs, :] = a
            mx[hh, br, qs, :] = jnp.broadcast_to(m, (ATTN_BLOCK, LANES))

    def bias_pair(ok, rel, d):
        return [jnp.where(ok, -sl * (rel * d).astype(F32), -jnp.inf) for sl in slopes]

    def contiguous_branch(br, order, d, n_runs, per_trip):
        nb = seq // (n_runs * ATTN_BLOCK)
        bias1, bias2 = bias_pair(ok1, rel1, d), bias_pair(ok2, rel2, d)
        run_units([(hh, br, order, pl.ds(r * nb * ATTN_BLOCK, ATTN_BLOCK),
                    pl.ds(r * nb * ATTN_BLOCK, ATTN_BLOCK), bias1[hh])
                   for r in range(n_runs) for hh in range(nh)])

        def later(g, carry):
            units = []
            for u in range(per_trip):
                idx = g * per_trip + u
                blk = idx // (nb - 1) * nb + idx % (nb - 1) + 1
                start = pl.multiple_of(blk * ATTN_BLOCK, ATTN_BLOCK)
                units += [(hh, br, order, pl.ds(start, ATTN_BLOCK),
                           pl.ds(start - ATTN_BLOCK, 2 * ATTN_BLOCK), bias2[hh])
                          for hh in range(nh)]
            run_units(units)
            return carry

        lax.fori_loop(0, n_runs * (nb - 1) // per_trip, later, 0)

    contiguous_branch(0, 0, DILATIONS[0], 1, UNITS_PER_TRIP[0])
    contiguous_branch(1, 1, DILATIONS[1], REGROUP, UNITS_PER_TRIP[1])

    d2 = DILATIONS[2]
    sub = d2 // REGROUP
    bias16 = bias_pair(ok1, rel1, d2)

    def strided(g, carry):
        units = []
        for u in range(UNITS_PER_TRIP[2]):
            idx = g * UNITS_PER_TRIP[2] + u
            rows = pl.ds(idx // sub * run + idx % sub, ATTN_BLOCK, stride=sub)
            units += [(hh, 2, 1, rows, rows, bias16[hh]) for hh in range(nh)]
        run_units(units)
        return carry

    lax.fori_loop(0, d2 // UNITS_PER_TRIP[2], strided, 0)

    def merge(c, carry):
        start = c // (run // ATTN_BLOCK) + c % (run // ATTN_BLOCK) * (ATTN_BLOCK * REGROUP)
        nat = pl.ds(start, ATTN_BLOCK, stride=REGROUP)
        reg = pl.ds(pl.multiple_of(c * ATTN_BLOCK, ATTN_BLOCK), ATTN_BLOCK)
        outs = []
        for hh in range(nh):
            m0, m1, m2 = mx[hh, 0, nat, :], mx[hh, 1, reg, :], mx[hh, 2, reg, :]
            mt = jnp.maximum(jnp.maximum(m0, m1), m2)
            tot = (jnp.exp(m0 - mt) * acc[hh, 0, nat, :] + jnp.exp(m1 - mt) * acc[hh, 1, reg, :]
                   + jnp.exp(m2 - mt) * acc[hh, 2, reg, :])
            den = pltpu.roll(tot, HEAD_DIM, axis=1)
            outs.append(tot / den)
        lane_r = lax.broadcasted_iota(jnp.int32, (ATTN_BLOCK, LANES), 1)
        o_ref[0, nat, :] = jnp.where(lane_r < HEAD_DIM, outs[0], outs[1])
        return carry

    lax.fori_loop(0, seq // ATTN_BLOCK, merge, 0, unroll=2)


def _attention(proj3, slopes):
    b, seq, _ = proj3.shape
    npair = ATTN_WIDTH // LANES
    nh = LANES // HEAD_DIM
    blk = (1, seq, LANES)
    return pl.pallas_call(
        _attention_body,
        grid_spec=pltpu.PrefetchScalarGridSpec(
            num_scalar_prefetch=1,
            grid=(b, npair),
            in_specs=[
                pl.BlockSpec(blk, lambda i, j, s: (i, 0, j)),
                pl.BlockSpec(blk, lambda i, j, s: (i, 0, npair + j)),
                pl.BlockSpec(blk, lambda i, j, s: (i, 0, 2 * npair + j)),
            ],
            out_specs=pl.BlockSpec(blk, lambda i, j, s: (i, 0, j)),
            scratch_shapes=[
                pltpu.VMEM((2, nh, seq, LANES), F32),
                pltpu.VMEM((seq, LANES), F32),
                pltpu.VMEM((2, nh, seq, LANES), F32),
                pltpu.VMEM((nh, len(DILATIONS), seq, LANES), F32),
                pltpu.VMEM((nh, len(DILATIONS), seq, LANES), F32),
            ],
        ),
        out_shape=jax.ShapeDtypeStruct((b, seq, ATTN_WIDTH), F32),
        compiler_params=_params(("parallel", "parallel")),
        name="attention",
    )(slopes, proj3, proj3, proj3)


def _rglru_body(xr_ref, gr_ref, cw_ref, cb_ref, w_ref, ba_ref, bx_ref, lam_ref, o_ref,
                xp, a_s, b_s):
    seq = xr_ref.shape[1]
    pad = SUBLANES
    xp[0:pad, :] = jnp.zeros((pad, LANES), F32)
    xp[pad:pad + seq, :] = xr_ref[0]
    lam = lam_ref[...]
    sp = jnp.maximum(-lam, 0.0) + jnp.log(1.0 + jnp.exp(-jnp.abs(lam)))
    w_hi = w_ref[0].astype(BF16)
    w_lo = (w_ref[0] - w_hi.astype(F32)).astype(BF16)
    rows = 256

    def gates(c, carry):
        base = pl.multiple_of(c * rows, rows)
        xc = cb_ref[...] + cw_ref[0:1, :] * xp[pl.ds(base + pad - 3, rows), :]
        for i in range(1, CONV_WIDTH):
            xc = xc + cw_ref[i:i + 1, :] * xp[pl.ds(base + pad - 3 + i, rows), :]
        hi = xc.astype(BF16)
        lo = (xc - hi.astype(F32)).astype(BF16)
        pre = (jnp.dot(hi, w_hi, preferred_element_type=F32)
               + jnp.dot(lo, w_hi, preferred_element_type=F32)
               + jnp.dot(hi, w_lo, preferred_element_type=F32))
        r = jax.nn.sigmoid(pre[:, :LANES] + ba_ref[...])
        ig = jax.nn.sigmoid(pre[:, LANES:] + bx_ref[...])
        log_a = -LRU_C * r * sp
        a = jnp.exp(log_a)
        t = jnp.tanh(log_a)
        b = jnp.sqrt(-2.0 * t / (1.0 - t)) * (ig * xc)
        a_s[pl.ds(base, rows), :] = a
        b_s[pl.ds(base, rows), :] = b
        return carry

    lax.fori_loop(0, seq // rows, gates, 0)

    row = lax.broadcasted_iota(jnp.int32, (SUBLANES, LANES), 0)

    def scan(c, h_prev):
        sl = pl.ds(pl.multiple_of(c * SUBLANES, SUBLANES), SUBLANES)
        a = a_s[sl, :]
        b = b_s[sl, :]
        for s in (1, 2, 4):
            keep = row >= s
            a_sh = jnp.where(keep, pltpu.roll(a, s, axis=0), 1.0)
            b_sh = jnp.where(keep, pltpu.roll(b, s, axis=0), 0.0)
            b = a * b_sh + b
            a = a * a_sh
        h = a * h_prev + b
        o_ref[0, sl, :] = h * jax.nn.gelu(gr_ref[0, sl, :])
        return jnp.broadcast_to(h[SUBLANES - 1:SUBLANES, :], (SUBLANES, LANES))

    lax.fori_loop(0, seq // SUBLANES, scan, jnp.zeros((SUBLANES, LANES), F32), unroll=8)


def _rglru(proj3, conv_w, conv_b, w_bd, b_a, b_x, lam):
    b, seq, _ = proj3.shape
    nt = LRU_WIDTH // LANES
    xr0 = 3 * ATTN_WIDTH // LANES
    gr0 = xr0 + nt
    blk = (1, seq, LANES)
    vec = lambda: pl.BlockSpec((1, LANES), lambda i, j: (0, j))
    return pl.pallas_call(
        _rglru_body,
        grid=(b, nt),
        in_specs=[
            pl.BlockSpec(blk, lambda i, j: (i, 0, xr0 + j)),
            pl.BlockSpec(blk, lambda i, j: (i, 0, gr0 + j)),
            pl.BlockSpec((CONV_WIDTH, LANES), lambda i, j: (0, j)),
            vec(),
            pl.BlockSpec((1, LANES, 2 * LANES), lambda i, j: (j, 0, 0)),
            vec(), vec(), vec(),
        ],
        out_specs=pl.BlockSpec(blk, lambda i, j: (i, 0, j)),
        out_shape=jax.ShapeDtypeStruct((b, seq, LRU_WIDTH), F32),
        scratch_shapes=[
            pltpu.VMEM((seq + SUBLANES, LANES), F32),
            pltpu.VMEM((seq, LANES), F32),
            pltpu.VMEM((seq, LANES), F32),
        ],
        compiler_params=_params(("parallel", "parallel")),
        name="rglru",
    )(proj3, proj3, conv_w, conv_b.reshape(1, -1), w_bd, b_a.reshape(1, -1),
      b_x.reshape(1, -1), lam.reshape(1, -1))


def _out_proj_body(at_ref, rc_ref, ga_ref, gr_ref, x_ref, w_ref, o_ref, h_ref):
    wa = at_ref.shape[1]

    @pl.when(pl.program_id(1) == 0)
    def _():
        h_ref[:, :wa] = _rms(at_ref[...], ga_ref[...]).astype(BF16)
        h_ref[:, wa:] = _rms(rc_ref[...], gr_ref[...]).astype(BF16)

    o_ref[...] = x_ref[...] + jnp.dot(h_ref[...], w_ref[...].astype(BF16),
                                      preferred_element_type=F32)


def _out_proj(attn2, rec2, g_attn, g_rec, x2, w_out, tm=1024, tn=512):
    n, d = x2.shape
    wa, wr = attn2.shape[1], rec2.shape[1]
    return pl.pallas_call(
        _out_proj_body,
        grid=(n // tm, d // tn),
        in_specs=[
            pl.BlockSpec((tm, wa), lambda i, j: (i, 0)),
            pl.BlockSpec((tm, wr), lambda i, j: (i, 0)),
            pl.BlockSpec((1, wa), lambda i, j: (0, 0)),
            pl.BlockSpec((1, wr), lambda i, j: (0, 0)),
            pl.BlockSpec((tm, tn), lambda i, j: (i, j)),
            pl.BlockSpec((d, tn), lambda i, j: (0, j)),
        ],
        out_specs=pl.BlockSpec((tm, tn), lambda i, j: (i, j)),
        out_shape=jax.ShapeDtypeStruct((n, d), F32),
        scratch_shapes=[pltpu.VMEM((tm, d), BF16)],
        compiler_params=_params(("parallel", "arbitrary")),
        name="out_proj",
    )(attn2, rec2, g_attn.reshape(1, wa), g_rec.reshape(1, wr), x2, w_out)


def _router_body(x_ref, g_ref, wt_ref, b_ref, xn_ref, idx_ref, gate_ref, rank_ref, cnt_ref,
                 base):
    tm = x_ref.shape[0]

    @pl.when(pl.program_id(0) == 0)
    def _():
        base[...] = jnp.zeros_like(base)

    xn = _rms(x_ref[...], g_ref[...])
    xn_ref[...] = _pack_halves(xn)
    x_hi = xn.astype(BF16)
    x_lo = (xn - x_hi.astype(F32)).astype(BF16)
    w = wt_ref[...]
    w_hi = w.astype(BF16)
    w_lo = (w - w_hi.astype(F32)).astype(BF16)
    nt = (((1,), (1,)), ((), ()))
    logits = (lax.dot_general(w_hi, x_hi, nt, preferred_element_type=F32)
              + lax.dot_general(w_hi, x_lo, nt, preferred_element_type=F32)
              + lax.dot_general(w_lo, x_hi, nt, preferred_element_type=F32)
              + b_ref[...])

    eid = lax.broadcasted_iota(jnp.int32, (N_EXPERTS, tm), 0)
    work = logits
    vals, hots = [], []
    for _ in range(TOP_K):
        best = jnp.max(work, axis=0, keepdims=True)
        pick = jnp.min(jnp.where(work == best, eid, N_EXPERTS), axis=0, keepdims=True)
        hot = eid == pick
        vals.append(best)
        hots.append(hot)
        work = jnp.where(hot, -jnp.inf, work)
        idx_ref[len(vals) - 1:len(vals), :] = pick

    ex = [jnp.exp(v - vals[0]) for v in vals]
    den = ex[0] + ex[1] + ex[2] + ex[3]
    for k in range(TOP_K):
        gate_ref[k:k + 1, :] = ex[k] / den

    chosen = (hots[0] | hots[1] | hots[2] | hots[3])
    si = lax.broadcasted_iota(jnp.int32, (tm, tm), 0)
    ti = lax.broadcasted_iota(jnp.int32, (tm, tm), 1)
    before = (si < ti).astype(BF16)
    prefix = jnp.dot(chosen.astype(BF16), before, preferred_element_type=F32)
    slot = base[:, 0:1] + prefix
    for k in range(TOP_K):
        rank_ref[k:k + 1, :] = jnp.sum(jnp.where(hots[k], slot, 0.0), axis=0,
                                       keepdims=True).astype(jnp.int32)
    base[...] = base[...] + jnp.sum(chosen.astype(F32), axis=1, keepdims=True)
    cnt_ref[...] = base[...].astype(jnp.int32)


def _router(x2, g, w_router_t, b_router, tm=256):
    n, d = x2.shape
    row = lambda: pl.BlockSpec((TOP_K, tm), lambda i: (0, i))
    return pl.pallas_call(
        _router_body,
        grid=(n // tm,),
        in_specs=[
            pl.BlockSpec((tm, d), lambda i: (i, 0)),
            pl.BlockSpec((1, d), lambda i: (0, 0)),
            pl.BlockSpec((N_EXPERTS, d), lambda i: (0, 0)),
            pl.BlockSpec((N_EXPERTS, 1), lambda i: (0, 0)),
        ],
        out_specs=[
            pl.BlockSpec((tm, d // 2), lambda i: (i, 0)),
            row(), row(), row(),
            pl.BlockSpec((N_EXPERTS, LANES), lambda i: (0, 0)),
        ],
        out_shape=[
            jax.ShapeDtypeStruct((n, d // 2), U32),
            jax.ShapeDtypeStruct((TOP_K, n), jnp.int32),
            jax.ShapeDtypeStruct((TOP_K, n), F32),
            jax.ShapeDtypeStruct((TOP_K, n), jnp.int32),
            jax.ShapeDtypeStruct((N_EXPERTS, LANES), jnp.int32),
        ],
        scratch_shapes=[pltpu.VMEM((N_EXPERTS, LANES), F32)],
        compiler_params=_params(("arbitrary",)),
        name="router",
    )(x2, g.reshape(1, d), w_router_t, b_router.reshape(N_EXPERTS, 1))


def _scatter_body(dest_ref, pad_ref, xn_ref, o_ref, zrow, sem, *, n_tokens, pad_per_step):
    tm = xn_ref.shape[0]
    i = pl.program_id(0)

    @pl.when(i == 0)
    def _():
        zrow[...] = jnp.zeros_like(zrow)

    def row_copy(t, k):
        d = dest_ref[k * n_tokens + i * tm + t]
        return pltpu.make_async_copy(xn_ref.at[pl.ds(t, 1), :], o_ref.at[pl.ds(d, 1), :], sem)

    def pad_copy(q):
        d = pad_ref[i * pad_per_step + q]
        return pltpu.make_async_copy(zrow.at[pl.ds(0, 1), :], o_ref.at[pl.ds(d, 1), :], sem)

    def issue(t, c):
        for k in range(TOP_K):
            row_copy(t, k).start(priority=k % 2)
        return c

    lax.fori_loop(0, tm, issue, 0)

    def issue_pad(q, c):
        pad_copy(q).start()
        return c

    lax.fori_loop(0, pad_per_step, issue_pad, 0)

    def drain(t, c):
        for k in range(TOP_K):
            row_copy(t, k).wait()
        return c

    lax.fori_loop(0, tm, drain, 0)

    def drain_pad(q, c):
        pad_copy(q).wait()
        return c

    lax.fori_loop(0, pad_per_step, drain_pad, 0)


def _scatter(dest_flat, pad_dest, xn, n_rows, tm=256):
    n, d = xn.shape
    steps = n // tm
    pad_per_step = pad_dest.shape[0] // steps
    return pl.pallas_call(
        functools.partial(_scatter_body, n_tokens=n, pad_per_step=pad_per_step),
        grid_spec=pltpu.PrefetchScalarGridSpec(
            num_scalar_prefetch=2,
            grid=(steps,),
            in_specs=[pl.BlockSpec((tm, d), lambda i, *_: (i, 0))],
            out_specs=pl.BlockSpec(memory_space=pl.ANY),
            scratch_shapes=[pltpu.VMEM((SUBLANES, d), xn.dtype), pltpu.SemaphoreType.DMA(())],
        ),
        out_shape=jax.ShapeDtypeStruct((n_rows, d), xn.dtype),
        compiler_params=_params(("arbitrary",)),
        name="scatter",
    )(dest_flat, pad_dest, xn)


def _moe_body(meta_ref, x_hbm, w_ref, b_ref, o_hbm, xbf, stage, wbf, obuf, sem_in, sem_out,
              pend, *, prep_w, epilogue):
    t, j = pl.program_id(0), pl.program_id(1)
    row0 = meta_ref[META_STRIDE + t]
    nsub = meta_ref[2 * META_STRIDE + t]
    zero = meta_ref[3 * META_STRIDE + t]
    tn_out = obuf.shape[2]
    col0 = pl.multiple_of(j * tn_out, tn_out)

    @pl.when((t == 0) & (j == 0))
    def _():
        for k in range(len(SLOT_BLOCKS)):
            pend[k] = 0

    def in_copy(c, slot):
        src = x_hbm.at[pl.ds(pl.multiple_of(row0 + c * MOE_BLOCK, MOE_BLOCK), MOE_BLOCK), :]
        if stage is None:
            dst = xbf.at[pl.ds(pl.multiple_of(c * MOE_BLOCK, MOE_BLOCK), MOE_BLOCK), :]
        else:
            dst = stage.at[slot]
        return pltpu.make_async_copy(src, dst, sem_in.at[slot])

    def fetch_rows(c):
        slot = c % 2

        @pl.when(c + 1 < nsub)
        def _():
            in_copy(c + 1, 1 - slot).start()

        in_copy(c, slot).wait()
        if stage is not None:
            rows = pl.ds(pl.multiple_of(c * MOE_BLOCK, MOE_BLOCK), MOE_BLOCK)
            lo, hi = _unpack_halves(stage[slot])
            xbf[rows, :lo.shape[1]] = lo.astype(BF16)
            xbf[rows, lo.shape[1]:] = hi.astype(BF16)

    def out_copy(first_block, slot, n_blocks):
        n_rows = n_blocks * MOE_BLOCK
        rows = pl.ds(pl.multiple_of(row0 + first_block * MOE_BLOCK, MOE_BLOCK), n_rows)
        return pltpu.make_async_copy(obuf.at[slot, pl.ds(0, n_rows), :],
                                     o_hbm.at[rows, pl.ds(col0, tn_out)], sem_out.at[slot])

    def emit(first_block, slot, n_blocks, value_fn):
        @pl.when(pend[slot] == 1)
        def _():
            out_copy(first_block, slot, n_blocks).wait()

        obuf[slot, pl.ds(0, n_blocks * MOE_BLOCK), :] = value_fn()
        out_copy(first_block, slot, n_blocks).start()
        pend[slot] = 1

    def walk(block_fn):
        def full(p, carry):
            block_fn(p * RUN_BLOCKS, p % 2, RUN_BLOCKS)
            return carry

        lax.fori_loop(0, nsub // RUN_BLOCKS, full, 0)
        done = nsub // RUN_BLOCKS * RUN_BLOCKS
        for slot in range(2, len(SLOT_BLOCKS)):
            n_blocks = SLOT_BLOCKS[slot]

            @pl.when((nsub - done) & n_blocks != 0)
            def _():
                block_fn(done + ((nsub - done) & ~(2 * n_blocks - 1)), slot, n_blocks)

    @pl.when((nsub > 0) & (zero == 0))
    def _compute():
        @pl.when(j == 0)
        def _():
            in_copy(0, 0).start()

        prep_w(w_ref, wbf)

        def block(first_block, slot, n_blocks):
            @pl.when(j == 0)
            def _():
                for k in range(n_blocks):
                    fetch_rows(first_block + k)

            rows = pl.ds(pl.multiple_of(first_block * MOE_BLOCK, MOE_BLOCK), n_blocks * MOE_BLOCK)
            emit(first_block, slot, n_blocks, lambda: epilogue(xbf[rows, :], wbf, b_ref))

        walk(block)

    @pl.when((nsub > 0) & (zero == 1))
    def _zeros():
        walk(lambda first_block, slot, n_blocks: emit(
            first_block, slot, n_blocks,
            lambda: jnp.zeros((n_blocks * MOE_BLOCK, tn_out), obuf.dtype)))

    @pl.when((t == pl.num_programs(0) - 1) & (j == pl.num_programs(1) - 1))
    def _drain():
        for k, n_blocks in enumerate(SLOT_BLOCKS):
            @pl.when(pend[k] == 1)
            def _():
                out_copy(0, k, n_blocks).wait()


def _moe_call(name, meta, x, w, b, out_cols, out_dtype, tn, tn_out, prep_w, epilogue, tmp_shape):
    p, kdim = x.shape[0], w.shape[1]
    n_items = _max_items(p)
    n_tiles = w.shape[2] // tn
    needs_stage = x.dtype != BF16

    def wmap(t, j, m):
        live = (m[2 * META_STRIDE + t] > 0) & (m[3 * META_STRIDE + t] == 0)
        return (m[t], 0, jnp.where(live, j, n_tiles - 1))

    scratch = [pltpu.VMEM((ITEM_ROWS, kdim), BF16)]
    if needs_stage:
        scratch.append(pltpu.VMEM((2, MOE_BLOCK, x.shape[1]), x.dtype))
    scratch += [
        pltpu.VMEM((kdim, tn), BF16),
        pltpu.VMEM((len(SLOT_BLOCKS), RUN_BLOCKS * MOE_BLOCK, tn_out), out_dtype),
        pltpu.SemaphoreType.DMA((2,)),
        pltpu.SemaphoreType.DMA((len(SLOT_BLOCKS),)),
        pltpu.SMEM((len(SLOT_BLOCKS),), jnp.int32),
    ]
    if tmp_shape is not None:
        scratch.append(pltpu.VMEM(tmp_shape, F32))

    def body(meta_ref, x_hbm, w_ref, b_ref, o_hbm, xbf, *rest):
        rest = list(rest)
        stage = rest.pop(0) if needs_stage else None
        wbf, obuf, sem_in, sem_out, pend = rest[:5]
        tmp = rest[5] if tmp_shape is not None else None
        _moe_body(meta_ref, x_hbm, w_ref, b_ref, o_hbm, xbf, stage, wbf, obuf, sem_in, sem_out,
                  pend, prep_w=functools.partial(prep_w, tmp=tmp), epilogue=epilogue)

    return pl.pallas_call(
        body,
        grid_spec=pltpu.PrefetchScalarGridSpec(
            num_scalar_prefetch=1,
            grid=(n_items, n_tiles),
            in_specs=[
                pl.BlockSpec(memory_space=pl.ANY),
                pl.BlockSpec((1, kdim, tn), wmap),
                pl.BlockSpec((1, 1, tn), wmap),
            ],
            out_specs=pl.BlockSpec(memory_space=pl.ANY),
            scratch_shapes=scratch,
        ),
        out_shape=jax.ShapeDtypeStruct((p, out_cols), out_dtype),
        compiler_params=_params(("arbitrary", "arbitrary")),
        name=name,
    )(meta, x, w, b.reshape(N_EXPERTS, 1, -1))


def _cast_weights(w_ref, wbf, tmp=None, rows=256):
    def cast(c, carry):
        sl = pl.ds(pl.multiple_of(c * rows, rows), rows)
        wbf[sl, :] = w_ref[0, sl, :].astype(BF16)
        return carry

    lax.fori_loop(0, wbf.shape[0] // rows, cast, 0)


def _swiglu_tile(x, wbf, b_ref):
    even = lax.broadcasted_iota(jnp.int32, (x.shape[0], LANES), 1) % 2 == 0
    cols = []
    for c in range(wbf.shape[1] // (2 * LANES)):
        c0 = c * 2 * LANES
        gu = jnp.dot(x, wbf[:, c0:c0 + 2 * LANES], preferred_element_type=F32)
        gu = gu + b_ref[0, :, c0:c0 + 2 * LANES]
        v1, v2 = gu[:, :LANES], gu[:, LANES:]
        gate = jnp.where(even, v1, pltpu.roll(v2, 1, axis=1))
        up = jnp.where(even, pltpu.roll(v1, LANES - 1, axis=1), v2)
        gate = jnp.minimum(gate, SWIGLU_LIMIT)
        up = jnp.clip(up, -SWIGLU_LIMIT, SWIGLU_LIMIT)
        cols.append((gate * jax.nn.sigmoid(SWIGLU_ALPHA * gate) * (up + 1.0)).astype(BF16))
    return jnp.concatenate(cols, axis=1)


def _gate_up(meta, x_sorted, w_gate_up, b_gate_up, tn=1024):
    return _moe_call("gate_up", meta, x_sorted, w_gate_up, b_gate_up, w_gate_up.shape[2] // 2,
                     BF16, tn, tn // 2, _cast_weights, _swiglu_tile, None)


def _permute_weights(w_ref, wbf, tmp):
    half = LANES // 2

    def permute(g, carry):
        base = pl.multiple_of(g * LANES, LANES)
        for c in range(tmp.shape[0]):
            cs = slice(c * LANES, (c + 1) * LANES)
            tmp[c, pl.ds(0, half, stride=2), :] = w_ref[0, pl.ds(base, half), cs]
            tmp[c, pl.ds(1, half, stride=2), :] = w_ref[0, pl.ds(base + half, half), cs]
            wbf[pl.ds(base, LANES), cs] = tmp[c].astype(BF16)
        return carry

    lax.fori_loop(0, wbf.shape[0] // LANES, permute, 0)


def _linear_tile(a, wbf, b_ref):
    return _pack_halves(jnp.dot(a, wbf[...], preferred_element_type=F32) + b_ref[0])


DOWN_TILE = 1024


def _down(meta, act, w_down, b_down, tn=DOWN_TILE):
    return _moe_call("down", meta, act, w_down, b_down, w_down.shape[2] // 2, U32, tn, tn // 2,
                     _permute_weights, _linear_tile, (tn // LANES, LANES, LANES))


def _combine_body(dest_ref, y_ref, x_ref, gate_ref, g_ref, o_ref, ybuf, sem, *, n_tokens,
                  y_tiles):
    tm = x_ref.shape[0]
    i = pl.program_id(0)

    def row_copy(t, k):
        d = dest_ref[k * n_tokens + i * tm + t]
        return pltpu.make_async_copy(y_ref.at[pl.ds(d, 1), :], ybuf.at[k, pl.ds(t, 1), :], sem)

    def issue(t, c):
        for k in range(TOP_K):
            row_copy(t, k).start(priority=k % 2)
        return c

    lax.fori_loop(0, tm, issue, 0)

    def drain(t, c):
        for k in range(TOP_K):
            row_copy(t, k).wait()
        return c

    lax.fori_loop(0, tm, drain, 0)

    acc = x_ref[...]
    tile = x_ref.shape[1] // y_tiles
    for k in range(TOP_K):
        parts = []
        for c in range(y_tiles):
            parts += _unpack_halves(ybuf[k, :, c * tile // 2:(c + 1) * tile // 2])
        acc = acc + gate_ref[:, k:k + 1] * jnp.concatenate(parts, axis=1)
    o_ref[...] = _rms(acc, g_ref[...])


def _combine(dest_flat, y_buf, x2, gates, g, y_tiles, tm=256):
    n, d = x2.shape
    return pl.pallas_call(
        functools.partial(_combine_body, n_tokens=n, y_tiles=y_tiles),
        grid_spec=pltpu.PrefetchScalarGridSpec(
            num_scalar_prefetch=1,
            grid=(n // tm,),
            in_specs=[
                pl.BlockSpec(memory_space=pl.ANY),
                pl.BlockSpec((tm, d), lambda i, *_: (i, 0)),
                pl.BlockSpec((tm, TOP_K), lambda i, *_: (i, 0)),
                pl.BlockSpec((1, d), lambda i, *_: (0, 0)),
            ],
            out_specs=pl.BlockSpec((tm, d), lambda i, *_: (i, 0)),
            scratch_shapes=[pltpu.VMEM((TOP_K, tm) + y_buf.shape[1:], y_buf.dtype),
                            pltpu.SemaphoreType.DMA(())],
        ),
        out_shape=jax.ShapeDtypeStruct((n, d), F32),
        compiler_params=_params(("arbitrary",)),
        name="combine",
    )(dest_flat, y_buf, x2, gates, g.reshape(1, d))


def _cumsum_sublanes(x):
    row = lax.broadcasted_iota(jnp.int32, x.shape, 0)
    s = 1
    while s < x.shape[0]:
        x = x + jnp.where(row >= s, pltpu.roll(x, s, axis=0), 0)
        s *= 2
    return x


def _tables_body(cnt_ref, idx_ref, rank_ref, dest_ref, meta_ref, pad_ref, *, n_rows):
    cnt = cnt_ref[...]
    padded = (cnt + (MOE_BLOCK - 1)) // MOE_BLOCK * MOE_BLOCK
    pend = _cumsum_sublanes(padded)
    pstart = pend - padded
    total = pend[N_EXPERTS - 1:N_EXPERTS, 0:1]

    def lookup(table, sel):
        eid = lax.broadcasted_iota(jnp.int32, (N_EXPERTS, sel.shape[1]), 0)
        return jnp.sum(jnp.where(eid == sel, table[:, 0:1], 0), axis=0, keepdims=True)

    def count_le(table, v):
        return jnp.sum((table[:, 0:1] <= v).astype(jnp.int32), axis=0, keepdims=True)

    width = 2048
    for k in range(TOP_K):
        for c in range(idx_ref.shape[1] // width):
            sl = slice(c * width, (c + 1) * width)
            dest_ref[k:k + 1, sl] = lookup(pstart, idx_ref[k:k + 1, sl]) + rank_ref[k:k + 1, sl]

    n_it = (padded + (ITEM_ROWS - 1)) // ITEM_ROWS
    it_end = _cumsum_sublanes(n_it)
    t = lax.broadcasted_iota(jnp.int32, (1, META_STRIDE), 1)
    e = count_le(it_end, t)
    ec = jnp.minimum(e, N_EXPERTS - 1)
    k = t - lookup(it_end - n_it, ec)
    rows_real = jnp.minimum(lookup(padded, ec) - k * ITEM_ROWS, ITEM_ROWS)
    row0_zero = total + (t - it_end[N_EXPERTS - 1:N_EXPERTS, 0:1]) * ITEM_ROWS
    rows_zero = jnp.clip(n_rows - row0_zero, 0, ITEM_ROWS)
    real = e < N_EXPERTS
    meta_ref[:, 0:META_STRIDE] = ec
    meta_ref[:, META_STRIDE:2 * META_STRIDE] = jnp.where(
        real, lookup(pstart, ec) + k * ITEM_ROWS, jnp.minimum(row0_zero, n_rows - MOE_BLOCK))
    meta_ref[:, 2 * META_STRIDE:3 * META_STRIDE] = jnp.where(real, rows_real, rows_zero) // MOE_BLOCK
    meta_ref[:, 3 * META_STRIDE:4 * META_STRIDE] = jnp.where(real, 0, 1)

    gap = padded - cnt
    gap_end = _cumsum_sublanes(gap)
    q = lax.broadcasted_iota(jnp.int32, pad_ref.shape, 1)
    eq = count_le(gap_end, q)
    inside = lookup(pstart + cnt - (gap_end - gap), eq) + q
    tail = total + q - gap_end[N_EXPERTS - 1:N_EXPERTS, 0:1]
    pad_ref[...] = jnp.where(eq < N_EXPERTS, inside, tail)


def _max_items(n_rows):
    return N_EXPERTS + n_rows // ITEM_ROWS + (N_EXPERTS * MOE_BLOCK) // ITEM_ROWS + 1


def _tables(cnt, idx_t, rank_t, n_rows):
    n_pad = n_rows - idx_t.size
    assert _max_items(n_rows) <= META_STRIDE
    dest, meta, pad_dest = pl.pallas_call(
        functools.partial(_tables_body, n_rows=n_rows),
        out_shape=[
            jax.ShapeDtypeStruct(idx_t.shape, jnp.int32),
            jax.ShapeDtypeStruct((1, 4 * META_STRIDE), jnp.int32),
            jax.ShapeDtypeStruct((1, n_pad), jnp.int32),
        ],
        name="tables",
    )(cnt, idx_t, rank_t)
    return dest.reshape(-1), meta.reshape(-1), pad_dest.reshape(-1)


def kernel(x, norm_mix, w_in, conv_w, conv_b, w_a, b_a, w_x, b_x, lru_lambda, attn_out_norm, lru_out_norm, w_out, norm_ffn, w_router, b_router, w_gate_up, b_gate_up, w_down, b_down, norm_final):
    b, seq, d = x.shape
    n = b * seq
    x2 = x.reshape(n, d)

    proj = _in_proj(x2, norm_mix, w_in)
    proj3 = proj.reshape(b, seq, IN_COLS)

    slopes = jnp.asarray(2.0 ** (-8.0 * np.arange(1, N_HEADS + 1) / N_HEADS), F32)
    attn = _attention(proj3, slopes)

    def pair_blocks(w):
        w4 = w.reshape(-1, 2, LRU_BLOCK, LRU_BLOCK)
        z = jnp.zeros_like(w4[:, 0])
        top = jnp.concatenate([w4[:, 0], z], axis=2)
        bot = jnp.concatenate([z, w4[:, 1]], axis=2)
        return jnp.concatenate([top, bot], axis=1)

    w_bd = jnp.concatenate([pair_blocks(w_a), pair_blocks(w_x)], axis=2)
    rec = _rglru(proj3, conv_w, conv_b, w_bd, b_a, b_x, lru_lambda)

    x_mid = _out_proj(attn.reshape(n, ATTN_WIDTH), rec.reshape(n, LRU_WIDTH),
                      attn_out_norm, lru_out_norm, x2, w_out)

    xn, idx_t, gate_t, rank_t, cnt = _router(x_mid, norm_ffn, w_router.T, b_router)
    n_rows = n * TOP_K + N_EXPERTS * MOE_BLOCK
    dest_flat, meta, pad_dest = _tables(cnt, idx_t, rank_t, n_rows)

    x_sorted = _scatter(dest_flat, pad_dest, xn, n_rows)
    act = _gate_up(meta, x_sorted, w_gate_up, b_gate_up)
    y_buf = _down(meta, act, w_down, b_down)
    out = _combine(dest_flat, y_buf, x_mid, gate_t.T, norm_final, d // DOWN_TILE)
    return out.reshape(b, seq, d)
```

```python
import functools

import numpy as np
import jax
import jax.numpy as jnp
from jax import lax
from jax.experimental import pallas as pl
from jax.experimental.pallas import tpu as pltpu

F32 = jnp.float32
BF16 = jnp.bfloat16
U32 = jnp.uint32
HIGH_HALF = np.uint32(0xFFFF0000)

D_MODEL = 2048
HEAD_DIM = 64
N_HEADS = 16
ATTN_WIDTH = N_HEADS * HEAD_DIM
LRU_WIDTH = D_MODEL - ATTN_WIDTH
LRU_BLOCK = 64
CONV_WIDTH = 4
LRU_C = 8.0
IN_COLS = 3 * ATTN_WIDTH + 2 * LRU_WIDTH
DILATIONS = (1, 4, 16)
ATTN_BLOCK = 128
UNITS_PER_TRIP = (5, 6, 8)
N_EXPERTS = 32
TOP_K = 4
D_FF = D_MODEL
SWIGLU_LIMIT = 7.0
SWIGLU_ALPHA = 1.702
MOE_BLOCK = 128
ITEM_ROWS = 16 * MOE_BLOCK
META_STRIDE = 128
RUN_BLOCKS = 4
LAST_SLOT = 2
MAX_LAST = 2 * RUN_BLOCKS - 1
EPS = 1e-6

LANES = 128
SUBLANES = 8
VMEM_LIMIT = 56 * 1024 * 1024


def _params(sem, vmem=VMEM_LIMIT):
    return pltpu.CompilerParams(dimension_semantics=sem, vmem_limit_bytes=vmem)


def _rms(xf, g):
    return xf * lax.rsqrt(jnp.mean(xf * xf, axis=-1, keepdims=True) + EPS) * g


def _pack_halves(x):
    half = x.shape[1] // 2
    lo = lax.bitcast_convert_type(x[:, :half].astype(BF16).astype(F32), U32) >> 16
    hi = lax.bitcast_convert_type(x[:, half:].astype(BF16).astype(F32), U32) & HIGH_HALF
    return lo | hi


def _unpack_halves(u):
    return (lax.bitcast_convert_type(u << 16, F32),
            lax.bitcast_convert_type(u & HIGH_HALF, F32))


def _in_proj_body(x_ref, g_ref, w_ref, o_ref, h_ref):
    @pl.when(pl.program_id(1) == 0)
    def _():
        h_ref[...] = _rms(x_ref[...], g_ref[...]).astype(BF16)

    o_ref[...] = jnp.dot(h_ref[...], w_ref[...].astype(BF16), preferred_element_type=F32)


def _in_proj(x2, g, w_in, tm=1024, tn=512):
    n, d = x2.shape
    cols = w_in.shape[1]
    return pl.pallas_call(
        _in_proj_body,
        grid=(n // tm, cols // tn),
        in_specs=[
            pl.BlockSpec((tm, d), lambda i, j: (i, 0)),
            pl.BlockSpec((1, d), lambda i, j: (0, 0)),
            pl.BlockSpec((d, tn), lambda i, j: (0, j)),
        ],
        out_specs=pl.BlockSpec((tm, tn), lambda i, j: (i, j)),
        out_shape=jax.ShapeDtypeStruct((n, cols), F32),
        scratch_shapes=[pltpu.VMEM((tm, d), BF16)],
        compiler_params=_params(("parallel", "arbitrary")),
        name="in_proj",
    )(x2, g.reshape(1, d), w_in)


REGROUP = 4


def _attention_body(slope_ref, q_ref, k_ref, v_ref, o_ref, qh, kh, vh, acc, mx):
    seq = q_ref.shape[1]
    run = seq // REGROUP
    pair = pl.program_id(1)
    qi = lax.broadcasted_iota(jnp.int32, (ATTN_BLOCK, 2 * ATTN_BLOCK), 0)
    ki = lax.broadcasted_iota(jnp.int32, (ATTN_BLOCK, 2 * ATTN_BLOCK), 1)
    rel2 = qi + ATTN_BLOCK - ki
    ok2 = (rel2 >= 0) & (rel2 <= ATTN_BLOCK)
    rel1 = (qi - ki)[:, :ATTN_BLOCK]
    ok1 = rel1 >= 0

    nh = LANES // HEAD_DIM
    slopes = [slope_ref[pair * nh + hh] for hh in range(nh)]
    def stage(order, rows_dst, rows_src):
        lane = lax.broadcasted_iota(jnp.int32, (run, LANES), 1)
        qv, vv = q_ref[0, rows_src, :] * (HEAD_DIM ** -0.5), v_ref[0, rows_src, :]
        if order:
            kh[rows_dst, :] = k_ref[0, rows_src, :]
        for hh in range(nh):
            mine = (lane >= hh * HEAD_DIM) & (lane < (hh + 1) * HEAD_DIM)
            qh[order, hh, rows_dst, :] = jnp.where(mine, qv, 0.0)
            vh[order, hh, rows_dst, :] = jnp.where(mine, vv, 1.0)

    for c in range(REGROUP):
        stage(0, slice(c * run, (c + 1) * run), pl.ds(c * run, run))
        stage(1, slice(c * run, (c + 1) * run), pl.ds(c, run, stride=REGROUP))

    def run_units(units):
        loaded = [(qh[order, hh, qs, :].astype(BF16),
                   (kh[ks, :] if order else k_ref[0, ks, :]).astype(BF16),
                   vh[order, hh, ks, :].astype(BF16)) for hh, br, order, qs, ks, bias in units]
        scores = [lax.dot_general(q, k, (((1,), (1,)), ((), ())), preferred_element_type=F32)
                  + u[5] for (q, k, v), u in zip(loaded, units)]
        maxes = [jnp.max(s, axis=1, keepdims=True) for s in scores]
        probs = [jnp.exp(s - m).astype(BF16) for s, m in zip(scores, maxes)]
        results = [(jnp.dot(p, v, preferred_element_type=F32), m)
                   for p, (q, k, v), m in zip(probs, loaded, maxes)]
        for (a, m), (hh, br, order, qs, ks, bias) in zip(results, units):
            acc[hh, br, qs, :] = a
            mx[hh, br, qs, :] = jnp.broadcast_to(m, (ATTN_BLOCK, LANES))

    def bias_pair(ok, rel, d):
        return [jnp.where(ok, -sl * (rel * d).astype(F32), -jnp.inf) for sl in slopes]

    def contiguous_branch(br, order, d, n_runs, per_trip):
        nb = seq // (n_runs * ATTN_BLOCK)
        bias1, bias2 = bias_pair(ok1, rel1, d), bias_pair(ok2, rel2, d)
        run_units([(hh, br, order, pl.ds(r * nb * ATTN_BLOCK, ATTN_BLOCK),
                    pl.ds(r * nb * ATTN_BLOCK, ATTN_BLOCK), bias1[hh])
                   for r in range(n_runs) for hh in range(nh)])

        def later(g, carry):
            units = []
            for u in range(per_trip):
                idx = g * per_trip + u
                blk = idx // (nb - 1) * nb + idx % (nb - 1) + 1
                start = pl.multiple_of(blk * ATTN_BLOCK, ATTN_BLOCK)
                units += [(hh, br, order, pl.ds(start, ATTN_BLOCK),
                           pl.ds(start - ATTN_BLOCK, 2 * ATTN_BLOCK), bias2[hh])
                          for hh in range(nh)]
            run_units(units)
            return carry

        lax.fori_loop(0, n_runs * (nb - 1) // per_trip, later, 0)

    contiguous_branch(0, 0, DILATIONS[0], 1, UNITS_PER_TRIP[0])
    contiguous_branch(1, 1, DILATIONS[1], REGROUP, UNITS_PER_TRIP[1])

    d2 = DILATIONS[2]
    sub = d2 // REGROUP
    bias16 = bias_pair(ok1, rel1, d2)

    def strided(g, carry):
        units = []
        for u in range(UNITS_PER_TRIP[2]):
            idx = g * UNITS_PER_TRIP[2] + u
            rows = pl.ds(idx // sub * run + idx % sub, ATTN_BLOCK, stride=sub)
            units += [(hh, 2, 1, rows, rows, bias16[hh]) for hh in range(nh)]
        run_units(units)
        return carry

    lax.fori_loop(0, d2 // UNITS_PER_TRIP[2], strided, 0)

    def merge(c, carry):
        start = c // (run // ATTN_BLOCK) + c % (run // ATTN_BLOCK) * (ATTN_BLOCK * REGROUP)
        nat = pl.ds(start, ATTN_BLOCK, stride=REGROUP)
        reg = pl.ds(pl.multiple_of(c * ATTN_BLOCK, ATTN_BLOCK), ATTN_BLOCK)
        outs = []
        for hh in range(nh):
            m0, m1, m2 = mx[hh, 0, nat, :], mx[hh, 1, reg, :], mx[hh, 2, reg, :]
            mt = jnp.maximum(jnp.maximum(m0, m1), m2)
            tot = (jnp.exp(m0 - mt) * acc[hh, 0, nat, :] + jnp.exp(m1 - mt) * acc[hh, 1, reg, :]
                   + jnp.exp(m2 - mt) * acc[hh, 2, reg, :])
            den = pltpu.roll(tot, HEAD_DIM, axis=1)
            outs.append(tot / den)
        lane_r = lax.broadcasted_iota(jnp.int32, (ATTN_BLOCK, LANES), 1)
        o_ref[0, nat, :] = jnp.where(lane_r < HEAD_DIM, outs[0], outs[1])
        return carry

    lax.fori_loop(0, seq // ATTN_BLOCK, merge, 0, unroll=2)


def _attention(proj3, slopes):
    b, seq, _ = proj3.shape
    npair = ATTN_WIDTH // LANES
    nh = LANES // HEAD_DIM
    blk = (1, seq, LANES)
    return pl.pallas_call(
        _attention_body,
        grid_spec=pltpu.PrefetchScalarGridSpec(
            num_scalar_prefetch=1,
            grid=(b, npair),
            in_specs=[
                pl.BlockSpec(blk, lambda i, j, s: (i, 0, j)),
                pl.BlockSpec(blk, lambda i, j, s: (i, 0, npair + j)),
                pl.BlockSpec(blk, lambda i, j, s: (i, 0, 2 * npair + j)),
            ],
            out_specs=pl.BlockSpec(blk, lambda i, j, s: (i, 0, j)),
            scratch_shapes=[
                pltpu.VMEM((2, nh, seq, LANES), F32),
                pltpu.VMEM((seq, LANES), F32),
                pltpu.VMEM((2, nh, seq, LANES), F32),
                pltpu.VMEM((nh, len(DILATIONS), seq, LANES), F32),
                pltpu.VMEM((nh, len(DILATIONS), seq, LANES), F32),
            ],
        ),
        out_shape=jax.ShapeDtypeStruct((b, seq, ATTN_WIDTH), F32),
        compiler_params=_params(("parallel", "parallel")),
        name="attention",
    )(slopes, proj3, proj3, proj3)


def _rglru_body(xr_ref, gr_ref, cw_ref, cb_ref, w_ref, ba_ref, bx_ref, lam_ref, o_ref,
                xp, a_s, b_s):
    seq = xr_ref.shape[1]
    pad = SUBLANES
    xp[0:pad, :] = jnp.zeros((pad, LANES), F32)
    xp[pad:pad + seq, :] = xr_ref[0]
    lam = lam_ref[...]
    sp = jnp.maximum(-lam, 0.0) + jnp.log(1.0 + jnp.exp(-jnp.abs(lam)))
    w_hi = w_ref[0].astype(BF16)
    w_lo = (w_ref[0] - w_hi.astype(F32)).astype(BF16)
    rows = 256

    def gates(c, carry):
        base = pl.multiple_of(c * rows, rows)
        xc = cb_ref[...] + cw_ref[0:1, :] * xp[pl.ds(base + pad - 3, rows), :]
        for i in range(1, CONV_WIDTH):
            xc = xc + cw_ref[i:i + 1, :] * xp[pl.ds(base + pad - 3 + i, rows), :]
        hi = xc.astype(BF16)
        lo = (xc - hi.astype(F32)).astype(BF16)
        pre = (jnp.dot(hi, w_hi, preferred_element_type=F32)
               + jnp.dot(lo, w_hi, preferred_element_type=F32)
               + jnp.dot(hi, w_lo, preferred_element_type=F32))
        r = jax.nn.sigmoid(pre[:, :LANES] + ba_ref[...])
        ig = jax.nn.sigmoid(pre[:, LANES:] + bx_ref[...])
        log_a = -LRU_C * r * sp
        a = jnp.exp(log_a)
        t = jnp.tanh(log_a)
        b = jnp.sqrt(-2.0 * t / (1.0 - t)) * (ig * xc)
        a_s[pl.ds(base, rows), :] = a
        b_s[pl.ds(base, rows), :] = b
        return carry

    lax.fori_loop(0, seq // rows, gates, 0)

    row = lax.broadcasted_iota(jnp.int32, (SUBLANES, LANES), 0)

    def scan(c, h_prev):
        sl = pl.ds(pl.multiple_of(c * SUBLANES, SUBLANES), SUBLANES)
        a = a_s[sl, :]
        b = b_s[sl, :]
        for s in (1, 2, 4):
            keep = row >= s
            a_sh = jnp.where(keep, pltpu.roll(a, s, axis=0), 1.0)
            b_sh = jnp.where(keep, pltpu.roll(b, s, axis=0), 0.0)
            b = a * b_sh + b
            a = a * a_sh
        h = a * h_prev + b
        o_ref[0, sl, :] = h * jax.nn.gelu(gr_ref[0, sl, :])
        return jnp.broadcast_to(h[SUBLANES - 1:SUBLANES, :], (SUBLANES, LANES))

    lax.fori_loop(0, seq // SUBLANES, scan, jnp.zeros((SUBLANES, LANES), F32), unroll=8)


def _rglru(proj3, conv_w, conv_b, w_bd, b_a, b_x, lam):
    b, seq, _ = proj3.shape
    nt = LRU_WIDTH // LANES
    xr0 = 3 * ATTN_WIDTH // LANES
    gr0 = xr0 + nt
    blk = (1, seq, LANES)
    vec = lambda: pl.BlockSpec((1, LANES), lambda i, j: (0, j))
    return pl.pallas_call(
        _rglru_body,
        grid=(b, nt),
        in_specs=[
            pl.BlockSpec(blk, lambda i, j: (i, 0, xr0 + j)),
            pl.BlockSpec(blk, lambda i, j: (i, 0, gr0 + j)),
            pl.BlockSpec((CONV_WIDTH, LANES), lambda i, j: (0, j)),
            vec(),
            pl.BlockSpec((1, LANES, 2 * LANES), lambda i, j: (j, 0, 0)),
            vec(), vec(), vec(),
        ],
        out_specs=pl.BlockSpec(blk, lambda i, j: (i, 0, j)),
        out_shape=jax.ShapeDtypeStruct((b, seq, LRU_WIDTH), F32),
        scratch_shapes=[
            pltpu.VMEM((seq + SUBLANES, LANES), F32),
            pltpu.VMEM((seq, LANES), F32),
            pltpu.VMEM((seq, LANES), F32),
        ],
        compiler_params=_params(("parallel", "parallel")),
        name="rglru",
    )(proj3, proj3, conv_w, conv_b.reshape(1, -1), w_bd, b_a.reshape(1, -1),
      b_x.reshape(1, -1), lam.reshape(1, -1))


def _out_proj_body(at_ref, rc_ref, ga_ref, gr_ref, x_ref, w_ref, o_ref, h_ref):
    wa = at_ref.shape[1]

    @pl.when(pl.program_id(1) == 0)
    def _():
        h_ref[:, :wa] = _rms(at_ref[...], ga_ref[...]).astype(BF16)
        h_ref[:, wa:] = _rms(rc_ref[...], gr_ref[...]).astype(BF16)

    o_ref[...] = x_ref[...] + jnp.dot(h_ref[...], w_ref[...].astype(BF16),
                                      preferred_element_type=F32)


def _out_proj(attn2, rec2, g_attn, g_rec, x2, w_out, tm=1024, tn=512):
    n, d = x2.shape
    wa, wr = attn2.shape[1], rec2.shape[1]
    return pl.pallas_call(
        _out_proj_body,
        grid=(n // tm, d // tn),
        in_specs=[
            pl.BlockSpec((tm, wa), lambda i, j: (i, 0)),
            pl.BlockSpec((tm, wr), lambda i, j: (i, 0)),
            pl.BlockSpec((1, wa), lambda i, j: (0, 0)),
            pl.BlockSpec((1, wr), lambda i, j: (0, 0)),
            pl.BlockSpec((tm, tn), lambda i, j: (i, j)),
            pl.BlockSpec((d, tn), lambda i, j: (0, j)),
        ],
        out_specs=pl.BlockSpec((tm, tn), lambda i, j: (i, j)),
        out_shape=jax.ShapeDtypeStruct((n, d), F32),
        scratch_shapes=[pltpu.VMEM((tm, d), BF16)],
        compiler_params=_params(("parallel", "arbitrary")),
        name="out_proj",
    )(attn2, rec2, g_attn.reshape(1, wa), g_rec.reshape(1, wr), x2, w_out)


def _router_body(x_ref, g_ref, wt_ref, b_ref, xn_ref, idx_ref, gate_ref, rank_ref, cnt_ref,
                 base):
    tm = x_ref.shape[0]

    @pl.when(pl.program_id(0) == 0)
    def _():
        base[...] = jnp.zeros_like(base)

    xn = _rms(x_ref[...], g_ref[...])
    xn_ref[...] = _pack_halves(xn)
    x_hi = xn.astype(BF16)
    x_lo = (xn - x_hi.astype(F32)).astype(BF16)
    w = wt_ref[...]
    w_hi = w.astype(BF16)
    w_lo = (w - w_hi.astype(F32)).astype(BF16)
    nt = (((1,), (1,)), ((), ()))
    logits = (lax.dot_general(w_hi, x_hi, nt, preferred_element_type=F32)
              + lax.dot_general(w_hi, x_lo, nt, preferred_element_type=F32)
              + lax.dot_general(w_lo, x_hi, nt, preferred_element_type=F32)
              + b_ref[...])

    eid = lax.broadcasted_iota(jnp.int32, (N_EXPERTS, tm), 0)
    work = logits
    vals, hots = [], []
    for _ in range(TOP_K):
        best = jnp.max(work, axis=0, keepdims=True)
        pick = jnp.min(jnp.where(work == best, eid, N_EXPERTS), axis=0, keepdims=True)
        hot = eid == pick
        vals.append(best)
        hots.append(hot)
        work = jnp.where(hot, -jnp.inf, work)
        idx_ref[len(vals) - 1:len(vals), :] = pick

    ex = [jnp.exp(v - vals[0]) for v in vals]
    den = ex[0] + ex[1] + ex[2] + ex[3]
    for k in range(TOP_K):
        gate_ref[k:k + 1, :] = ex[k] / den

    chosen = (hots[0] | hots[1] | hots[2] | hots[3])
    si = lax.broadcasted_iota(jnp.int32, (tm, tm), 0)
    ti = lax.broadcasted_iota(jnp.int32, (tm, tm), 1)
    before = (si < ti).astype(BF16)
    prefix = jnp.dot(chosen.astype(BF16), before, preferred_element_type=F32)
    slot = base[:, 0:1] + prefix
    for k in range(TOP_K):
        rank_ref[k:k + 1, :] = jnp.sum(jnp.where(hots[k], slot, 0.0), axis=0,
                                       keepdims=True).astype(jnp.int32)
    base[...] = base[...] + jnp.sum(chosen.astype(F32), axis=1, keepdims=True)
    cnt_ref[...] = base[...].astype(jnp.int32)


def _router(x2, g, w_router_t, b_router, tm=256):
    n, d = x2.shape
    row = lambda: pl.BlockSpec((TOP_K, tm), lambda i: (0, i))
    return pl.pallas_call(
        _router_body,
        grid=(n // tm,),
        in_specs=[
            pl.BlockSpec((tm, d), lambda i: (i, 0)),
            pl.BlockSpec((1, d), lambda i: (0, 0)),
            pl.BlockSpec((N_EXPERTS, d), lambda i: (0, 0)),
            pl.BlockSpec((N_EXPERTS, 1), lambda i: (0, 0)),
        ],
        out_specs=[
            pl.BlockSpec((tm, d // 2), lambda i: (i, 0)),
            row(), row(), row(),
            pl.BlockSpec((N_EXPERTS, LANES), lambda i: (0, 0)),
        ],
        out_shape=[
            jax.ShapeDtypeStruct((n, d // 2), U32),
            jax.ShapeDtypeStruct((TOP_K, n), jnp.int32),
            jax.ShapeDtypeStruct((TOP_K, n), F32),
            jax.ShapeDtypeStruct((TOP_K, n), jnp.int32),
            jax.ShapeDtypeStruct((N_EXPERTS, LANES), jnp.int32),
        ],
        scratch_shapes=[pltpu.VMEM((N_EXPERTS, LANES), F32)],
        compiler_params=_params(("arbitrary",)),
        name="router",
    )(x2, g.reshape(1, d), w_router_t, b_router.reshape(N_EXPERTS, 1))


def _scatter_body(dest_ref, pad_ref, xn_ref, o_ref, zrow, sem, *, n_tokens, pad_per_step):
    tm = xn_ref.shape[0]
    i = pl.program_id(0)

    @pl.when(i == 0)
    def _():
        zrow[...] = jnp.zeros_like(zrow)

    def row_copy(t, k):
        d = dest_ref[k * n_tokens + i * tm + t]
        return pltpu.make_async_copy(xn_ref.at[pl.ds(t, 1), :], o_ref.at[pl.ds(d, 1), :], sem)

    def pad_copy(q):
        d = pad_ref[i * pad_per_step + q]
        return pltpu.make_async_copy(zrow.at[pl.ds(0, 1), :], o_ref.at[pl.ds(d, 1), :], sem)

    def issue(t, c):
        for k in range(TOP_K):
            row_copy(t, k).start(priority=k % 2)
        return c

    lax.fori_loop(0, tm, issue, 0)

    def issue_pad(q, c):
        pad_copy(q).start()
        return c

    lax.fori_loop(0, pad_per_step, issue_pad, 0)

    def drain(t, c):
        for k in range(TOP_K):
            row_copy(t, k).wait()
        return c

    lax.fori_loop(0, tm, drain, 0)

    def drain_pad(q, c):
        pad_copy(q).wait()
        return c

    lax.fori_loop(0, pad_per_step, drain_pad, 0)


def _scatter(dest_flat, pad_dest, xn, n_rows, tm=256):
    n, d = xn.shape
    steps = n // tm
    pad_per_step = pad_dest.shape[0] // steps
    return pl.pallas_call(
        functools.partial(_scatter_body, n_tokens=n, pad_per_step=pad_per_step),
        grid_spec=pltpu.PrefetchScalarGridSpec(
            num_scalar_prefetch=2,
            grid=(steps,),
            in_specs=[pl.BlockSpec((tm, d), lambda i, *_: (i, 0))],
            out_specs=pl.BlockSpec(memory_space=pl.ANY),
            scratch_shapes=[pltpu.VMEM((SUBLANES, d), xn.dtype), pltpu.SemaphoreType.DMA(())],
        ),
        out_shape=jax.ShapeDtypeStruct((n_rows, d), xn.dtype),
        compiler_params=_params(("arbitrary",)),
        name="scatter",
    )(dest_flat, pad_dest, xn)


def _moe_body(meta_ref, x_hbm, w_ref, b_ref, o_hbm, xbf, stage, wbf, obuf, sem_in, sem_out,
              pend, *, prep_w, epilogue):
    t, j = pl.program_id(0), pl.program_id(1)
    row0 = meta_ref[META_STRIDE + t]
    nsub = meta_ref[2 * META_STRIDE + t]
    zero = meta_ref[3 * META_STRIDE + t]
    tn_out = obuf.shape[2]
    col0 = pl.multiple_of(j * tn_out, tn_out)

    @pl.when((t == 0) & (j == 0))
    def _():
        for k in range(LAST_SLOT + 1):
            pend[k] = 0

    def in_copy(c, slot):
        src = x_hbm.at[pl.ds(pl.multiple_of(row0 + c * MOE_BLOCK, MOE_BLOCK), MOE_BLOCK), :]
        if stage is None:
            dst = xbf.at[pl.ds(pl.multiple_of(c * MOE_BLOCK, MOE_BLOCK), MOE_BLOCK), :]
        else:
            dst = stage.at[slot]
        return pltpu.make_async_copy(src, dst, sem_in.at[slot])

    def fetch_rows(c):
        slot = c % 2

        @pl.when(c + 1 < nsub)
        def _():
            in_copy(c + 1, 1 - slot).start()

        in_copy(c, slot).wait()
        if stage is not None:
            rows = pl.ds(pl.multiple_of(c * MOE_BLOCK, MOE_BLOCK), MOE_BLOCK)
            lo, hi = _unpack_halves(stage[slot])
            xbf[rows, :lo.shape[1]] = lo.astype(BF16)
            xbf[rows, lo.shape[1]:] = hi.astype(BF16)

    def out_copy(first_block, slot, n_blocks):
        n_rows = n_blocks * MOE_BLOCK
        rows = pl.ds(pl.multiple_of(row0 + first_block * MOE_BLOCK, MOE_BLOCK), n_rows)
        return pltpu.make_async_copy(obuf.at[slot, pl.ds(0, n_rows), :],
                                     o_hbm.at[rows, pl.ds(col0, tn_out)], sem_out.at[slot])

    def wait_slot(slot, sizes):
        for m in sizes:
            @pl.when(pend[slot] == m)
            def _():
                out_copy(0, slot, m).wait()

    def emit(first_block, slot, n_blocks, value_fn):
        last = isinstance(slot, int)
        wait_slot(slot, range(1, MAX_LAST + 1) if last else (RUN_BLOCKS,))
        obuf[slot, pl.ds(0, n_blocks * MOE_BLOCK), :] = value_fn()
        out_copy(first_block, slot, n_blocks).start()
        pend[slot] = n_blocks

    def walk(block_fn):
        n_full = jnp.maximum(nsub // RUN_BLOCKS - 1, 0)

        def full(p, carry):
            block_fn(p * RUN_BLOCKS, p % 2, RUN_BLOCKS)
            return carry

        lax.fori_loop(0, n_full, full, 0)
        for n_blocks in range(1, MAX_LAST + 1):
            @pl.when(nsub - n_full * RUN_BLOCKS == n_blocks)
            def _():
                block_fn(n_full * RUN_BLOCKS, LAST_SLOT, n_blocks)

    @pl.when((nsub > 0) & (zero == 0))
    def _compute():
        @pl.when(j == 0)
        def _():
            in_copy(0, 0).start()

        prep_w(w_ref, wbf)

        def block(first_block, slot, n_blocks):
            @pl.when(j == 0)
            def _():
                for k in range(n_blocks):
                    fetch_rows(first_block + k)

            rows = pl.ds(pl.multiple_of(first_block * MOE_BLOCK, MOE_BLOCK), n_blocks * MOE_BLOCK)
            emit(first_block, slot, n_blocks, lambda: epilogue(xbf[rows, :], wbf, b_ref))

        walk(block)

    @pl.when((nsub > 0) & (zero == 1))
    def _zeros():
        walk(lambda first_block, slot, n_blocks: emit(
            first_block, slot, n_blocks,
            lambda: jnp.zeros((n_blocks * MOE_BLOCK, tn_out), obuf.dtype)))

    @pl.when((t == pl.num_programs(0) - 1) & (j == pl.num_programs(1) - 1))
    def _drain():
        for k in range(LAST_SLOT):
            wait_slot(k, (RUN_BLOCKS,))
        wait_slot(LAST_SLOT, range(1, MAX_LAST + 1))


def _moe_call(name, meta, x, w, b, out_cols, out_dtype, tn, tn_out, prep_w, epilogue, tmp_shape):
    p, kdim = x.shape[0], w.shape[1]
    n_items = _max_items(p)
    n_tiles = w.shape[2] // tn
    needs_stage = x.dtype != BF16

    def wmap(t, j, m):
        live = (m[2 * META_STRIDE + t] > 0) & (m[3 * META_STRIDE + t] == 0)
        return (m[t], 0, jnp.where(live, j, n_tiles - 1))

    scratch = [pltpu.VMEM((ITEM_ROWS, kdim), BF16)]
    if needs_stage:
        scratch.append(pltpu.VMEM((2, MOE_BLOCK, x.shape[1]), x.dtype))
    scratch += [
        pltpu.VMEM((kdim, tn), BF16),
        pltpu.VMEM((LAST_SLOT + 1, MAX_LAST * MOE_BLOCK, tn_out), out_dtype),
        pltpu.SemaphoreType.DMA((2,)),
        pltpu.SemaphoreType.DMA((LAST_SLOT + 1,)),
        pltpu.SMEM((LAST_SLOT + 1,), jnp.int32),
    ]
    if tmp_shape is not None:
        scratch.append(pltpu.VMEM(tmp_shape, F32))

    def body(meta_ref, x_hbm, w_ref, b_ref, o_hbm, xbf, *rest):
        rest = list(rest)
        stage = rest.pop(0) if needs_stage else None
        wbf, obuf, sem_in, sem_out, pend = rest[:5]
        tmp = rest[5] if tmp_shape is not None else None
        _moe_body(meta_ref, x_hbm, w_ref, b_ref, o_hbm, xbf, stage, wbf, obuf, sem_in, sem_out,
                  pend, prep_w=functools.partial(prep_w, tmp=tmp), epilogue=epilogue)

    return pl.pallas_call(
        body,
        grid_spec=pltpu.PrefetchScalarGridSpec(
            num_scalar_prefetch=1,
            grid=(n_items, n_tiles),
            in_specs=[
                pl.BlockSpec(memory_space=pl.ANY),
                pl.BlockSpec((1, kdim, tn), wmap),
                pl.BlockSpec((1, 1, tn), wmap),
            ],
            out_specs=pl.BlockSpec(memory_space=pl.ANY),
            scratch_shapes=scratch,
        ),
        out_shape=jax.ShapeDtypeStruct((p, out_cols), out_dtype),
        compiler_params=_params(("arbitrary", "arbitrary")),
        name=name,
    )(meta, x, w, b.reshape(N_EXPERTS, 1, -1))


def _cast_weights(w_ref, wbf, tmp=None, rows=256):
    def cast(c, carry):
        sl = pl.ds(pl.multiple_of(c * rows, rows), rows)
        wbf[sl, :] = w_ref[0, sl, :].astype(BF16)
        return carry

    lax.fori_loop(0, wbf.shape[0] // rows, cast, 0)


def _swiglu_tile(x, wbf, b_ref):
    even = lax.broadcasted_iota(jnp.int32, (x.shape[0], LANES), 1) % 2 == 0
    cols = []
    for c in range(wbf.shape[1] // (2 * LANES)):
        c0 = c * 2 * LANES
        gu = jnp.dot(x, wbf[:, c0:c0 + 2 * LANES], preferred_element_type=F32)
        gu = gu + b_ref[0, :, c0:c0 + 2 * LANES]
        v1, v2 = gu[:, :LANES], gu[:, LANES:]
        gate = jnp.where(even, v1, pltpu.roll(v2, 1, axis=1))
        up = jnp.where(even, pltpu.roll(v1, LANES - 1, axis=1), v2)
        gate = jnp.minimum(gate, SWIGLU_LIMIT)
        up = jnp.clip(up, -SWIGLU_LIMIT, SWIGLU_LIMIT)
        cols.append((gate * jax.nn.sigmoid(SWIGLU_ALPHA * gate) * (up + 1.0)).astype(BF16))
    return jnp.concatenate(cols, axis=1)


def _gate_up(meta, x_sorted, w_gate_up, b_gate_up, tn=1024):
    return _moe_call("gate_up", meta, x_sorted, w_gate_up, b_gate_up, w_gate_up.shape[2] // 2,
                     BF16, tn, tn // 2, _cast_weights, _swiglu_tile, None)


def _permute_weights(w_ref, wbf, tmp):
    half = LANES // 2

    def permute(g, carry):
        base = pl.multiple_of(g * LANES, LANES)
        for c in range(tmp.shape[0]):
            cs = slice(c * LANES, (c + 1) * LANES)
            tmp[c, pl.ds(0, half, stride=2), :] = w_ref[0, pl.ds(base, half), cs]
            tmp[c, pl.ds(1, half, stride=2), :] = w_ref[0, pl.ds(base + half, half), cs]
            wbf[pl.ds(base, LANES), cs] = tmp[c].astype(BF16)
        return carry

    lax.fori_loop(0, wbf.shape[0] // LANES, permute, 0)


def _linear_tile(a, wbf, b_ref):
    return _pack_halves(jnp.dot(a, wbf[...], preferred_element_type=F32) + b_ref[0])


DOWN_TILE = 1024


def _down(meta, act, w_down, b_down, tn=DOWN_TILE):
    return _moe_call("down", meta, act, w_down, b_down, w_down.shape[2] // 2, U32, tn, tn // 2,
                     _permute_weights, _linear_tile, (tn // LANES, LANES, LANES))


def _combine_body(dest_ref, y_ref, x_ref, gate_ref, g_ref, o_ref, ybuf, sem, *, n_tokens,
                  y_tiles):
    tm = x_ref.shape[0]
    i = pl.program_id(0)

    def row_copy(t, k):
        d = dest_ref[k * n_tokens + i * tm + t]
        return pltpu.make_async_copy(y_ref.at[pl.ds(d, 1), :], ybuf.at[k, pl.ds(t, 1), :], sem)

    def issue(t, c):
        for k in range(TOP_K):
            row_copy(t, k).start(priority=k % 2)
        return c

    lax.fori_loop(0, tm, issue, 0)

    def drain(t, c):
        for k in range(TOP_K):
            row_copy(t, k).wait()
        return c

    lax.fori_loop(0, tm, drain, 0)

    acc = x_ref[...]
    tile = x_ref.shape[1] // y_tiles
    for k in range(TOP_K):
        parts = []
        for c in range(y_tiles):
            parts += _unpack_halves(ybuf[k, :, c * tile // 2:(c + 1) * tile // 2])
        acc = acc + gate_ref[:, k:k + 1] * jnp.concatenate(parts, axis=1)
    o_ref[...] = _rms(acc, g_ref[...])


def _combine(dest_flat, y_buf, x2, gates, g, y_tiles, tm=256):
    n, d = x2.shape
    return pl.pallas_call(
        functools.partial(_combine_body, n_tokens=n, y_tiles=y_tiles),
        grid_spec=pltpu.PrefetchScalarGridSpec(
            num_scalar_prefetch=1,
            grid=(n // tm,),
            in_specs=[
                pl.BlockSpec(memory_space=pl.ANY),
                pl.BlockSpec((tm, d), lambda i, *_: (i, 0)),
                pl.BlockSpec((tm, TOP_K), lambda i, *_: (i, 0)),
                pl.BlockSpec((1, d), lambda i, *_: (0, 0)),
            ],
            out_specs=pl.BlockSpec((tm, d), lambda i, *_: (i, 0)),
            scratch_shapes=[pltpu.VMEM((TOP_K, tm) + y_buf.shape[1:], y_buf.dtype),
                            pltpu.SemaphoreType.DMA(())],
        ),
        out_shape=jax.ShapeDtypeStruct((n, d), F32),
        compiler_params=_params(("arbitrary",)),
        name="combine",
    )(dest_flat, y_buf, x2, gates, g.reshape(1, d))


def _cumsum_sublanes(x):
    row = lax.broadcasted_iota(jnp.int32, x.shape, 0)
    s = 1
    while s < x.shape[0]:
        x = x + jnp.where(row >= s, pltpu.roll(x, s, axis=0), 0)
        s *= 2
    return x


def _tables_body(cnt_ref, idx_ref, rank_ref, dest_ref, meta_ref, pad_ref, *, n_rows):
    cnt = cnt_ref[...]
    padded = (cnt + (MOE_BLOCK - 1)) // MOE_BLOCK * MOE_BLOCK
    pend = _cumsum_sublanes(padded)
    pstart = pend - padded
    total = pend[N_EXPERTS - 1:N_EXPERTS, 0:1]

    def lookup(table, sel):
        eid = lax.broadcasted_iota(jnp.int32, (N_EXPERTS, sel.shape[1]), 0)
        return jnp.sum(jnp.where(eid == sel, table[:, 0:1], 0), axis=0, keepdims=True)

    def count_le(table, v):
        return jnp.sum((table[:, 0:1] <= v).astype(jnp.int32), axis=0, keepdims=True)

    width = 2048
    for k in range(TOP_K):
        for c in range(idx_ref.shape[1] // width):
            sl = slice(c * width, (c + 1) * width)
            dest_ref[k:k + 1, sl] = lookup(pstart, idx_ref[k:k + 1, sl]) + rank_ref[k:k + 1, sl]

    n_it = (padded + (ITEM_ROWS - 1)) // ITEM_ROWS
    it_end = _cumsum_sublanes(n_it)
    t = lax.broadcasted_iota(jnp.int32, (1, META_STRIDE), 1)
    e = count_le(it_end, t)
    ec = jnp.minimum(e, N_EXPERTS - 1)
    k = t - lookup(it_end - n_it, ec)
    rows_real = jnp.minimum(lookup(padded, ec) - k * ITEM_ROWS, ITEM_ROWS)
    row0_zero = total + (t - it_end[N_EXPERTS - 1:N_EXPERTS, 0:1]) * ITEM_ROWS
    rows_zero = jnp.clip(n_rows - row0_zero, 0, ITEM_ROWS)
    real = e < N_EXPERTS
    meta_ref[:, 0:META_STRIDE] = ec
    meta_ref[:, META_STRIDE:2 * META_STRIDE] = jnp.where(
        real, lookup(pstart, ec) + k * ITEM_ROWS, jnp.minimum(row0_zero, n_rows - MOE_BLOCK))
    meta_ref[:, 2 * META_STRIDE:3 * META_STRIDE] = jnp.where(real, rows_real, rows_zero) // MOE_BLOCK
    meta_ref[:, 3 * META_STRIDE:4 * META_STRIDE] = jnp.where(real, 0, 1)

    gap = padded - cnt
    gap_end = _cumsum_sublanes(gap)
    q = lax.broadcasted_iota(jnp.int32, pad_ref.shape, 1)
    eq = count_le(gap_end, q)
    inside = lookup(pstart + cnt - (gap_end - gap), eq) + q
    tail = total + q - gap_end[N_EXPERTS - 1:N_EXPERTS, 0:1]
    pad_ref[...] = jnp.where(eq < N_EXPERTS, inside, tail)


def _max_items(n_rows):
    return N_EXPERTS + n_rows // ITEM_ROWS + (N_EXPERTS * MOE_BLOCK) // ITEM_ROWS + 1


def _tables(cnt, idx_t, rank_t, n_rows):
    n_pad = n_rows - idx_t.size
    assert _max_items(n_rows) <= META_STRIDE
    dest, meta, pad_dest = pl.pallas_call(
        functools.partial(_tables_body, n_rows=n_rows),
        out_shape=[
            jax.ShapeDtypeStruct(idx_t.shape, jnp.int32),
            jax.ShapeDtypeStruct((1, 4 * META_STRIDE), jnp.int32),
            jax.ShapeDtypeStruct((1, n_pad), jnp.int32),
        ],
        name="tables",
    )(cnt, idx_t, rank_t)
    return dest.reshape(-1), meta.reshape(-1), pad_dest.reshape(-1)


def kernel(x, norm_mix, w_in, conv_w, conv_b, w_a, b_a, w_x, b_x, lru_lambda, attn_out_norm, lru_out_norm, w_out, norm_ffn, w_router, b_router, w_gate_up, b_gate_up, w_down, b_down, norm_final):
    b, seq, d = x.shape
    n = b * seq
    x2 = x.reshape(n, d)

    proj = _in_proj(x2, norm_mix, w_in)
    proj3 = proj.reshape(b, seq, IN_COLS)

    slopes = jnp.asarray(2.0 ** (-8.0 * np.arange(1, N_HEADS + 1) / N_HEADS), F32)
    attn = _attention(proj3, slopes)

    def pair_blocks(w):
        w4 = w.reshape(-1, 2, LRU_BLOCK, LRU_BLOCK)
        z = jnp.zeros_like(w4[:, 0])
        top = jnp.concatenate([w4[:, 0], z], axis=2)
        bot = jnp.concatenate([z, w4[:, 1]], axis=2)
        return jnp.concatenate([top, bot], axis=1)

    w_bd = jnp.concatenate([pair_blocks(w_a), pair_blocks(w_x)], axis=2)
    rec = _rglru(proj3, conv_w, conv_b, w_bd, b_a, b_x, lru_lambda)

    x_mid = _out_proj(attn.reshape(n, ATTN_WIDTH), rec.reshape(n, LRU_WIDTH),
                      attn_out_norm, lru_out_norm, x2, w_out)

    xn, idx_t, gate_t, rank_t, cnt = _router(x_mid, norm_ffn, w_router.T, b_router)
    n_rows = n * TOP_K + N_EXPERTS * MOE_BLOCK
    dest_flat, meta, pad_dest = _tables(cnt, idx_t, rank_t, n_rows)

    x_sorted = _scatter(dest_flat, pad_dest, xn, n_rows)
    act = _gate_up(meta, x_sorted, w_gate_up, b_gate_up)
    y_buf = _down(meta, act, w_down, b_down)
    out = _combine(dest_flat, y_buf, x_mid, gate_t.T, norm_final, d // DOWN_TILE)
    return out.reshape(b, seq, d)
```

```python
import functools

import numpy as np
import jax
import jax.numpy as jnp
from jax import lax
from jax.experimental import pallas as pl
from jax.experimental.pallas import tpu as pltpu

F32 = jnp.float32
BF16 = jnp.bfloat16
U32 = jnp.uint32
HIGH_HALF = np.uint32(0xFFFF0000)

D_MODEL = 2048
HEAD_DIM = 64
N_HEADS = 16
ATTN_WIDTH = N_HEADS * HEAD_DIM
LRU_WIDTH = D_MODEL - ATTN_WIDTH
LRU_BLOCK = 64
CONV_WIDTH = 4
LRU_C = 8.0
IN_COLS = 3 * ATTN_WIDTH + 2 * LRU_WIDTH
DILATIONS = (1, 4, 16)
ATTN_BLOCK = 128
UNITS_PER_TRIP = (5, 6, 8)
N_EXPERTS = 32
TOP_K = 4
D_FF = D_MODEL
SWIGLU_LIMIT = 7.0
SWIGLU_ALPHA = 1.702
MOE_BLOCK = 128
ITEM_ROWS = 16 * MOE_BLOCK
META_STRIDE = 128
RUN_BLOCKS = 4
LAST_SLOT = 2
MAX_LAST = 2 * RUN_BLOCKS - 1
W_SLABS = 4
EPS = 1e-6

LANES = 128
SUBLANES = 8
VMEM_LIMIT = 56 * 1024 * 1024


def _params(sem, vmem=VMEM_LIMIT):
    return pltpu.CompilerParams(dimension_semantics=sem, vmem_limit_bytes=vmem)


def _rms(xf, g):
    return xf * lax.rsqrt(jnp.mean(xf * xf, axis=-1, keepdims=True) + EPS) * g


def _pack_halves(x):
    half = x.shape[1] // 2
    lo = lax.bitcast_convert_type(x[:, :half].astype(BF16).astype(F32), U32) >> 16
    hi = lax.bitcast_convert_type(x[:, half:].astype(BF16).astype(F32), U32) & HIGH_HALF
    return lo | hi


def _unpack_halves(u):
    return (lax.bitcast_convert_type(u << 16, F32),
            lax.bitcast_convert_type(u & HIGH_HALF, F32))


def _in_proj_body(x_ref, g_ref, w_ref, o_ref, h_ref):
    @pl.when(pl.program_id(1) == 0)
    def _():
        h_ref[...] = _rms(x_ref[...], g_ref[...]).astype(BF16)

    o_ref[...] = jnp.dot(h_ref[...], w_ref[...].astype(BF16), preferred_element_type=F32)


def _in_proj(x2, g, w_in, tm=1024, tn=512):
    n, d = x2.shape
    cols = w_in.shape[1]
    return pl.pallas_call(
        _in_proj_body,
        grid=(n // tm, cols // tn),
        in_specs=[
            pl.BlockSpec((tm, d), lambda i, j: (i, 0)),
            pl.BlockSpec((1, d), lambda i, j: (0, 0)),
            pl.BlockSpec((d, tn), lambda i, j: (0, j)),
        ],
        out_specs=pl.BlockSpec((tm, tn), lambda i, j: (i, j)),
        out_shape=jax.ShapeDtypeStruct((n, cols), F32),
        scratch_shapes=[pltpu.VMEM((tm, d), BF16)],
        compiler_params=_params(("parallel", "arbitrary")),
        name="in_proj",
    )(x2, g.reshape(1, d), w_in)


REGROUP = 4


def _attention_body(slope_ref, q_ref, k_ref, v_ref, o_ref, qh, kh, vh, acc, mx):
    seq = q_ref.shape[1]
    run = seq // REGROUP
    pair = pl.program_id(1)
    qi = lax.broadcasted_iota(jnp.int32, (ATTN_BLOCK, 2 * ATTN_BLOCK), 0)
    ki = lax.broadcasted_iota(jnp.int32, (ATTN_BLOCK, 2 * ATTN_BLOCK), 1)
    rel2 = qi + ATTN_BLOCK - ki
    ok2 = (rel2 >= 0) & (rel2 <= ATTN_BLOCK)
    rel1 = (qi - ki)[:, :ATTN_BLOCK]
    ok1 = rel1 >= 0

    nh = LANES // HEAD_DIM
    slopes = [slope_ref[pair * nh + hh] for hh in range(nh)]
    def stage(order, rows_dst, rows_src):
        lane = lax.broadcasted_iota(jnp.int32, (run, LANES), 1)
        qv, vv = q_ref[0, rows_src, :] * (HEAD_DIM ** -0.5), v_ref[0, rows_src, :]
        if order:
            kh[rows_dst, :] = k_ref[0, rows_src, :]
        for hh in range(nh):
            mine = (lane >= hh * HEAD_DIM) & (lane < (hh + 1) * HEAD_DIM)
            qh[order, hh, rows_dst, :] = jnp.where(mine, qv, 0.0)
            vh[order, hh, rows_dst, :] = jnp.where(mine, vv, 1.0)

    for c in range(REGROUP):
        stage(0, slice(c * run, (c + 1) * run), pl.ds(c * run, run))
        stage(1, slice(c * run, (c + 1) * run), pl.ds(c, run, stride=REGROUP))

    def run_units(units):
        loaded = [(qh[order, hh, qs, :].astype(BF16),
                   (kh[ks, :] if order else k_ref[0, ks, :]).astype(BF16),
                   vh[order, hh, ks, :].astype(BF16)) for hh, br, order, qs, ks, bias in units]
        scores = [lax.dot_general(q, k, (((1,), (1,)), ((), ())), preferred_element_type=F32)
                  + u[5] for (q, k, v), u in zip(loaded, units)]
        maxes = [jnp.max(s, axis=1, keepdims=True) for s in scores]
        probs = [jnp.exp(s - m).astype(BF16) for s, m in zip(scores, maxes)]
        results = [(jnp.dot(p, v, preferred_element_type=F32), m)
                   for p, (q, k, v), m in zip(probs, loaded, maxes)]
        for (a, m), (hh, br, order, qs, ks, bias) in zip(results, units):
            acc[hh, br, qs, :] = a
            mx[hh, br, qs, :] = jnp.broadcast_to(m, (ATTN_BLOCK, LANES))

    def bias_pair(ok, rel, d):
        return [jnp.where(ok, -sl * (rel * d).astype(F32), -jnp.inf) for sl in slopes]

    def contiguous_branch(br, order, d, n_runs, per_trip):
        nb = seq // (n_runs * ATTN_BLOCK)
        bias1, bias2 = bias_pair(ok1, rel1, d), bias_pair(ok2, rel2, d)
        run_units([(hh, br, order, pl.ds(r * nb * ATTN_BLOCK, ATTN_BLOCK),
                    pl.ds(r * nb * ATTN_BLOCK, ATTN_BLOCK), bias1[hh])
                   for r in range(n_runs) for hh in range(nh)])

        def later(g, carry):
            units = []
            for u in range(per_trip):
                idx = g * per_trip + u
                blk = idx // (nb - 1) * nb + idx % (nb - 1) + 1
                start = pl.multiple_of(blk * ATTN_BLOCK, ATTN_BLOCK)
                units += [(hh, br, order, pl.ds(start, ATTN_BLOCK),
                           pl.ds(start - ATTN_BLOCK, 2 * ATTN_BLOCK), bias2[hh])
                          for hh in range(nh)]
            run_units(units)
            return carry

        lax.fori_loop(0, n_runs * (nb - 1) // per_trip, later, 0)

    contiguous_branch(0, 0, DILATIONS[0], 1, UNITS_PER_TRIP[0])
    contiguous_branch(1, 1, DILATIONS[1], REGROUP, UNITS_PER_TRIP[1])

    d2 = DILATIONS[2]
    sub = d2 // REGROUP
    bias16 = bias_pair(ok1, rel1, d2)

    def strided(g, carry):
        units = []
        for u in range(UNITS_PER_TRIP[2]):
            idx = g * UNITS_PER_TRIP[2] + u
            rows = pl.ds(idx // sub * run + idx % sub, ATTN_BLOCK, stride=sub)
            units += [(hh, 2, 1, rows, rows, bias16[hh]) for hh in range(nh)]
        run_units(units)
        return carry

    lax.fori_loop(0, d2 // UNITS_PER_TRIP[2], strided, 0)

    def merge(c, carry):
        start = c // (run // ATTN_BLOCK) + c % (run // ATTN_BLOCK) * (ATTN_BLOCK * REGROUP)
        nat = pl.ds(start, ATTN_BLOCK, stride=REGROUP)
        reg = pl.ds(pl.multiple_of(c * ATTN_BLOCK, ATTN_BLOCK), ATTN_BLOCK)
        outs = []
        for hh in range(nh):
            m0, m1, m2 = mx[hh, 0, nat, :], mx[hh, 1, reg, :], mx[hh, 2, reg, :]
            mt = jnp.maximum(jnp.maximum(m0, m1), m2)
            tot = (jnp.exp(m0 - mt) * acc[hh, 0, nat, :] + jnp.exp(m1 - mt) * acc[hh, 1, reg, :]
                   + jnp.exp(m2 - mt) * acc[hh, 2, reg, :])
            den = pltpu.roll(tot, HEAD_DIM, axis=1)
            outs.append(tot / den)
        lane_r = lax.broadcasted_iota(jnp.int32, (ATTN_BLOCK, LANES), 1)
        o_ref[0, nat, :] = jnp.where(lane_r < HEAD_DIM, outs[0], outs[1])
        return carry

    lax.fori_loop(0, seq // ATTN_BLOCK, merge, 0, unroll=2)


def _attention(proj3, slopes):
    b, seq, _ = proj3.shape
    npair = ATTN_WIDTH // LANES
    nh = LANES // HEAD_DIM
    blk = (1, seq, LANES)
    return pl.pallas_call(
        _attention_body,
        grid_spec=pltpu.PrefetchScalarGridSpec(
            num_scalar_prefetch=1,
            grid=(b, npair),
            in_specs=[
                pl.BlockSpec(blk, lambda i, j, s: (i, 0, j)),
                pl.BlockSpec(blk, lambda i, j, s: (i, 0, npair + j)),
                pl.BlockSpec(blk, lambda i, j, s: (i, 0, 2 * npair + j)),
            ],
            out_specs=pl.BlockSpec(blk, lambda i, j, s: (i, 0, j)),
            scratch_shapes=[
                pltpu.VMEM((2, nh, seq, LANES), F32),
                pltpu.VMEM((seq, LANES), F32),
                pltpu.VMEM((2, nh, seq, LANES), F32),
                pltpu.VMEM((nh, len(DILATIONS), seq, LANES), F32),
                pltpu.VMEM((nh, len(DILATIONS), seq, LANES), F32),
            ],
        ),
        out_shape=jax.ShapeDtypeStruct((b, seq, ATTN_WIDTH), F32),
        compiler_params=_params(("parallel", "parallel")),
        name="attention",
    )(slopes, proj3, proj3, proj3)


def _rglru_body(xr_ref, gr_ref, cw_ref, cb_ref, w_ref, ba_ref, bx_ref, lam_ref, o_ref,
                xp, a_s, b_s):
    seq = xr_ref.shape[1]
    pad = SUBLANES
    xp[0:pad, :] = jnp.zeros((pad, LANES), F32)
    xp[pad:pad + seq, :] = xr_ref[0]
    lam = lam_ref[...]
    sp = jnp.maximum(-lam, 0.0) + jnp.log(1.0 + jnp.exp(-jnp.abs(lam)))
    w_hi = w_ref[0].astype(BF16)
    w_lo = (w_ref[0] - w_hi.astype(F32)).astype(BF16)
    rows = 256

    def gates(c, carry):
        base = pl.multiple_of(c * rows, rows)
        xc = cb_ref[...] + cw_ref[0:1, :] * xp[pl.ds(base + pad - 3, rows), :]
        for i in range(1, CONV_WIDTH):
            xc = xc + cw_ref[i:i + 1, :] * xp[pl.ds(base + pad - 3 + i, rows), :]
        hi = xc.astype(BF16)
        lo = (xc - hi.astype(F32)).astype(BF16)
        pre = (jnp.dot(hi, w_hi, preferred_element_type=F32)
               + jnp.dot(lo, w_hi, preferred_element_type=F32)
               + jnp.dot(hi, w_lo, preferred_element_type=F32))
        r = jax.nn.sigmoid(pre[:, :LANES] + ba_ref[...])
        ig = jax.nn.sigmoid(pre[:, LANES:] + bx_ref[...])
        log_a = -LRU_C * r * sp
        a = jnp.exp(log_a)
        t = jnp.tanh(log_a)
        b = jnp.sqrt(-2.0 * t / (1.0 - t)) * (ig * xc)
        a_s[pl.ds(base, rows), :] = a
        b_s[pl.ds(base, rows), :] = b
        return carry

    lax.fori_loop(0, seq // rows, gates, 0)

    row = lax.broadcasted_iota(jnp.int32, (SUBLANES, LANES), 0)

    def scan(c, h_prev):
        sl = pl.ds(pl.multiple_of(c * SUBLANES, SUBLANES), SUBLANES)
        a = a_s[sl, :]
        b = b_s[sl, :]
        for s in (1, 2, 4):
            keep = row >= s
            a_sh = jnp.where(keep, pltpu.roll(a, s, axis=0), 1.0)
            b_sh = jnp.where(keep, pltpu.roll(b, s, axis=0), 0.0)
            b = a * b_sh + b
            a = a * a_sh
        h = a * h_prev + b
        o_ref[0, sl, :] = h * jax.nn.gelu(gr_ref[0, sl, :])
        return jnp.broadcast_to(h[SUBLANES - 1:SUBLANES, :], (SUBLANES, LANES))

    lax.fori_loop(0, seq // SUBLANES, scan, jnp.zeros((SUBLANES, LANES), F32), unroll=8)


def _rglru(proj3, conv_w, conv_b, w_bd, b_a, b_x, lam):
    b, seq, _ = proj3.shape
    nt = LRU_WIDTH // LANES
    xr0 = 3 * ATTN_WIDTH // LANES
    gr0 = xr0 + nt
    blk = (1, seq, LANES)
    vec = lambda: pl.BlockSpec((1, LANES), lambda i, j: (0, j))
    return pl.pallas_call(
        _rglru_body,
        grid=(b, nt),
        in_specs=[
            pl.BlockSpec(blk, lambda i, j: (i, 0, xr0 + j)),
            pl.BlockSpec(blk, lambda i, j: (i, 0, gr0 + j)),
            pl.BlockSpec((CONV_WIDTH, LANES), lambda i, j: (0, j)),
            vec(),
            pl.BlockSpec((1, LANES, 2 * LANES), lambda i, j: (j, 0, 0)),
            vec(), vec(), vec(),
        ],
        out_specs=pl.BlockSpec(blk, lambda i, j: (i, 0, j)),
        out_shape=jax.ShapeDtypeStruct((b, seq, LRU_WIDTH), F32),
        scratch_shapes=[
            pltpu.VMEM((seq + SUBLANES, LANES), F32),
            pltpu.VMEM((seq, LANES), F32),
            pltpu.VMEM((seq, LANES), F32),
        ],
        compiler_params=_params(("parallel", "parallel")),
        name="rglru",
    )(proj3, proj3, conv_w, conv_b.reshape(1, -1), w_bd, b_a.reshape(1, -1),
      b_x.reshape(1, -1), lam.reshape(1, -1))


def _out_proj_body(at_ref, rc_ref, ga_ref, gr_ref, x_ref, w_ref, o_ref, h_ref):
    wa = at_ref.shape[1]

    @pl.when(pl.program_id(1) == 0)
    def _():
        h_ref[:, :wa] = _rms(at_ref[...], ga_ref[...]).astype(BF16)
        h_ref[:, wa:] = _rms(rc_ref[...], gr_ref[...]).astype(BF16)

    o_ref[...] = x_ref[...] + jnp.dot(h_ref[...], w_ref[...].astype(BF16),
                                      preferred_element_type=F32)


def _out_proj(attn2, rec2, g_attn, g_rec, x2, w_out, tm=1024, tn=512):
    n, d = x2.shape
    wa, wr = attn2.shape[1], rec2.shape[1]
    return pl.pallas_call(
        _out_proj_body,
        grid=(n // tm, d // tn),
        in_specs=[
            pl.BlockSpec((tm, wa), lambda i, j: (i, 0)),
            pl.BlockSpec((tm, wr), lambda i, j: (i, 0)),
            pl.BlockSpec((1, wa), lambda i, j: (0, 0)),
            pl.BlockSpec((1, wr), lambda i, j: (0, 0)),
            pl.BlockSpec((tm, tn), lambda i, j: (i, j)),
            pl.BlockSpec((d, tn), lambda i, j: (0, j)),
        ],
        out_specs=pl.BlockSpec((tm, tn), lambda i, j: (i, j)),
        out_shape=jax.ShapeDtypeStruct((n, d), F32),
        scratch_shapes=[pltpu.VMEM((tm, d), BF16)],
        compiler_params=_params(("parallel", "arbitrary")),
        name="out_proj",
    )(attn2, rec2, g_attn.reshape(1, wa), g_rec.reshape(1, wr), x2, w_out)


def _router_body(x_ref, g_ref, wt_ref, b_ref, xn_ref, idx_ref, gate_ref, rank_ref, cnt_ref,
                 base):
    tm = x_ref.shape[0]

    @pl.when(pl.program_id(0) == 0)
    def _():
        base[...] = jnp.zeros_like(base)

    xn = _rms(x_ref[...], g_ref[...])
    xn_ref[...] = _pack_halves(xn)
    x_hi = xn.astype(BF16)
    x_lo = (xn - x_hi.astype(F32)).astype(BF16)
    w = wt_ref[...]
    w_hi = w.astype(BF16)
    w_lo = (w - w_hi.astype(F32)).astype(BF16)
    nt = (((1,), (1,)), ((), ()))
    logits = (lax.dot_general(w_hi, x_hi, nt, preferred_element_type=F32)
              + lax.dot_general(w_hi, x_lo, nt, preferred_element_type=F32)
              + lax.dot_general(w_lo, x_hi, nt, preferred_element_type=F32)
              + b_ref[...])

    eid = lax.broadcasted_iota(jnp.int32, (N_EXPERTS, tm), 0)
    work = logits
    vals, hots = [], []
    for _ in range(TOP_K):
        best = jnp.max(work, axis=0, keepdims=True)
        pick = jnp.min(jnp.where(work == best, eid, N_EXPERTS), axis=0, keepdims=True)
        hot = eid == pick
        vals.append(best)
        hots.append(hot)
        work = jnp.where(hot, -jnp.inf, work)
        idx_ref[len(vals) - 1:len(vals), :] = pick

    ex = [jnp.exp(v - vals[0]) for v in vals]
    den = ex[0] + ex[1] + ex[2] + ex[3]
    for k in range(TOP_K):
        gate_ref[k:k + 1, :] = ex[k] / den

    chosen = (hots[0] | hots[1] | hots[2] | hots[3])
    si = lax.broadcasted_iota(jnp.int32, (tm, tm), 0)
    ti = lax.broadcasted_iota(jnp.int32, (tm, tm), 1)
    before = (si < ti).astype(BF16)
    prefix = jnp.dot(chosen.astype(BF16), before, preferred_element_type=F32)
    slot = base[:, 0:1] + prefix
    for k in range(TOP_K):
        rank_ref[k:k + 1, :] = jnp.sum(jnp.where(hots[k], slot, 0.0), axis=0,
                                       keepdims=True).astype(jnp.int32)
    base[...] = base[...] + jnp.sum(chosen.astype(F32), axis=1, keepdims=True)
    cnt_ref[...] = base[...].astype(jnp.int32)


def _router(x2, g, w_router_t, b_router, tm=256):
    n, d = x2.shape
    row = lambda: pl.BlockSpec((TOP_K, tm), lambda i: (0, i))
    return pl.pallas_call(
        _router_body,
        grid=(n // tm,),
        in_specs=[
            pl.BlockSpec((tm, d), lambda i: (i, 0)),
            pl.BlockSpec((1, d), lambda i: (0, 0)),
            pl.BlockSpec((N_EXPERTS, d), lambda i: (0, 0)),
            pl.BlockSpec((N_EXPERTS, 1), lambda i: (0, 0)),
        ],
        out_specs=[
            pl.BlockSpec((tm, d // 2), lambda i: (i, 0)),
            row(), row(), row(),
            pl.BlockSpec((N_EXPERTS, LANES), lambda i: (0, 0)),
        ],
        out_shape=[
            jax.ShapeDtypeStruct((n, d // 2), U32),
            jax.ShapeDtypeStruct((TOP_K, n), jnp.int32),
            jax.ShapeDtypeStruct((TOP_K, n), F32),
            jax.ShapeDtypeStruct((TOP_K, n), jnp.int32),
            jax.ShapeDtypeStruct((N_EXPERTS, LANES), jnp.int32),
        ],
        scratch_shapes=[pltpu.VMEM((N_EXPERTS, LANES), F32)],
        compiler_params=_params(("arbitrary",)),
        name="router",
    )(x2, g.reshape(1, d), w_router_t, b_router.reshape(N_EXPERTS, 1))


def _scatter_body(dest_ref, pad_ref, xn_ref, o_ref, zrow, sem, *, n_tokens, pad_per_step):
    tm = xn_ref.shape[0]
    i = pl.program_id(0)

    @pl.when(i == 0)
    def _():
        zrow[...] = jnp.zeros_like(zrow)

    def row_copy(t, k):
        d = dest_ref[k * n_tokens + i * tm + t]
        return pltpu.make_async_copy(xn_ref.at[pl.ds(t, 1), :], o_ref.at[pl.ds(d, 1), :], sem)

    def pad_copy(q):
        d = pad_ref[i * pad_per_step + q]
        return pltpu.make_async_copy(zrow.at[pl.ds(0, 1), :], o_ref.at[pl.ds(d, 1), :], sem)

    def issue(t, c):
        for k in range(TOP_K):
            row_copy(t, k).start(priority=k % 2)
        return c

    lax.fori_loop(0, tm, issue, 0)

    def issue_pad(q, c):
        pad_copy(q).start()
        return c

    lax.fori_loop(0, pad_per_step, issue_pad, 0)

    def drain(t, c):
        for k in range(TOP_K):
            row_copy(t, k).wait()
        return c

    lax.fori_loop(0, tm, drain, 0)

    def drain_pad(q, c):
        pad_copy(q).wait()
        return c

    lax.fori_loop(0, pad_per_step, drain_pad, 0)


def _scatter(dest_flat, pad_dest, xn, n_rows, tm=256):
    n, d = xn.shape
    steps = n // tm
    pad_per_step = pad_dest.shape[0] // steps
    return pl.pallas_call(
        functools.partial(_scatter_body, n_tokens=n, pad_per_step=pad_per_step),
        grid_spec=pltpu.PrefetchScalarGridSpec(
            num_scalar_prefetch=2,
            grid=(steps,),
            in_specs=[pl.BlockSpec((tm, d), lambda i, *_: (i, 0))],
            out_specs=pl.BlockSpec(memory_space=pl.ANY),
            scratch_shapes=[pltpu.VMEM((SUBLANES, d), xn.dtype), pltpu.SemaphoreType.DMA(())],
        ),
        out_shape=jax.ShapeDtypeStruct((n_rows, d), xn.dtype),
        compiler_params=_params(("arbitrary",)),
        name="scatter",
    )(dest_flat, pad_dest, xn)


def _moe_body(meta_ref, x_hbm, w_refs, b_ref, o_hbm, xbf, stage, wbf, obuf, sem_in, sem_out,
              pend, *, prep_w, epilogue):
    t, j = pl.program_id(0), pl.program_id(1)
    row0 = meta_ref[META_STRIDE + t]
    nsub = meta_ref[2 * META_STRIDE + t]
    zero = meta_ref[3 * META_STRIDE + t]
    tn_out = obuf.shape[2]
    col0 = pl.multiple_of(j * tn_out, tn_out)

    @pl.when((t == 0) & (j == 0))
    def _():
        for k in range(LAST_SLOT + 1):
            pend[k] = 0

    def in_copy(c, slot):
        src = x_hbm.at[pl.ds(pl.multiple_of(row0 + c * MOE_BLOCK, MOE_BLOCK), MOE_BLOCK), :]
        if stage is None:
            dst = xbf.at[pl.ds(pl.multiple_of(c * MOE_BLOCK, MOE_BLOCK), MOE_BLOCK), :]
        else:
            dst = stage.at[slot]
        return pltpu.make_async_copy(src, dst, sem_in.at[slot])

    def fetch_rows(c):
        slot = c % 2

        @pl.when(c + 1 < nsub)
        def _():
            in_copy(c + 1, 1 - slot).start()

        in_copy(c, slot).wait()
        if stage is not None:
            rows = pl.ds(pl.multiple_of(c * MOE_BLOCK, MOE_BLOCK), MOE_BLOCK)
            lo, hi = _unpack_halves(stage[slot])
            xbf[rows, :lo.shape[1]] = lo.astype(BF16)
            xbf[rows, lo.shape[1]:] = hi.astype(BF16)

    def out_copy(first_block, slot, n_blocks):
        n_rows = n_blocks * MOE_BLOCK
        rows = pl.ds(pl.multiple_of(row0 + first_block * MOE_BLOCK, MOE_BLOCK), n_rows)
        return pltpu.make_async_copy(obuf.at[slot, pl.ds(0, n_rows), :],
                                     o_hbm.at[rows, pl.ds(col0, tn_out)], sem_out.at[slot])

    def wait_slot(slot, sizes):
        for m in sizes:
            @pl.when(pend[slot] == m)
            def _():
                out_copy(0, slot, m).wait()

    def emit(first_block, slot, n_blocks, value_fn):
        last = isinstance(slot, int)
        wait_slot(slot, range(1, MAX_LAST + 1) if last else (RUN_BLOCKS,))
        obuf[slot, pl.ds(0, n_blocks * MOE_BLOCK), :] = value_fn()
        out_copy(first_block, slot, n_blocks).start()
        pend[slot] = n_blocks

    def walk(block_fn):
        n_full = jnp.maximum(nsub // RUN_BLOCKS - 1, 0)

        def full(p, carry):
            block_fn(p * RUN_BLOCKS, p % 2, RUN_BLOCKS)
            return carry

        lax.fori_loop(0, n_full, full, 0)
        for n_blocks in range(1, MAX_LAST + 1):
            @pl.when(nsub - n_full * RUN_BLOCKS == n_blocks)
            def _():
                block_fn(n_full * RUN_BLOCKS, LAST_SLOT, n_blocks)

    @pl.when((nsub > 0) & (zero == 0))
    def _compute():
        @pl.when(j == 0)
        def _():
            in_copy(0, 0).start()

        prep_w(w_refs, wbf)

        def block(first_block, slot, n_blocks):
            @pl.when(j == 0)
            def _():
                for k in range(n_blocks):
                    fetch_rows(first_block + k)

            rows = pl.ds(pl.multiple_of(first_block * MOE_BLOCK, MOE_BLOCK), n_blocks * MOE_BLOCK)
            emit(first_block, slot, n_blocks, lambda: epilogue(xbf[rows, :], wbf, b_ref))

        walk(block)

    @pl.when((nsub > 0) & (zero == 1))
    def _zeros():
        walk(lambda first_block, slot, n_blocks: emit(
            first_block, slot, n_blocks,
            lambda: jnp.zeros((n_blocks * MOE_BLOCK, tn_out), obuf.dtype)))

    @pl.when((t == pl.num_programs(0) - 1) & (j == pl.num_programs(1) - 1))
    def _drain():
        for k in range(LAST_SLOT):
            wait_slot(k, (RUN_BLOCKS,))
        wait_slot(LAST_SLOT, range(1, MAX_LAST + 1))


def _moe_call(name, meta, x, w, b, out_cols, out_dtype, tn, tn_out, prep_w, epilogue, tmp_shape):
    p, kdim = x.shape[0], w.shape[1]
    n_items = _max_items(p)
    n_tiles = w.shape[2] // tn
    needs_stage = x.dtype != BF16

    def wmap(t, j, m, slab=0):
        live = (m[2 * META_STRIDE + t] > 0) & (m[3 * META_STRIDE + t] == 0)
        return (m[t], slab, jnp.where(live, j, n_tiles - 1))

    scratch = [pltpu.VMEM((ITEM_ROWS, kdim), BF16)]
    if needs_stage:
        scratch.append(pltpu.VMEM((2, MOE_BLOCK, x.shape[1]), x.dtype))
    scratch += [
        pltpu.VMEM((kdim, tn), BF16),
        pltpu.VMEM((LAST_SLOT + 1, MAX_LAST * MOE_BLOCK, tn_out), out_dtype),
        pltpu.SemaphoreType.DMA((2,)),
        pltpu.SemaphoreType.DMA((LAST_SLOT + 1,)),
        pltpu.SMEM((LAST_SLOT + 1,), jnp.int32),
    ]
    if tmp_shape is not None:
        scratch.append(pltpu.VMEM(tmp_shape, F32))

    def body(meta_ref, x_hbm, *rest):
        rest = list(rest)
        w_refs = [rest.pop(0) for _ in range(W_SLABS)]
        b_ref, o_hbm, xbf = rest.pop(0), rest.pop(0), rest.pop(0)
        stage = rest.pop(0) if needs_stage else None
        wbf, obuf, sem_in, sem_out, pend = rest[:5]
        tmp = rest[5] if tmp_shape is not None else None
        _moe_body(meta_ref, x_hbm, w_refs, b_ref, o_hbm, xbf, stage, wbf, obuf, sem_in, sem_out,
                  pend, prep_w=functools.partial(prep_w, tmp=tmp), epilogue=epilogue)

    return pl.pallas_call(
        body,
        grid_spec=pltpu.PrefetchScalarGridSpec(
            num_scalar_prefetch=1,
            grid=(n_items, n_tiles),
            in_specs=[
                pl.BlockSpec(memory_space=pl.ANY),
                *[pl.BlockSpec((1, kdim // W_SLABS, tn), functools.partial(wmap, slab=k))
                  for k in range(W_SLABS)],
                pl.BlockSpec((1, 1, tn), wmap),
            ],
            out_specs=pl.BlockSpec(memory_space=pl.ANY),
            scratch_shapes=scratch,
        ),
        out_shape=jax.ShapeDtypeStruct((p, out_cols), out_dtype),
        compiler_params=_params(("arbitrary", "arbitrary")),
        name=name,
    )(meta, x, *([w] * W_SLABS), b.reshape(N_EXPERTS, 1, -1))


def _cast_weights(w_refs, wbf, tmp=None, rows=256):
    slab = wbf.shape[0] // len(w_refs)
    for k, w_ref in enumerate(w_refs):
        def cast(c, carry, k=k, w_ref=w_ref):
            sl = pl.ds(pl.multiple_of(c * rows, rows), rows)
            wbf[pl.ds(pl.multiple_of(k * slab + c * rows, rows), rows), :] = (
                w_ref[0, sl, :].astype(BF16))
            return carry

        lax.fori_loop(0, slab // rows, cast, 0)


def _swiglu_tile(x, wbf, b_ref):
    even = lax.broadcasted_iota(jnp.int32, (x.shape[0], LANES), 1) % 2 == 0
    cols = []
    for c in range(wbf.shape[1] // (2 * LANES)):
        c0 = c * 2 * LANES
        gu = jnp.dot(x, wbf[:, c0:c0 + 2 * LANES], preferred_element_type=F32)
        gu = gu + b_ref[0, :, c0:c0 + 2 * LANES]
        v1, v2 = gu[:, :LANES], gu[:, LANES:]
        gate = jnp.where(even, v1, pltpu.roll(v2, 1, axis=1))
        up = jnp.where(even, pltpu.roll(v1, LANES - 1, axis=1), v2)
        gate = jnp.minimum(gate, SWIGLU_LIMIT)
        up = jnp.clip(up, -SWIGLU_LIMIT, SWIGLU_LIMIT)
        cols.append((gate * jax.nn.sigmoid(SWIGLU_ALPHA * gate) * (up + 1.0)).astype(BF16))
    return jnp.concatenate(cols, axis=1)


def _gate_up(meta, x_sorted, w_gate_up, b_gate_up, tn=1024):
    return _moe_call("gate_up", meta, x_sorted, w_gate_up, b_gate_up, w_gate_up.shape[2] // 2,
                     BF16, tn, tn // 2, _cast_weights, _swiglu_tile, None)


def _permute_weights(w_refs, wbf, tmp):
    half = LANES // 2
    slab = wbf.shape[0] // len(w_refs)
    for k, w_ref in enumerate(w_refs):
        def permute(g, carry, k=k, w_ref=w_ref):
            base = pl.multiple_of(g * LANES, LANES)
            out = pl.ds(pl.multiple_of(k * slab + g * LANES, LANES), LANES)
            for c in range(tmp.shape[0]):
                cs = slice(c * LANES, (c + 1) * LANES)
                tmp[c, pl.ds(0, half, stride=2), :] = w_ref[0, pl.ds(base, half), cs]
                tmp[c, pl.ds(1, half, stride=2), :] = w_ref[0, pl.ds(base + half, half), cs]
                wbf[out, cs] = tmp[c].astype(BF16)
            return carry

        lax.fori_loop(0, slab // LANES, permute, 0)


def _linear_tile(a, wbf, b_ref):
    return _pack_halves(jnp.dot(a, wbf[...], preferred_element_type=F32) + b_ref[0])


DOWN_TILE = 1024


def _down(meta, act, w_down, b_down, tn=DOWN_TILE):
    return _moe_call("down", meta, act, w_down, b_down, w_down.shape[2] // 2, U32, tn, tn // 2,
                     _permute_weights, _linear_tile, (tn // LANES, LANES, LANES))


def _combine_body(dest_ref, y_ref, x_ref, gate_ref, g_ref, o_ref, ybuf, sem, *, n_tokens,
                  y_tiles):
    tm = x_ref.shape[0]
    i = pl.program_id(0)

    def row_copy(t, k):
        d = dest_ref[k * n_tokens + i * tm + t]
        return pltpu.make_async_copy(y_ref.at[pl.ds(d, 1), :], ybuf.at[k, pl.ds(t, 1), :], sem)

    def issue(t, c):
        for k in range(TOP_K):
            row_copy(t, k).start(priority=k % 2)
        return c

    lax.fori_loop(0, tm, issue, 0)

    def drain(t, c):
        for k in range(TOP_K):
            row_copy(t, k).wait()
        return c

    lax.fori_loop(0, tm, drain, 0)

    acc = x_ref[...]
    tile = x_ref.shape[1] // y_tiles
    for k in range(TOP_K):
        parts = []
        for c in range(y_tiles):
            parts += _unpack_halves(ybuf[k, :, c * tile // 2:(c + 1) * tile // 2])
        acc = acc + gate_ref[:, k:k + 1] * jnp.concatenate(parts, axis=1)
    o_ref[...] = _rms(acc, g_ref[...])


def _combine(dest_flat, y_buf, x2, gates, g, y_tiles, tm=256):
    n, d = x2.shape
    return pl.pallas_call(
        functools.partial(_combine_body, n_tokens=n, y_tiles=y_tiles),
        grid_spec=pltpu.PrefetchScalarGridSpec(
            num_scalar_prefetch=1,
            grid=(n // tm,),
            in_specs=[
                pl.BlockSpec(memory_space=pl.ANY),
                pl.BlockSpec((tm, d), lambda i, *_: (i, 0)),
                pl.BlockSpec((tm, TOP_K), lambda i, *_: (i, 0)),
                pl.BlockSpec((1, d), lambda i, *_: (0, 0)),
            ],
            out_specs=pl.BlockSpec((tm, d), lambda i, *_: (i, 0)),
            scratch_shapes=[pltpu.VMEM((TOP_K, tm) + y_buf.shape[1:], y_buf.dtype),
                            pltpu.SemaphoreType.DMA(())],
        ),
        out_shape=jax.ShapeDtypeStruct((n, d), F32),
        compiler_params=_params(("arbitrary",)),
        name="combine",
    )(dest_flat, y_buf, x2, gates, g.reshape(1, d))


def _cumsum_sublanes(x):
    row = lax.broadcasted_iota(jnp.int32, x.shape, 0)
    s = 1
    while s < x.shape[0]:
        x = x + jnp.where(row >= s, pltpu.roll(x, s, axis=0), 0)
        s *= 2
    return x


def _tables_body(cnt_ref, idx_ref, rank_ref, dest_ref, meta_ref, pad_ref, *, n_rows):
    cnt = cnt_ref[...]
    padded = (cnt + (MOE_BLOCK - 1)) // MOE_BLOCK * MOE_BLOCK
    pend = _cumsum_sublanes(padded)
    pstart = pend - padded
    total = pend[N_EXPERTS - 1:N_EXPERTS, 0:1]

    def lookup(table, sel):
        eid = lax.broadcasted_iota(jnp.int32, (N_EXPERTS, sel.shape[1]), 0)
        return jnp.sum(jnp.where(eid == sel, table[:, 0:1], 0), axis=0, keepdims=True)

    def count_le(table, v):
        return jnp.sum((table[:, 0:1] <= v).astype(jnp.int32), axis=0, keepdims=True)

    width = 2048
    for k in range(TOP_K):
        for c in range(idx_ref.shape[1] // width):
            sl = slice(c * width, (c + 1) * width)
            dest_ref[k:k + 1, sl] = lookup(pstart, idx_ref[k:k + 1, sl]) + rank_ref[k:k + 1, sl]

    n_it = (padded + (ITEM_ROWS - 1)) // ITEM_ROWS
    it_end = _cumsum_sublanes(n_it)
    t = lax.broadcasted_iota(jnp.int32, (1, META_STRIDE), 1)
    e = count_le(it_end, t)
    ec = jnp.minimum(e, N_EXPERTS - 1)
    k = t - lookup(it_end - n_it, ec)
    rows_real = jnp.minimum(lookup(padded, ec) - k * ITEM_ROWS, ITEM_ROWS)
    row0_zero = total + (t - it_end[N_EXPERTS - 1:N_EXPERTS, 0:1]) * ITEM_ROWS
    rows_zero = jnp.clip(n_rows - row0_zero, 0, ITEM_ROWS)
    real = e < N_EXPERTS
    meta_ref[:, 0:META_STRIDE] = ec
    meta_ref[:, META_STRIDE:2 * META_STRIDE] = jnp.where(
        real, lookup(pstart, ec) + k * ITEM_ROWS, jnp.minimum(row0_zero, n_rows - MOE_BLOCK))
    meta_ref[:, 2 * META_STRIDE:3 * META_STRIDE] = jnp.where(real, rows_real, rows_zero) // MOE_BLOCK
    meta_ref[:, 3 * META_STRIDE:4 * META_STRIDE] = jnp.where(real, 0, 1)

    gap = padded - cnt
    gap_end = _cumsum_sublanes(gap)
    q = lax.broadcasted_iota(jnp.int32, pad_ref.shape, 1)
    eq = count_le(gap_end, q)
    inside = lookup(pstart + cnt - (gap_end - gap), eq) + q
    tail = total + q - gap_end[N_EXPERTS - 1:N_EXPERTS, 0:1]
    pad_ref[...] = jnp.where(eq < N_EXPERTS, inside, tail)


def _max_items(n_rows):
    return N_EXPERTS + n_rows // ITEM_ROWS + (N_EXPERTS * MOE_BLOCK) // ITEM_ROWS + 1


def _tables(cnt, idx_t, rank_t, n_rows):
    n_pad = n_rows - idx_t.size
    assert _max_items(n_rows) <= META_STRIDE
    dest, meta, pad_dest = pl.pallas_call(
        functools.partial(_tables_body, n_rows=n_rows),
        out_shape=[
            jax.ShapeDtypeStruct(idx_t.shape, jnp.int32),
            jax.ShapeDtypeStruct((1, 4 * META_STRIDE), jnp.int32),
            jax.ShapeDtypeStruct((1, n_pad), jnp.int32),
        ],
        name="tables",
    )(cnt, idx_t, rank_t)
    return dest.reshape(-1), meta.reshape(-1), pad_dest.reshape(-1)


def kernel(x, norm_mix, w_in, conv_w, conv_b, w_a, b_a, w_x, b_x, lru_lambda, attn_out_norm, lru_out_norm, w_out, norm_ffn, w_router, b_router, w_gate_up, b_gate_up, w_down, b_down, norm_final):
    b, seq, d = x.shape
    n = b * seq
    x2 = x.reshape(n, d)

    proj = _in_proj(x2, norm_mix, w_in)
    proj3 = proj.reshape(b, seq, IN_COLS)

    slopes = jnp.asarray(2.0 ** (-8.0 * np.arange(1, N_HEADS + 1) / N_HEADS), F32)
    attn = _attention(proj3, slopes)

    def pair_blocks(w):
        w4 = w.reshape(-1, 2, LRU_BLOCK, LRU_BLOCK)
        z = jnp.zeros_like(w4[:, 0])
        top = jnp.concatenate([w4[:, 0], z], axis=2)
        bot = jnp.concatenate([z, w4[:, 1]], axis=2)
        return jnp.concatenate([top, bot], axis=1)

    w_bd = jnp.concatenate([pair_blocks(w_a), pair_blocks(w_x)], axis=2)
    rec = _rglru(proj3, conv_w, conv_b, w_bd, b_a, b_x, lru_lambda)

    x_mid = _out_proj(attn.reshape(n, ATTN_WIDTH), rec.reshape(n, LRU_WIDTH),
                      attn_out_norm, lru_out_norm, x2, w_out)

    xn, idx_t, gate_t, rank_t, cnt = _router(x_mid, norm_ffn, w_router.T, b_router)
    n_rows = n * TOP_K + N_EXPERTS * MOE_BLOCK
    dest_flat, meta, pad_dest = _tables(cnt, idx_t, rank_t, n_rows)

    x_sorted = _scatter(dest_flat, pad_dest, xn, n_rows)
    act = _gate_up(meta, x_sorted, w_gate_up, b_gate_up)
    y_buf = _down(meta, act, w_down, b_down)
    out = _combine(dest_flat, y_buf, x_mid, gate_t.T, norm_final, d // DOWN_TILE)
    return out.reshape(b, seq, d)
```

```python
import functools

import numpy as np
import jax
import jax.numpy as jnp
from jax import lax
from jax.experimental import pallas as pl
from jax.experimental.pallas import tpu as pltpu

F32 = jnp.float32
BF16 = jnp.bfloat16
U32 = jnp.uint32
HIGH_HALF = np.uint32(0xFFFF0000)

D_MODEL = 2048
HEAD_DIM = 64
N_HEADS = 16
ATTN_WIDTH = N_HEADS * HEAD_DIM
LRU_WIDTH = D_MODEL - ATTN_WIDTH
LRU_BLOCK = 64
CONV_WIDTH = 4
LRU_C = 8.0
IN_COLS = 3 * ATTN_WIDTH + 2 * LRU_WIDTH
DILATIONS = (1, 4, 16)
ATTN_BLOCK = 128
UNITS_PER_TRIP = (5, 6, 8)
N_EXPERTS = 32
TOP_K = 4
D_FF = D_MODEL
SWIGLU_LIMIT = 7.0
SWIGLU_ALPHA = 1.702
MOE_BLOCK = 128
ITEM_ROWS = 16 * MOE_BLOCK
META_STRIDE = 128
RUN_BLOCKS = 4
SLOT_BLOCKS = (RUN_BLOCKS, RUN_BLOCKS, 2, 1)
EPS = 1e-6

LANES = 128
SUBLANES = 8
VMEM_LIMIT = 56 * 1024 * 1024


def _params(sem, vmem=VMEM_LIMIT):
    return pltpu.CompilerParams(dimension_semantics=sem, vmem_limit_bytes=vmem)


def _rms(xf, g):
    return xf * lax.rsqrt(jnp.mean(xf * xf, axis=-1, keepdims=True) + EPS) * g


def _pack_halves(x):
    half = x.shape[1] // 2
    lo = lax.bitcast_convert_type(x[:, :half].astype(BF16).astype(F32), U32) >> 16
    hi = lax.bitcast_convert_type(x[:, half:].astype(BF16).astype(F32), U32) & HIGH_HALF
    return lo | hi


def _unpack_halves(u):
    return (lax.bitcast_convert_type(u << 16, F32),
            lax.bitcast_convert_type(u & HIGH_HALF, F32))


def _in_proj_body(x_ref, g_ref, w_ref, o_ref, h_ref):
    @pl.when(pl.program_id(1) == 0)
    def _():
        h_ref[...] = _rms(x_ref[...], g_ref[...]).astype(BF16)

    o_ref[...] = jnp.dot(h_ref[...], w_ref[...].astype(BF16), preferred_element_type=F32)


def _in_proj(x2, g, w_in, tm=1024, tn=1024):
    n, d = x2.shape
    cols = w_in.shape[1]
    return pl.pallas_call(
        _in_proj_body,
        grid=(n // tm, cols // tn),
        in_specs=[
            pl.BlockSpec((tm, d), lambda i, j: (i, 0)),
            pl.BlockSpec((1, d), lambda i, j: (0, 0)),
            pl.BlockSpec((d, tn), lambda i, j: (0, j)),
        ],
        out_specs=pl.BlockSpec((tm, tn), lambda i, j: (i, j)),
        out_shape=jax.ShapeDtypeStruct((n, cols), F32),
        scratch_shapes=[pltpu.VMEM((tm, d), BF16)],
        compiler_params=_params(("parallel", "arbitrary")),
        name="in_proj",
    )(x2, g.reshape(1, d), w_in)


REGROUP = 4


def _attention_body(slope_ref, q_ref, k_ref, v_ref, o_ref, qh, kh, vh, acc, mx):
    seq = q_ref.shape[1]
    run = seq // REGROUP
    pair = pl.program_id(1)
    qi = lax.broadcasted_iota(jnp.int32, (ATTN_BLOCK, 2 * ATTN_BLOCK), 0)
    ki = lax.broadcasted_iota(jnp.int32, (ATTN_BLOCK, 2 * ATTN_BLOCK), 1)
    rel2 = qi + ATTN_BLOCK - ki
    ok2 = (rel2 >= 0) & (rel2 <= ATTN_BLOCK)
    rel1 = (qi - ki)[:, :ATTN_BLOCK]
    ok1 = rel1 >= 0

    nh = LANES // HEAD_DIM
    slopes = [slope_ref[pair * nh + hh] for hh in range(nh)]
    def stage(order, rows_dst, rows_src):
        lane = lax.broadcasted_iota(jnp.int32, (run, LANES), 1)
        qv, vv = q_ref[0, rows_src, :] * (HEAD_DIM ** -0.5), v_ref[0, rows_src, :]
        if order:
            kh[rows_dst, :] = k_ref[0, rows_src, :]
        for hh in range(nh):
            mine = (lane >= hh * HEAD_DIM) & (lane < (hh + 1) * HEAD_DIM)
            qh[order, hh, rows_dst, :] = jnp.where(mine, qv, 0.0)
            vh[order, hh, rows_dst, :] = jnp.where(mine, vv, 1.0)

    for c in range(REGROUP):
        stage(0, slice(c * run, (c + 1) * run), pl.ds(c * run, run))
        stage(1, slice(c * run, (c + 1) * run), pl.ds(c, run, stride=REGROUP))

    def run_units(units):
        loaded = [(qh[order, hh, qs, :].astype(BF16),
                   (kh[ks, :] if order else k_ref[0, ks, :]).astype(BF16),
                   vh[order, hh, ks, :].astype(BF16)) for hh, br, order, qs, ks, bias in units]
        scores = [lax.dot_general(q, k, (((1,), (1,)), ((), ())), preferred_element_type=F32)
                  + u[5] for (q, k, v), u in zip(loaded, units)]
        maxes = [jnp.max(s, axis=1, keepdims=True) for s in scores]
        probs = [jnp.exp(s - m).astype(BF16) for s, m in zip(scores, maxes)]
        results = [(jnp.dot(p, v, preferred_element_type=F32), m)
                   for p, (q, k, v), m in zip(probs, loaded, maxes)]
        for (a, m), (hh, br, order, qs, ks, bias) in zip(results, units):
            acc[hh, br, qs, :] = a
            mx[hh, br, qs, :] = jnp.broadcast_to(m, (ATTN_BLOCK, LANES))

    def bias_pair(ok, rel, d):
        return [jnp.where(ok, -sl * (rel * d).astype(F32), -jnp.inf) for sl in slopes]

    def contiguous_branch(br, order, d, n_runs, per_trip):
        nb = seq // (n_runs * ATTN_BLOCK)
        bias1, bias2 = bias_pair(ok1, rel1, d), bias_pair(ok2, rel2, d)
        run_units([(hh, br, order, pl.ds(r * nb * ATTN_BLOCK, ATTN_BLOCK),
                    pl.ds(r * nb * ATTN_BLOCK, ATTN_BLOCK), bias1[hh])
                   for r in range(n_runs) for hh in range(nh)])

        def later(g, carry):
            units = []
            for u in range(per_trip):
                idx = g * per_trip + u
                blk = idx // (nb - 1) * nb + idx % (nb - 1) + 1
                start = pl.multiple_of(blk * ATTN_BLOCK, ATTN_BLOCK)
                units += [(hh, br, order, pl.ds(start, ATTN_BLOCK),
                           pl.ds(start - ATTN_BLOCK, 2 * ATTN_BLOCK), bias2[hh])
                          for hh in range(nh)]
            run_units(units)
            return carry

        lax.fori_loop(0, n_runs * (nb - 1) // per_trip, later, 0)

    contiguous_branch(0, 0, DILATIONS[0], 1, UNITS_PER_TRIP[0])
    contiguous_branch(1, 1, DILATIONS[1], REGROUP, UNITS_PER_TRIP[1])

    d2 = DILATIONS[2]
    sub = d2 // REGROUP
    bias16 = bias_pair(ok1, rel1, d2)

    def strided(g, carry):
        units = []
        for u in range(UNITS_PER_TRIP[2]):
            idx = g * UNITS_PER_TRIP[2] + u
            rows = pl.ds(idx // sub * run + idx % sub, ATTN_BLOCK, stride=sub)
            units += [(hh, 2, 1, rows, rows, bias16[hh]) for hh in range(nh)]
        run_units(units)
        return carry

    lax.fori_loop(0, d2 // UNITS_PER_TRIP[2], strided, 0)

    def merge(c, carry):
        start = c // (run // ATTN_BLOCK) + c % (run // ATTN_BLOCK) * (ATTN_BLOCK * REGROUP)
        nat = pl.ds(start, ATTN_BLOCK, stride=REGROUP)
        reg = pl.ds(pl.multiple_of(c * ATTN_BLOCK, ATTN_BLOCK), ATTN_BLOCK)
        outs = []
        for hh in range(nh):
            m0, m1, m2 = mx[hh, 0, nat, :], mx[hh, 1, reg, :], mx[hh, 2, reg, :]
            mt = jnp.maximum(jnp.maximum(m0, m1), m2)
            tot = (jnp.exp(m0 - mt) * acc[hh, 0, nat, :] + jnp.exp(m1 - mt) * acc[hh, 1, reg, :]
                   + jnp.exp(m2 - mt) * acc[hh, 2, reg, :])
            den = pltpu.roll(tot, HEAD_DIM, axis=1)
            outs.append(tot / den)
        lane_r = lax.broadcasted_iota(jnp.int32, (ATTN_BLOCK, LANES), 1)
        o_ref[0, nat, :] = jnp.where(lane_r < HEAD_DIM, outs[0], outs[1])
        return carry

    lax.fori_loop(0, seq // ATTN_BLOCK, merge, 0, unroll=2)


def _attention(proj3, slopes):
    b, seq, _ = proj3.shape
    npair = ATTN_WIDTH // LANES
    nh = LANES // HEAD_DIM
    blk = (1, seq, LANES)
    return pl.pallas_call(
        _attention_body,
        grid_spec=pltpu.PrefetchScalarGridSpec(
            num_scalar_prefetch=1,
            grid=(b, npair),
            in_specs=[
                pl.BlockSpec(blk, lambda i, j, s: (i, 0, j)),
                pl.BlockSpec(blk, lambda i, j, s: (i, 0, npair + j)),
                pl.BlockSpec(blk, lambda i, j, s: (i, 0, 2 * npair + j)),
            ],
            out_specs=pl.BlockSpec(blk, lambda i, j, s: (i, 0, j)),
            scratch_shapes=[
                pltpu.VMEM((2, nh, seq, LANES), F32),
                pltpu.VMEM((seq, LANES), F32),
                pltpu.VMEM((2, nh, seq, LANES), F32),
                pltpu.VMEM((nh, len(DILATIONS), seq, LANES), F32),
                pltpu.VMEM((nh, len(DILATIONS), seq, LANES), F32),
            ],
        ),
        out_shape=jax.ShapeDtypeStruct((b, seq, ATTN_WIDTH), F32),
        compiler_params=_params(("parallel", "parallel")),
        name="attention",
    )(slopes, proj3, proj3, proj3)


def _rglru_body(xr_ref, gr_ref, cw_ref, cb_ref, w_ref, ba_ref, bx_ref, lam_ref, o_ref,
                xp, a_s, b_s):
    seq = xr_ref.shape[1]
    pad = SUBLANES
    xp[0:pad, :] = jnp.zeros((pad, LANES), F32)
    xp[pad:pad + seq, :] = xr_ref[0]
    lam = lam_ref[...]
    sp = jnp.maximum(-lam, 0.0) + jnp.log(1.0 + jnp.exp(-jnp.abs(lam)))
    w_hi = w_ref[0].astype(BF16)
    w_lo = (w_ref[0] - w_hi.astype(F32)).astype(BF16)
    rows = 256

    def gates(c, carry):
        base = pl.multiple_of(c * rows, rows)
        xc = cb_ref[...] + cw_ref[0:1, :] * xp[pl.ds(base + pad - 3, rows), :]
        for i in range(1, CONV_WIDTH):
            xc = xc + cw_ref[i:i + 1, :] * xp[pl.ds(base + pad - 3 + i, rows), :]
        hi = xc.astype(BF16)
        lo = (xc - hi.astype(F32)).astype(BF16)
        pre = (jnp.dot(hi, w_hi, preferred_element_type=F32)
               + jnp.dot(lo, w_hi, preferred_element_type=F32)
               + jnp.dot(hi, w_lo, preferred_element_type=F32))
        r = jax.nn.sigmoid(pre[:, :LANES] + ba_ref[...])
        ig = jax.nn.sigmoid(pre[:, LANES:] + bx_ref[...])
        log_a = -LRU_C * r * sp
        a = jnp.exp(log_a)
        t = jnp.tanh(log_a)
        b = jnp.sqrt(-2.0 * t / (1.0 - t)) * (ig * xc)
        a_s[pl.ds(base, rows), :] = a
        b_s[pl.ds(base, rows), :] = b
        return carry

    lax.fori_loop(0, seq // rows, gates, 0)

    row = lax.broadcasted_iota(jnp.int32, (SUBLANES, LANES), 0)

    def scan(c, h_prev):
        sl = pl.ds(pl.multiple_of(c * SUBLANES, SUBLANES), SUBLANES)
        a = a_s[sl, :]
        b = b_s[sl, :]
        for s in (1, 2, 4):
            keep = row >= s
            a_sh = jnp.where(keep, pltpu.roll(a, s, axis=0), 1.0)
            b_sh = jnp.where(keep, pltpu.roll(b, s, axis=0), 0.0)
            b = a * b_sh + b
            a = a * a_sh
        h = a * h_prev + b
        o_ref[0, sl, :] = h * jax.nn.gelu(gr_ref[0, sl, :])
        return jnp.broadcast_to(h[SUBLANES - 1:SUBLANES, :], (SUBLANES, LANES))

    lax.fori_loop(0, seq // SUBLANES, scan, jnp.zeros((SUBLANES, LANES), F32), unroll=8)


def _rglru(proj3, conv_w, conv_b, w_bd, b_a, b_x, lam):
    b, seq, _ = proj3.shape
    nt = LRU_WIDTH // LANES
    xr0 = 3 * ATTN_WIDTH // LANES
    gr0 = xr0 + nt
    blk = (1, seq, LANES)
    vec = lambda: pl.BlockSpec((1, LANES), lambda i, j: (0, j))
    return pl.pallas_call(
        _rglru_body,
        grid=(b, nt),
        in_specs=[
            pl.BlockSpec(blk, lambda i, j: (i, 0, xr0 + j)),
            pl.BlockSpec(blk, lambda i, j: (i, 0, gr0 + j)),
            pl.BlockSpec((CONV_WIDTH, LANES), lambda i, j: (0, j)),
            vec(),
            pl.BlockSpec((1, LANES, 2 * LANES), lambda i, j: (j, 0, 0)),
            vec(), vec(), vec(),
        ],
        out_specs=pl.BlockSpec(blk, lambda i, j: (i, 0, j)),
        out_shape=jax.ShapeDtypeStruct((b, seq, LRU_WIDTH), F32),
        scratch_shapes=[
            pltpu.VMEM((seq + SUBLANES, LANES), F32),
            pltpu.VMEM((seq, LANES), F32),
            pltpu.VMEM((seq, LANES), F32),
        ],
        compiler_params=_params(("parallel", "parallel")),
        name="rglru",
    )(proj3, proj3, conv_w, conv_b.reshape(1, -1), w_bd, b_a.reshape(1, -1),
      b_x.reshape(1, -1), lam.reshape(1, -1))


def _out_proj_body(at_ref, rc_ref, ga_ref, gr_ref, x_ref, w_ref, o_ref, h_ref):
    wa = at_ref.shape[1]

    @pl.when(pl.program_id(1) == 0)
    def _():
        h_ref[:, :wa] = _rms(at_ref[...], ga_ref[...]).astype(BF16)
        h_ref[:, wa:] = _rms(rc_ref[...], gr_ref[...]).astype(BF16)

    o_ref[...] = x_ref[...] + jnp.dot(h_ref[...], w_ref[...].astype(BF16),
                                      preferred_element_type=F32)


def _out_proj(attn2, rec2, g_attn, g_rec, x2, w_out, tm=1024, tn=512):
    n, d = x2.shape
    wa, wr = attn2.shape[1], rec2.shape[1]
    return pl.pallas_call(
        _out_proj_body,
        grid=(n // tm, d // tn),
        in_specs=[
            pl.BlockSpec((tm, wa), lambda i, j: (i, 0)),
            pl.BlockSpec((tm, wr), lambda i, j: (i, 0)),
            pl.BlockSpec((1, wa), lambda i, j: (0, 0)),
            pl.BlockSpec((1, wr), lambda i, j: (0, 0)),
            pl.BlockSpec((tm, tn), lambda i, j: (i, j)),
            pl.BlockSpec((d, tn), lambda i, j: (0, j)),
        ],
        out_specs=pl.BlockSpec((tm, tn), lambda i, j: (i, j)),
        out_shape=jax.ShapeDtypeStruct((n, d), F32),
        scratch_shapes=[pltpu.VMEM((tm, d), BF16)],
        compiler_params=_params(("parallel", "arbitrary")),
        name="out_proj",
    )(attn2, rec2, g_attn.reshape(1, wa), g_rec.reshape(1, wr), x2, w_out)


def _router_body(x_ref, g_ref, wt_ref, b_ref, xn_ref, idx_ref, gate_ref, rank_ref, cnt_ref,
                 base):
    tm = x_ref.shape[0]

    @pl.when(pl.program_id(0) == 0)
    def _():
        base[...] = jnp.zeros_like(base)

    xn = _rms(x_ref[...], g_ref[...])
    xn_ref[...] = _pack_halves(xn)
    x_hi = xn.astype(BF16)
    x_lo = (xn - x_hi.astype(F32)).astype(BF16)
    w = wt_ref[...]
    w_hi = w.astype(BF16)
    w_lo = (w - w_hi.astype(F32)).astype(BF16)
    nt = (((1,), (1,)), ((), ()))
    logits = (lax.dot_general(w_hi, x_hi, nt, preferred_element_type=F32)
              + lax.dot_general(w_hi, x_lo, nt, preferred_element_type=F32)
              + lax.dot_general(w_lo, x_hi, nt, preferred_element_type=F32)
              + b_ref[...])

    eid = lax.broadcasted_iota(jnp.int32, (N_EXPERTS, tm), 0)
    work = logits
    vals, hots = [], []
    for _ in range(TOP_K):
        best = jnp.max(work, axis=0, keepdims=True)
        pick = jnp.min(jnp.where(work == best, eid, N_EXPERTS), axis=0, keepdims=True)
        hot = eid == pick
        vals.append(best)
        hots.append(hot)
        work = jnp.where(hot, -jnp.inf, work)
        idx_ref[len(vals) - 1:len(vals), :] = pick

    ex = [jnp.exp(v - vals[0]) for v in vals]
    den = ex[0] + ex[1] + ex[2] + ex[3]
    for k in range(TOP_K):
        gate_ref[k:k + 1, :] = ex[k] / den

    chosen = (hots[0] | hots[1] | hots[2] | hots[3])
    si = lax.broadcasted_iota(jnp.int32, (tm, tm), 0)
    ti = lax.broadcasted_iota(jnp.int32, (tm, tm), 1)
    before = (si < ti).astype(BF16)
    prefix = jnp.dot(chosen.astype(BF16), before, preferred_element_type=F32)
    slot = base[:, 0:1] + prefix
    for k in range(TOP_K):
        rank_ref[k:k + 1, :] = jnp.sum(jnp.where(hots[k], slot, 0.0), axis=0,
                                       keepdims=True).astype(jnp.int32)
    base[...] = base[...] + jnp.sum(chosen.astype(F32), axis=1, keepdims=True)
    cnt_ref[...] = base[...].astype(jnp.int32)


def _router(x2, g, w_router_t, b_router, tm=256):
    n, d = x2.shape
    row = lambda: pl.BlockSpec((TOP_K, tm), lambda i: (0, i))
    return pl.pallas_call(
        _router_body,
        grid=(n // tm,),
        in_specs=[
            pl.BlockSpec((tm, d), lambda i: (i, 0)),
            pl.BlockSpec((1, d), lambda i: (0, 0)),
            pl.BlockSpec((N_EXPERTS, d), lambda i: (0, 0)),
            pl.BlockSpec((N_EXPERTS, 1), lambda i: (0, 0)),
        ],
        out_specs=[
            pl.BlockSpec((tm, d // 2), lambda i: (i, 0)),
            row(), row(), row(),
            pl.BlockSpec((N_EXPERTS, LANES), lambda i: (0, 0)),
        ],
        out_shape=[
            jax.ShapeDtypeStruct((n, d // 2), U32),
            jax.ShapeDtypeStruct((TOP_K, n), jnp.int32),
            jax.ShapeDtypeStruct((TOP_K, n), F32),
            jax.ShapeDtypeStruct((TOP_K, n), jnp.int32),
            jax.ShapeDtypeStruct((N_EXPERTS, LANES), jnp.int32),
        ],
        scratch_shapes=[pltpu.VMEM((N_EXPERTS, LANES), F32)],
        compiler_params=_params(("arbitrary",)),
        name="router",
    )(x2, g.reshape(1, d), w_router_t, b_router.reshape(N_EXPERTS, 1))


def _scatter_body(dest_ref, pad_ref, xn_ref, o_ref, zrow, sem, *, n_tokens, pad_per_step):
    tm = xn_ref.shape[0]
    i = pl.program_id(0)

    @pl.when(i == 0)
    def _():
        zrow[...] = jnp.zeros_like(zrow)

    def row_copy(t, k):
        d = dest_ref[k * n_tokens + i * tm + t]
        return pltpu.make_async_copy(xn_ref.at[pl.ds(t, 1), :], o_ref.at[pl.ds(d, 1), :], sem)

    def pad_copy(q):
        d = pad_ref[i * pad_per_step + q]
        return pltpu.make_async_copy(zrow.at[pl.ds(0, 1), :], o_ref.at[pl.ds(d, 1), :], sem)

    def issue(t, c):
        for k in range(TOP_K):
            row_copy(t, k).start(priority=k % 2)
        return c

    lax.fori_loop(0, tm, issue, 0)

    def issue_pad(q, c):
        pad_copy(q).start()
        return c

    lax.fori_loop(0, pad_per_step, issue_pad, 0)

    def drain(t, c):
        for k in range(TOP_K):
            row_copy(t, k).wait()
        return c

    lax.fori_loop(0, tm, drain, 0)

    def drain_pad(q, c):
        pad_copy(q).wait()
        return c

    lax.fori_loop(0, pad_per_step, drain_pad, 0)


def _scatter(dest_flat, pad_dest, xn, n_rows, tm=256):
    n, d = xn.shape
    steps = n // tm
    pad_per_step = pad_dest.shape[0] // steps
    return pl.pallas_call(
        functools.partial(_scatter_body, n_tokens=n, pad_per_step=pad_per_step),
        grid_spec=pltpu.PrefetchScalarGridSpec(
            num_scalar_prefetch=2,
            grid=(steps,),
            in_specs=[pl.BlockSpec((tm, d), lambda i, *_: (i, 0))],
            out_specs=pl.BlockSpec(memory_space=pl.ANY),
            scratch_shapes=[pltpu.VMEM((SUBLANES, d), xn.dtype), pltpu.SemaphoreType.DMA(())],
        ),
        out_shape=jax.ShapeDtypeStruct((n_rows, d), xn.dtype),
        compiler_params=_params(("arbitrary",)),
        name="scatter",
    )(dest_flat, pad_dest, xn)


def _moe_body(meta_ref, x_hbm, w_ref, b_ref, o_hbm, xbf, stage, wbf, obuf, sem_in, sem_out,
              pend, *, prep_w, epilogue):
    t, j = pl.program_id(0), pl.program_id(1)
    row0 = meta_ref[META_STRIDE + t]
    nsub = meta_ref[2 * META_STRIDE + t]
    zero = meta_ref[3 * META_STRIDE + t]
    tn_out = obuf.shape[2]
    col0 = pl.multiple_of(j * tn_out, tn_out)

    @pl.when((t == 0) & (j == 0))
    def _():
        for k in range(len(SLOT_BLOCKS)):
            pend[k] = 0

    def in_copy(c, slot):
        src = x_hbm.at[pl.ds(pl.multiple_of(row0 + c * MOE_BLOCK, MOE_BLOCK), MOE_BLOCK), :]
        if stage is None:
            dst = xbf.at[pl.ds(pl.multiple_of(c * MOE_BLOCK, MOE_BLOCK), MOE_BLOCK), :]
        else:
            dst = stage.at[slot]
        return pltpu.make_async_copy(src, dst, sem_in.at[slot])

    def fetch_rows(c):
        slot = c % 2

        @pl.when(c + 1 < nsub)
        def _():
            in_copy(c + 1, 1 - slot).start()

        in_copy(c, slot).wait()
        if stage is not None:
            rows = pl.ds(pl.multiple_of(c * MOE_BLOCK, MOE_BLOCK), MOE_BLOCK)
            lo, hi = _unpack_halves(stage[slot])
            xbf[rows, :lo.shape[1]] = lo.astype(BF16)
            xbf[rows, lo.shape[1]:] = hi.astype(BF16)

    def out_copy(first_block, slot, n_blocks):
        n_rows = n_blocks * MOE_BLOCK
        rows = pl.ds(pl.multiple_of(row0 + first_block * MOE_BLOCK, MOE_BLOCK), n_rows)
        return pltpu.make_async_copy(obuf.at[slot, pl.ds(0, n_rows), :],
                                     o_hbm.at[rows, pl.ds(col0, tn_out)], sem_out.at[slot])

    def emit(first_block, slot, n_blocks, value_fn):
        @pl.when(pend[slot] == 1)
        def _():
            out_copy(first_block, slot, n_blocks).wait()

        obuf[slot, pl.ds(0, n_blocks * MOE_BLOCK), :] = value_fn()
        out_copy(first_block, slot, n_blocks).start()
        pend[slot] = 1

    def walk(block_fn):
        def full(p, carry):
            block_fn(p * RUN_BLOCKS, p % 2, RUN_BLOCKS)
            return carry

        lax.fori_loop(0, nsub // RUN_BLOCKS, full, 0)
        done = nsub // RUN_BLOCKS * RUN_BLOCKS
        for slot in range(2, len(SLOT_BLOCKS)):
            n_blocks = SLOT_BLOCKS[slot]

            @pl.when((nsub - done) & n_blocks != 0)
            def _():
                block_fn(done + ((nsub - done) & ~(2 * n_blocks - 1)), slot, n_blocks)

    @pl.when((nsub > 0) & (zero == 0))
    def _compute():
        @pl.when(j == 0)
        def _():
            in_copy(0, 0).start()

        prep_w(w_ref, wbf)

        def block(first_block, slot, n_blocks):
            @pl.when(j == 0)
            def _():
                for k in range(n_blocks):
                    fetch_rows(first_block + k)

            rows = pl.ds(pl.multiple_of(first_block * MOE_BLOCK, MOE_BLOCK), n_blocks * MOE_BLOCK)
            emit(first_block, slot, n_blocks,
                 lambda: epilogue(xbf[rows, :], w_ref if prep_w.func is _keep_weights else wbf, b_ref))

        walk(block)

    @pl.when((nsub > 0) & (zero == 1))
    def _zeros():
        walk(lambda first_block, slot, n_blocks: emit(
            first_block, slot, n_blocks,
            lambda: jnp.zeros((n_blocks * MOE_BLOCK, tn_out), obuf.dtype)))

    @pl.when((t == pl.num_programs(0) - 1) & (j == pl.num_programs(1) - 1))
    def _drain():
        for k, n_blocks in enumerate(SLOT_BLOCKS):
            @pl.when(pend[k] == 1)
            def _():
                out_copy(0, k, n_blocks).wait()


def _moe_call(name, meta, x, w, b, out_cols, out_dtype, tn, tn_out, prep_w, epilogue, tmp_shape):
    p, kdim = x.shape[0], w.shape[1]
    n_items = _max_items(p)
    n_tiles = w.shape[2] // tn
    needs_stage = x.dtype != BF16

    def wmap(t, j, m):
        live = (m[2 * META_STRIDE + t] > 0) & (m[3 * META_STRIDE + t] == 0)
        return (m[t], 0, jnp.where(live, j, n_tiles - 1))

    scratch = [pltpu.VMEM((ITEM_ROWS, kdim), BF16)]
    if needs_stage:
        scratch.append(pltpu.VMEM((2, MOE_BLOCK, x.shape[1]), x.dtype))
    scratch += [
        pltpu.VMEM((kdim, tn), BF16),
        pltpu.VMEM((len(SLOT_BLOCKS), RUN_BLOCKS * MOE_BLOCK, tn_out), out_dtype),
        pltpu.SemaphoreType.DMA((2,)),
        pltpu.SemaphoreType.DMA((len(SLOT_BLOCKS),)),
        pltpu.SMEM((len(SLOT_BLOCKS),), jnp.int32),
    ]
    if tmp_shape is not None:
        scratch.append(pltpu.VMEM(tmp_shape, F32))

    def body(meta_ref, x_hbm, w_ref, b_ref, o_hbm, xbf, *rest):
        rest = list(rest)
        stage = rest.pop(0) if needs_stage else None
        wbf, obuf, sem_in, sem_out, pend = rest[:5]
        tmp = rest[5] if tmp_shape is not None else None
        _moe_body(meta_ref, x_hbm, w_ref, b_ref, o_hbm, xbf, stage, wbf, obuf, sem_in, sem_out,
                  pend, prep_w=functools.partial(prep_w, tmp=tmp), epilogue=epilogue)

    return pl.pallas_call(
        body,
        grid_spec=pltpu.PrefetchScalarGridSpec(
            num_scalar_prefetch=1,
            grid=(n_items, n_tiles),
            in_specs=[
                pl.BlockSpec(memory_space=pl.ANY),
                pl.BlockSpec((1, kdim, tn), wmap),
                pl.BlockSpec((1, 1, tn), wmap),
            ],
            out_specs=pl.BlockSpec(memory_space=pl.ANY),
            scratch_shapes=scratch,
        ),
        out_shape=jax.ShapeDtypeStruct((p, out_cols), out_dtype),
        compiler_params=_params(("arbitrary", "arbitrary")),
        name=name,
    )(meta, x, w, b.reshape(N_EXPERTS, 1, -1))


def _keep_weights(w_ref, wbf, tmp=None):
    pass


def _swiglu_tile(x, w_ref, b_ref):
    even = lax.broadcasted_iota(jnp.int32, (x.shape[0], LANES), 1) % 2 == 0
    cols = []
    for c in range(w_ref.shape[2] // (2 * LANES)):
        c0 = c * 2 * LANES
        gu = jnp.dot(x, w_ref[0, :, c0:c0 + 2 * LANES].astype(BF16), preferred_element_type=F32)
        gu = gu + b_ref[0, :, c0:c0 + 2 * LANES]
        v1, v2 = gu[:, :LANES], gu[:, LANES:]
        gate = jnp.where(even, v1, pltpu.roll(v2, 1, axis=1))
        up = jnp.where(even, pltpu.roll(v1, LANES - 1, axis=1), v2)
        gate = jnp.minimum(gate, SWIGLU_LIMIT)
        up = jnp.clip(up, -SWIGLU_LIMIT, SWIGLU_LIMIT)
        cols.append((gate * jax.nn.sigmoid(SWIGLU_ALPHA * gate) * (up + 1.0)).astype(BF16))
    return jnp.concatenate(cols, axis=1)


def _gate_up(meta, x_sorted, w_gate_up, b_gate_up, tn=1024):
    return _moe_call("gate_up", meta, x_sorted, w_gate_up, b_gate_up, w_gate_up.shape[2] // 2,
                     BF16, tn, tn // 2, _keep_weights, _swiglu_tile, None)


def _permute_weights(w_ref, wbf, tmp):
    half = LANES // 2

    def permute(g, carry):
        base = pl.multiple_of(g * LANES, LANES)
        for c in range(tmp.shape[0]):
            cs = slice(c * LANES, (c + 1) * LANES)
            tmp[c, pl.ds(0, half, stride=2), :] = w_ref[0, pl.ds(base, half), cs]
            tmp[c, pl.ds(1, half, stride=2), :] = w_ref[0, pl.ds(base + half, half), cs]
            wbf[pl.ds(base, LANES), cs] = tmp[c].astype(BF16)
        return carry

    lax.fori_loop(0, wbf.shape[0] // LANES, permute, 0)


def _linear_tile(a, wbf, b_ref):
    return _pack_halves(jnp.dot(a, wbf[...], preferred_element_type=F32) + b_ref[0])


DOWN_TILE = 1024


def _down(meta, act, w_down, b_down, tn=DOWN_TILE):
    return _moe_call("down", meta, act, w_down, b_down, w_down.shape[2] // 2, U32, tn, tn // 2,
                     _permute_weights, _linear_tile, (tn // LANES, LANES, LANES))


def _combine_body(dest_ref, y_ref, x_ref, gate_ref, g_ref, o_ref, ybuf, sem, *, n_tokens,
                  y_tiles):
    tm = x_ref.shape[0]
    i = pl.program_id(0)

    def row_copy(t, k):
        d = dest_ref[k * n_tokens + i * tm + t]
        return pltpu.make_async_copy(y_ref.at[pl.ds(d, 1), :], ybuf.at[k, pl.ds(t, 1), :], sem)

    def issue(t, c):
        for k in range(TOP_K):
            row_copy(t, k).start(priority=k % 2)
        return c

    lax.fori_loop(0, tm, issue, 0)

    def drain(t, c):
        for k in range(TOP_K):
            row_copy(t, k).wait()
        return c

    lax.fori_loop(0, tm, drain, 0)

    acc = x_ref[...]
    tile = x_ref.shape[1] // y_tiles
    for k in range(TOP_K):
        parts = []
        for c in range(y_tiles):
            parts += _unpack_halves(ybuf[k, :, c * tile // 2:(c + 1) * tile // 2])
        acc = acc + gate_ref[:, k:k + 1] * jnp.concatenate(parts, axis=1)
    o_ref[...] = _rms(acc, g_ref[...])


def _combine(dest_flat, y_buf, x2, gates, g, y_tiles, tm=256):
    n, d = x2.shape
    return pl.pallas_call(
        functools.partial(_combine_body, n_tokens=n, y_tiles=y_tiles),
        grid_spec=pltpu.PrefetchScalarGridSpec(
            num_scalar_prefetch=1,
            grid=(n // tm,),
            in_specs=[
                pl.BlockSpec(memory_space=pl.ANY),
                pl.BlockSpec((tm, d), lambda i, *_: (i, 0)),
                pl.BlockSpec((tm, TOP_K), lambda i, *_: (i, 0)),
                pl.BlockSpec((1, d), lambda i, *_: (0, 0)),
            ],
            out_specs=pl.BlockSpec((tm, d), lambda i, *_: (i, 0)),
            scratch_shapes=[pltpu.VMEM((TOP_K, tm) + y_buf.shape[1:], y_buf.dtype),
                            pltpu.SemaphoreType.DMA(())],
        ),
        out_shape=jax.ShapeDtypeStruct((n, d), F32),
        compiler_params=_params(("arbitrary",)),
        name="combine",
    )(dest_flat, y_buf, x2, gates, g.reshape(1, d))


def _cumsum_sublanes(x):
    row = lax.broadcasted_iota(jnp.int32, x.shape, 0)
    s = 1
    while s < x.shape[0]:
        x = x + jnp.where(row >= s, pltpu.roll(x, s, axis=0), 0)
        s *= 2
    return x


def _tables_body(cnt_ref, idx_ref, rank_ref, dest_ref, meta_ref, pad_ref, *, n_rows):
    cnt = cnt_ref[...]
    padded = (cnt + (MOE_BLOCK - 1)) // MOE_BLOCK * MOE_BLOCK
    pend = _cumsum_sublanes(padded)
    pstart = pend - padded
    total = pend[N_EXPERTS - 1:N_EXPERTS, 0:1]

    def lookup(table, sel):
        eid = lax.broadcasted_iota(jnp.int32, (N_EXPERTS, sel.shape[1]), 0)
        return jnp.sum(jnp.where(eid == sel, table[:, 0:1], 0), axis=0, keepdims=True)

    def count_le(table, v):
        return jnp.sum((table[:, 0:1] <= v).astype(jnp.int32), axis=0, keepdims=True)

    width = 2048
    for k in range(TOP_K):
        for c in range(idx_ref.shape[1] // width):
            sl = slice(c * width, (c + 1) * width)
            dest_ref[k:k + 1, sl] = lookup(pstart, idx_ref[k:k + 1, sl]) + rank_ref[k:k + 1, sl]

    n_it = (padded + (ITEM_ROWS - 1)) // ITEM_ROWS
    it_end = _cumsum_sublanes(n_it)
    t = lax.broadcasted_iota(jnp.int32, (1, META_STRIDE), 1)
    e = count_le(it_end, t)
    ec = jnp.minimum(e, N_EXPERTS - 1)
    k = t - lookup(it_end - n_it, ec)
    rows_real = jnp.minimum(lookup(padded, ec) - k * ITEM_ROWS, ITEM_ROWS)
    row0_zero = total + (t - it_end[N_EXPERTS - 1:N_EXPERTS, 0:1]) * ITEM_ROWS
    rows_zero = jnp.clip(n_rows - row0_zero, 0, ITEM_ROWS)
    real = e < N_EXPERTS
    meta_ref[:, 0:META_STRIDE] = ec
    meta_ref[:, META_STRIDE:2 * META_STRIDE] = jnp.where(
        real, lookup(pstart, ec) + k * ITEM_ROWS, jnp.minimum(row0_zero, n_rows - MOE_BLOCK))
    meta_ref[:, 2 * META_STRIDE:3 * META_STRIDE] = jnp.where(real, rows_real, rows_zero) // MOE_BLOCK
    meta_ref[:, 3 * META_STRIDE:4 * META_STRIDE] = jnp.where(real, 0, 1)

    gap = padded - cnt
    gap_end = _cumsum_sublanes(gap)
    q = lax.broadcasted_iota(jnp.int32, pad_ref.shape, 1)
    eq = count_le(gap_end, q)
    inside = lookup(pstart + cnt - (gap_end - gap), eq) + q
    tail = total + q - gap_end[N_EXPERTS - 1:N_EXPERTS, 0:1]
    pad_ref[...] = jnp.where(eq < N_EXPERTS, inside, tail)


def _max_items(n_rows):
    return N_EXPERTS + n_rows // ITEM_ROWS + (N_EXPERTS * MOE_BLOCK) // ITEM_ROWS + 1


def _tables(cnt, idx_t, rank_t, n_rows):
    n_pad = n_rows - idx_t.size
    assert _max_items(n_rows) <= META_STRIDE
    dest, meta, pad_dest = pl.pallas_call(
        functools.partial(_tables_body, n_rows=n_rows),
        out_shape=[
            jax.ShapeDtypeStruct(idx_t.shape, jnp.int32),
            jax.ShapeDtypeStruct((1, 4 * META_STRIDE), jnp.int32),
            jax.ShapeDtypeStruct((1, n_pad), jnp.int32),
        ],
        name="tables",
    )(cnt, idx_t, rank_t)
    return dest.reshape(-1), meta.reshape(-1), pad_dest.reshape(-1)


def kernel(x, norm_mix, w_in, conv_w, conv_b, w_a, b_a, w_x, b_x, lru_lambda, attn_out_norm, lru_out_norm, w_out, norm_ffn, w_router, b_router, w_gate_up, b_gate_up, w_down, b_down, norm_final):
    b, seq, d = x.shape
    n = b * seq
    x2 = x.reshape(n, d)

    proj = _in_proj(x2, norm_mix, w_in)
    proj3 = proj.reshape(b, seq, IN_COLS)

    slopes = jnp.asarray(2.0 ** (-8.0 * np.arange(1, N_HEADS + 1) / N_HEADS), F32)
    attn = _attention(proj3, slopes)

    def pair_blocks(w):
        w4 = w.reshape(-1, 2, LRU_BLOCK, LRU_BLOCK)
        z = jnp.zeros_like(w4[:, 0])
        top = jnp.concatenate([w4[:, 0], z], axis=2)
        bot = jnp.concatenate([z, w4[:, 1]], axis=2)
        return jnp.concatenate([top, bot], axis=1)

    w_bd = jnp.concatenate([pair_blocks(w_a), pair_blocks(w_x)], axis=2)
    rec = _rglru(proj3, conv_w, conv_b, w_bd, b_a, b_x, lru_lambda)

    x_mid = _out_proj(attn.reshape(n, ATTN_WIDTH), rec.reshape(n, LRU_WIDTH),
                      attn_out_norm, lru_out_norm, x2, w_out)

    xn, idx_t, gate_t, rank_t, cnt = _router(x_mid, norm_ffn, w_router.T, b_router)
    n_rows = n * TOP_K + N_EXPERTS * MOE_BLOCK
    dest_flat, meta, pad_dest = _tables(cnt, idx_t, rank_t, n_rows)

    x_sorted = _scatter(dest_flat, pad_dest, xn, n_rows)
    act = _gate_up(meta, x_sorted, w_gate_up, b_gate_up)
    y_buf = _down(meta, act, w_down, b_down)
    out = _combine(dest_flat, y_buf, x_mid, gate_t.T, norm_final, d // DOWN_TILE)
    return out.reshape(b, seq, d)
```

```python
import functools

import numpy as np
import jax
import jax.numpy as jnp
from jax import lax
from jax.experimental import pallas as pl
from jax.experimental.pallas import tpu as pltpu

F32 = jnp.float32
BF16 = jnp.bfloat16
U32 = jnp.uint32
HIGH_HALF = np.uint32(0xFFFF0000)

D_MODEL = 2048
HEAD_DIM = 64
N_HEADS = 16
ATTN_WIDTH = N_HEADS * HEAD_DIM
LRU_WIDTH = D_MODEL - ATTN_WIDTH
LRU_BLOCK = 64
CONV_WIDTH = 4
LRU_C = 8.0
IN_COLS = 3 * ATTN_WIDTH + 2 * LRU_WIDTH
DILATIONS = (1, 4, 16)
ATTN_BLOCK = 128
UNITS_PER_TRIP = (5, 6, 8)
N_EXPERTS = 32
TOP_K = 4
D_FF = D_MODEL
SWIGLU_LIMIT = 7.0
SWIGLU_ALPHA = 1.702
MOE_BLOCK = 128
ITEM_ROWS = 16 * MOE_BLOCK
META_STRIDE = 128
RUN_BLOCKS = 4
SLOT_BLOCKS = (RUN_BLOCKS, RUN_BLOCKS, 2, 1)
EPS = 1e-6

LANES = 128
SUBLANES = 8
VMEM_LIMIT = 56 * 1024 * 1024


def _params(sem, vmem=VMEM_LIMIT):
    return pltpu.CompilerParams(dimension_semantics=sem, vmem_limit_bytes=vmem)


def _rms(xf, g):
    return xf * lax.rsqrt(jnp.mean(xf * xf, axis=-1, keepdims=True) + EPS) * g


def _pack_halves(x):
    half = x.shape[1] // 2
    lo = lax.bitcast_convert_type(x[:, :half].astype(BF16).astype(F32), U32) >> 16
    hi = lax.bitcast_convert_type(x[:, half:].astype(BF16).astype(F32), U32) & HIGH_HALF
    return lo | hi


def _unpack_halves(u):
    return (lax.bitcast_convert_type(u << 16, F32),
            lax.bitcast_convert_type(u & HIGH_HALF, F32))


def _in_proj_body(x_ref, g_ref, w_ref, o_ref, h_ref):
    @pl.when(pl.program_id(1) == 0)
    def _():
        h_ref[...] = _rms(x_ref[...], g_ref[...]).astype(BF16)

    o_ref[...] = jnp.dot(h_ref[...], w_ref[...].astype(BF16), preferred_element_type=F32)


def _in_proj(x2, g, w_in, tm=1024, tn=1024):
    n, d = x2.shape
    cols = w_in.shape[1]
    return pl.pallas_call(
        _in_proj_body,
        grid=(n // tm, cols // tn),
        in_specs=[
            pl.BlockSpec((tm, d), lambda i, j: (i, 0)),
            pl.BlockSpec((1, d), lambda i, j: (0, 0)),
            pl.BlockSpec((d, tn), lambda i, j: (0, j)),
        ],
        out_specs=pl.BlockSpec((tm, tn), lambda i, j: (i, j)),
        out_shape=jax.ShapeDtypeStruct((n, cols), F32),
        scratch_shapes=[pltpu.VMEM((tm, d), BF16)],
        compiler_params=_params(("parallel", "arbitrary")),
        name="in_proj",
    )(x2, g.reshape(1, d), w_in)


REGROUP = 4


def _attention_body(slope_ref, q_ref, k_ref, v_ref, o_ref, qh, kh, vh, acc, mx):
    seq = q_ref.shape[1]
    run = seq // REGROUP
    pair = pl.program_id(1)
    qi = lax.broadcasted_iota(jnp.int32, (ATTN_BLOCK, 2 * ATTN_BLOCK), 0)
    ki = lax.broadcasted_iota(jnp.int32, (ATTN_BLOCK, 2 * ATTN_BLOCK), 1)
    rel2 = qi + ATTN_BLOCK - ki
    ok2 = (rel2 >= 0) & (rel2 <= ATTN_BLOCK)
    rel1 = (qi - ki)[:, :ATTN_BLOCK]
    ok1 = rel1 >= 0

    nh = LANES // HEAD_DIM
    slopes = [slope_ref[pair * nh + hh] for hh in range(nh)]
    def stage(order, rows_dst, rows_src):
        lane = lax.broadcasted_iota(jnp.int32, (run, LANES), 1)
        qv, vv = q_ref[0, rows_src, :] * (HEAD_DIM ** -0.5), v_ref[0, rows_src, :]
        if order:
            kh[rows_dst, :] = k_ref[0, rows_src, :]
        for hh in range(nh):
            mine = (lane >= hh * HEAD_DIM) & (lane < (hh + 1) * HEAD_DIM)
            qh[order, hh, rows_dst, :] = jnp.where(mine, qv, 0.0)
            vh[order, hh, rows_dst, :] = jnp.where(mine, vv, 1.0)

    for c in range(REGROUP):
        stage(0, slice(c * run, (c + 1) * run), pl.ds(c * run, run))
        stage(1, slice(c * run, (c + 1) * run), pl.ds(c, run, stride=REGROUP))

    def run_units(units):
        loaded = [(qh[order, hh, qs, :].astype(BF16),
                   (kh[ks, :] if order else k_ref[0, ks, :]).astype(BF16),
                   vh[order, hh, ks, :].astype(BF16)) for hh, br, order, qs, ks, bias in units]
        scores = [lax.dot_general(q, k, (((1,), (1,)), ((), ())), preferred_element_type=F32)
                  + u[5] for (q, k, v), u in zip(loaded, units)]
        maxes = [jnp.max(s, axis=1, keepdims=True) for s in scores]
        probs = [jnp.exp(s - m).astype(BF16) for s, m in zip(scores, maxes)]
        results = [(jnp.dot(p, v, preferred_element_type=F32), m)
                   for p, (q, k, v), m in zip(probs, loaded, maxes)]
        for (a, m), (hh, br, order, qs, ks, bias) in zip(results, units):
            acc[hh, br, qs, :] = a
            mx[hh, br, qs, :] = jnp.broadcast_to(m, (ATTN_BLOCK, LANES))

    def bias_pair(ok, rel, d):
        return [jnp.where(ok, -sl * (rel * d).astype(F32), -jnp.inf) for sl in slopes]

    def contiguous_branch(br, order, d, n_runs, per_trip):
        nb = seq // (n_runs * ATTN_BLOCK)
        bias1, bias2 = bias_pair(ok1, rel1, d), bias_pair(ok2, rel2, d)
        run_units([(hh, br, order, pl.ds(r * nb * ATTN_BLOCK, ATTN_BLOCK),
                    pl.ds(r * nb * ATTN_BLOCK, ATTN_BLOCK), bias1[hh])
                   for r in range(n_runs) for hh in range(nh)])

        def later(g, carry):
            units = []
            for u in range(per_trip):
                idx = g * per_trip + u
                blk = idx // (nb - 1) * nb + idx % (nb - 1) + 1
                start = pl.multiple_of(blk * ATTN_BLOCK, ATTN_BLOCK)
                units += [(hh, br, order, pl.ds(start, ATTN_BLOCK),
                           pl.ds(start - ATTN_BLOCK, 2 * ATTN_BLOCK), bias2[hh])
                          for hh in range(nh)]
            run_units(units)
            return carry

        lax.fori_loop(0, n_runs * (nb - 1) // per_trip, later, 0)

    contiguous_branch(0, 0, DILATIONS[0], 1, UNITS_PER_TRIP[0])
    contiguous_branch(1, 1, DILATIONS[1], REGROUP, UNITS_PER_TRIP[1])

    d2 = DILATIONS[2]
    sub = d2 // REGROUP
    bias16 = bias_pair(ok1, rel1, d2)

    def strided(g, carry):
        units = []
        for u in range(UNITS_PER_TRIP[2]):
            idx = g * UNITS_PER_TRIP[2] + u
            rows = pl.ds(idx // sub * run + idx % sub, ATTN_BLOCK, stride=sub)
            units += [(hh, 2, 1, rows, rows, bias16[hh]) for hh in range(nh)]
        run_units(units)
        return carry

    lax.fori_loop(0, d2 // UNITS_PER_TRIP[2], strided, 0)

    def merge(c, carry):
        start = c // (run // ATTN_BLOCK) + c % (run // ATTN_BLOCK) * (ATTN_BLOCK * REGROUP)
        nat = pl.ds(start, ATTN_BLOCK, stride=REGROUP)
        reg = pl.ds(pl.multiple_of(c * ATTN_BLOCK, ATTN_BLOCK), ATTN_BLOCK)
        outs = []
        for hh in range(nh):
            m0, m1, m2 = mx[hh, 0, nat, :], mx[hh, 1, reg, :], mx[hh, 2, reg, :]
            mt = jnp.maximum(jnp.maximum(m0, m1), m2)
            tot = (jnp.exp(m0 - mt) * acc[hh, 0, nat, :] + jnp.exp(m1 - mt) * acc[hh, 1, reg, :]
                   + jnp.exp(m2 - mt) * acc[hh, 2, reg, :])
            den = pltpu.roll(tot, HEAD_DIM, axis=1)
            outs.append(tot / den)
        lane_r = lax.broadcasted_iota(jnp.int32, (ATTN_BLOCK, LANES), 1)
        o_ref[0, nat, :] = jnp.where(lane_r < HEAD_DIM, outs[0], outs[1])
        return carry

    lax.fori_loop(0, seq // ATTN_BLOCK, merge, 0, unroll=2)


def _attention(proj3, slopes):
    b, seq, _ = proj3.shape
    npair = ATTN_WIDTH // LANES
    nh = LANES // HEAD_DIM
    blk = (1, seq, LANES)
    return pl.pallas_call(
        _attention_body,
        grid_spec=pltpu.PrefetchScalarGridSpec(
            num_scalar_prefetch=1,
            grid=(b, npair),
            in_specs=[
                pl.BlockSpec(blk, lambda i, j, s: (i, 0, j)),
                pl.BlockSpec(blk, lambda i, j, s: (i, 0, npair + j)),
                pl.BlockSpec(blk, lambda i, j, s: (i, 0, 2 * npair + j)),
            ],
            out_specs=pl.BlockSpec(blk, lambda i, j, s: (i, 0, j)),
            scratch_shapes=[
                pltpu.VMEM((2, nh, seq, LANES), F32),
                pltpu.VMEM((seq, LANES), F32),
                pltpu.VMEM((2, nh, seq, LANES), F32),
                pltpu.VMEM((nh, len(DILATIONS), seq, LANES), F32),
                pltpu.VMEM((nh, len(DILATIONS), seq, LANES), F32),
            ],
        ),
        out_shape=jax.ShapeDtypeStruct((b, seq, ATTN_WIDTH), F32),
        compiler_params=_params(("parallel", "parallel")),
        name="attention",
    )(slopes, proj3, proj3, proj3)


def _rglru_body(xr_ref, gr_ref, cw_ref, cb_ref, w_ref, ba_ref, bx_ref, lam_ref, o_ref,
                xp, a_s, b_s):
    seq = xr_ref.shape[1]
    pad = SUBLANES
    xp[0:pad, :] = jnp.zeros((pad, LANES), F32)
    xp[pad:pad + seq, :] = xr_ref[0]
    lam = lam_ref[...]
    sp = jnp.maximum(-lam, 0.0) + jnp.log(1.0 + jnp.exp(-jnp.abs(lam)))
    w_hi = w_ref[0].astype(BF16)
    w_lo = (w_ref[0] - w_hi.astype(F32)).astype(BF16)
    rows = 256

    def gates(c, carry):
        base = pl.multiple_of(c * rows, rows)
        xc = cb_ref[...] + cw_ref[0:1, :] * xp[pl.ds(base + pad - 3, rows), :]
        for i in range(1, CONV_WIDTH):
            xc = xc + cw_ref[i:i + 1, :] * xp[pl.ds(base + pad - 3 + i, rows), :]
        hi = xc.astype(BF16)
        lo = (xc - hi.astype(F32)).astype(BF16)
        pre = (jnp.dot(hi, w_hi, preferred_element_type=F32)
               + jnp.dot(lo, w_hi, preferred_element_type=F32)
               + jnp.dot(hi, w_lo, preferred_element_type=F32))
        r = jax.nn.sigmoid(pre[:, :LANES] + ba_ref[...])
        ig = jax.nn.sigmoid(pre[:, LANES:] + bx_ref[...])
        log_a = -LRU_C * r * sp
        a = jnp.exp(log_a)
        t = jnp.tanh(log_a)
        b = jnp.sqrt(-2.0 * t / (1.0 - t)) * (ig * xc)
        a_s[pl.ds(base, rows), :] = a
        b_s[pl.ds(base, rows), :] = b
        return carry

    lax.fori_loop(0, seq // rows, gates, 0)

    row = lax.broadcasted_iota(jnp.int32, (SUBLANES, LANES), 0)

    def scan(c, h_prev):
        sl = pl.ds(pl.multiple_of(c * SUBLANES, SUBLANES), SUBLANES)
        a = a_s[sl, :]
        b = b_s[sl, :]
        for s in (1, 2, 4):
            keep = row >= s
            a_sh = jnp.where(keep, pltpu.roll(a, s, axis=0), 1.0)
            b_sh = jnp.where(keep, pltpu.roll(b, s, axis=0), 0.0)
            b = a * b_sh + b
            a = a * a_sh
        h = a * h_prev + b
        o_ref[0, sl, :] = h * jax.nn.gelu(gr_ref[0, sl, :])
        return jnp.broadcast_to(h[SUBLANES - 1:SUBLANES, :], (SUBLANES, LANES))

    lax.fori_loop(0, seq // SUBLANES, scan, jnp.zeros((SUBLANES, LANES), F32), unroll=8)


def _rglru(proj3, conv_w, conv_b, w_bd, b_a, b_x, lam):
    b, seq, _ = proj3.shape
    nt = LRU_WIDTH // LANES
    xr0 = 3 * ATTN_WIDTH // LANES
    gr0 = xr0 + nt
    blk = (1, seq, LANES)
    vec = lambda: pl.BlockSpec((1, LANES), lambda i, j: (0, j))
    return pl.pallas_call(
        _rglru_body,
        grid=(b, nt),
        in_specs=[
            pl.BlockSpec(blk, lambda i, j: (i, 0, xr0 + j)),
            pl.BlockSpec(blk, lambda i, j: (i, 0, gr0 + j)),
            pl.BlockSpec((CONV_WIDTH, LANES), lambda i, j: (0, j)),
            vec(),
            pl.BlockSpec((1, LANES, 2 * LANES), lambda i, j: (j, 0, 0)),
            vec(), vec(), vec(),
        ],
        out_specs=pl.BlockSpec(blk, lambda i, j: (i, 0, j)),
        out_shape=jax.ShapeDtypeStruct((b, seq, LRU_WIDTH), F32),
        scratch_shapes=[
            pltpu.VMEM((seq + SUBLANES, LANES), F32),
            pltpu.VMEM((seq, LANES), F32),
            pltpu.VMEM((seq, LANES), F32),
        ],
        compiler_params=_params(("parallel", "parallel")),
        name="rglru",
    )(proj3, proj3, conv_w, conv_b.reshape(1, -1), w_bd, b_a.reshape(1, -1),
      b_x.reshape(1, -1), lam.reshape(1, -1))


def _out_proj_body(at_ref, rc_ref, ga_ref, gr_ref, x_ref, w_ref, o_ref, h_ref):
    wa = at_ref.shape[1]

    @pl.when(pl.program_id(1) == 0)
    def _():
        h_ref[:, :wa] = _rms(at_ref[...], ga_ref[...]).astype(BF16)
        h_ref[:, wa:] = _rms(rc_ref[...], gr_ref[...]).astype(BF16)

    o_ref[...] = x_ref[...] + jnp.dot(h_ref[...], w_ref[...].astype(BF16),
                                      preferred_element_type=F32)


def _out_proj(attn2, rec2, g_attn, g_rec, x2, w_out, tm=1024, tn=512):
    n, d = x2.shape
    wa, wr = attn2.shape[1], rec2.shape[1]
    return pl.pallas_call(
        _out_proj_body,
        grid=(n // tm, d // tn),
        in_specs=[
            pl.BlockSpec((tm, wa), lambda i, j: (i, 0)),
            pl.BlockSpec((tm, wr), lambda i, j: (i, 0)),
            pl.BlockSpec((1, wa), lambda i, j: (0, 0)),
            pl.BlockSpec((1, wr), lambda i, j: (0, 0)),
            pl.BlockSpec((tm, tn), lambda i, j: (i, j)),
            pl.BlockSpec((d, tn), lambda i, j: (0, j)),
        ],
        out_specs=pl.BlockSpec((tm, tn), lambda i, j: (i, j)),
        out_shape=jax.ShapeDtypeStruct((n, d), F32),
        scratch_shapes=[pltpu.VMEM((tm, d), BF16)],
        compiler_params=_params(("parallel", "arbitrary")),
        name="out_proj",
    )(attn2, rec2, g_attn.reshape(1, wa), g_rec.reshape(1, wr), x2, w_out)


def _router_body(x_ref, g_ref, wt_ref, b_ref, xn_ref, idx_ref, gate_ref, rank_ref, cnt_ref,
                 base):
    tm = x_ref.shape[0]

    @pl.when(pl.program_id(0) == 0)
    def _():
        base[...] = jnp.zeros_like(base)

    xn = _rms(x_ref[...], g_ref[...])
    xn_ref[...] = _pack_halves(xn)
    x_hi = xn.astype(BF16)
    x_lo = (xn - x_hi.astype(F32)).astype(BF16)
    w = wt_ref[...]
    w_hi = w.astype(BF16)
    w_lo = (w - w_hi.astype(F32)).astype(BF16)
    nt = (((1,), (1,)), ((), ()))
    logits = (lax.dot_general(w_hi, x_hi, nt, preferred_element_type=F32)
              + lax.dot_general(w_hi, x_lo, nt, preferred_element_type=F32)
              + lax.dot_general(w_lo, x_hi, nt, preferred_element_type=F32)
              + b_ref[...])

    eid = lax.broadcasted_iota(jnp.int32, (N_EXPERTS, tm), 0)
    work = logits
    vals, hots = [], []
    for _ in range(TOP_K):
        best = jnp.max(work, axis=0, keepdims=True)
        pick = jnp.min(jnp.where(work == best, eid, N_EXPERTS), axis=0, keepdims=True)
        hot = eid == pick
        vals.append(best)
        hots.append(hot)
        work = jnp.where(hot, -jnp.inf, work)
        idx_ref[len(vals) - 1:len(vals), :] = pick

    ex = [jnp.exp(v - vals[0]) for v in vals]
    den = ex[0] + ex[1] + ex[2] + ex[3]
    for k in range(TOP_K):
        gate_ref[k:k + 1, :] = ex[k] / den

    chosen = (hots[0] | hots[1] | hots[2] | hots[3])
    si = lax.broadcasted_iota(jnp.int32, (tm, tm), 0)
    ti = lax.broadcasted_iota(jnp.int32, (tm, tm), 1)
    before = (si < ti).astype(BF16)
    prefix = jnp.dot(chosen.astype(BF16), before, preferred_element_type=F32)
    slot = base[:, 0:1] + prefix
    for k in range(TOP_K):
        rank_ref[k:k + 1, :] = jnp.sum(jnp.where(hots[k], slot, 0.0), axis=0,
                                       keepdims=True).astype(jnp.int32)
    base[...] = base[...] + jnp.sum(chosen.astype(F32), axis=1, keepdims=True)
    cnt_ref[...] = base[...].astype(jnp.int32)


def _router(x2, g, w_router_t, b_router, tm=256):
    n, d = x2.shape
    row = lambda: pl.BlockSpec((TOP_K, tm), lambda i: (0, i))
    return pl.pallas_call(
        _router_body,
        grid=(n // tm,),
        in_specs=[
            pl.BlockSpec((tm, d), lambda i: (i, 0)),
            pl.BlockSpec((1, d), lambda i: (0, 0)),
            pl.BlockSpec((N_EXPERTS, d), lambda i: (0, 0)),
            pl.BlockSpec((N_EXPERTS, 1), lambda i: (0, 0)),
        ],
        out_specs=[
            pl.BlockSpec((tm, d // 2), lambda i: (i, 0)),
            row(), row(), row(),
            pl.BlockSpec((N_EXPERTS, LANES), lambda i: (0, 0)),
        ],
        out_shape=[
            jax.ShapeDtypeStruct((n, d // 2), U32),
            jax.ShapeDtypeStruct((TOP_K, n), jnp.int32),
            jax.ShapeDtypeStruct((TOP_K, n), F32),
            jax.ShapeDtypeStruct((TOP_K, n), jnp.int32),
            jax.ShapeDtypeStruct((N_EXPERTS, LANES), jnp.int32),
        ],
        scratch_shapes=[pltpu.VMEM((N_EXPERTS, LANES), F32)],
        compiler_params=_params(("arbitrary",)),
        name="router",
    )(x2, g.reshape(1, d), w_router_t, b_router.reshape(N_EXPERTS, 1))


def _scatter_body(dest_ref, pad_ref, xn_ref, o_ref, zrow, sem, *, n_tokens, pad_per_step):
    tm = xn_ref.shape[0]
    i = pl.program_id(0)

    @pl.when(i == 0)
    def _():
        zrow[...] = jnp.zeros_like(zrow)

    def row_copy(t, k):
        d = dest_ref[k * n_tokens + i * tm + t]
        return pltpu.make_async_copy(xn_ref.at[pl.ds(t, 1), :], o_ref.at[pl.ds(d, 1), :], sem)

    def pad_copy(q):
        d = pad_ref[i * pad_per_step + q]
        return pltpu.make_async_copy(zrow.at[pl.ds(0, 1), :], o_ref.at[pl.ds(d, 1), :], sem)

    def issue(t, c):
        for k in range(TOP_K):
            row_copy(t, k).start(priority=k % 2)
        return c

    lax.fori_loop(0, tm, issue, 0)

    def issue_pad(q, c):
        pad_copy(q).start()
        return c

    lax.fori_loop(0, pad_per_step, issue_pad, 0)

    def drain(t, c):
        for k in range(TOP_K):
            row_copy(t, k).wait()
        return c

    lax.fori_loop(0, tm, drain, 0)

    def drain_pad(q, c):
        pad_copy(q).wait()
        return c

    lax.fori_loop(0, pad_per_step, drain_pad, 0)


def _scatter(dest_flat, pad_dest, xn, n_rows, tm=256):
    n, d = xn.shape
    steps = n // tm
    pad_per_step = pad_dest.shape[0] // steps
    return pl.pallas_call(
        functools.partial(_scatter_body, n_tokens=n, pad_per_step=pad_per_step),
        grid_spec=pltpu.PrefetchScalarGridSpec(
            num_scalar_prefetch=2,
            grid=(steps,),
            in_specs=[pl.BlockSpec((tm, d), lambda i, *_: (i, 0))],
            out_specs=pl.BlockSpec(memory_space=pl.ANY),
            scratch_shapes=[pltpu.VMEM((SUBLANES, d), xn.dtype), pltpu.SemaphoreType.DMA(())],
        ),
        out_shape=jax.ShapeDtypeStruct((n_rows, d), xn.dtype),
        compiler_params=_params(("arbitrary",)),
        name="scatter",
    )(dest_flat, pad_dest, xn)


def _moe_body(meta_ref, x_hbm, w_ref, b_ref, o_hbm, xbf, stage, wbf, obuf, sem_in, sem_out,
              pend, *, prep_w, epilogue):
    t, j = pl.program_id(0), pl.program_id(1)
    row0 = meta_ref[META_STRIDE + t]
    nsub = meta_ref[2 * META_STRIDE + t]
    zero = meta_ref[3 * META_STRIDE + t]
    tn_out = obuf.shape[2]
    col0 = pl.multiple_of(j * tn_out, tn_out)

    @pl.when((t == 0) & (j == 0))
    def _():
        for k in range(len(SLOT_BLOCKS)):
            pend[k] = 0

    def in_copy(c, slot):
        src = x_hbm.at[pl.ds(pl.multiple_of(row0 + c * MOE_BLOCK, MOE_BLOCK), MOE_BLOCK), :]
        if stage is None:
            dst = xbf.at[pl.ds(pl.multiple_of(c * MOE_BLOCK, MOE_BLOCK), MOE_BLOCK), :]
        else:
            dst = stage.at[slot]
        return pltpu.make_async_copy(src, dst, sem_in.at[slot])

    def fetch_rows(c):
        slot = c % 2

        @pl.when(c + 1 < nsub)
        def _():
            in_copy(c + 1, 1 - slot).start()

        in_copy(c, slot).wait()
        if stage is not None:
            rows = pl.ds(pl.multiple_of(c * MOE_BLOCK, MOE_BLOCK), MOE_BLOCK)
            lo, hi = _unpack_halves(stage[slot])
            xbf[rows, :lo.shape[1]] = lo.astype(BF16)
            xbf[rows, lo.shape[1]:] = hi.astype(BF16)

    def out_copy(first_block, slot, n_blocks):
        n_rows = n_blocks * MOE_BLOCK
        rows = pl.ds(pl.multiple_of(row0 + first_block * MOE_BLOCK, MOE_BLOCK), n_rows)
        return pltpu.make_async_copy(obuf.at[slot, pl.ds(0, n_rows), :],
                                     o_hbm.at[rows, pl.ds(col0, tn_out)], sem_out.at[slot])

    def emit(first_block, slot, n_blocks, value_fn):
        @pl.when(pend[slot] == 1)
        def _():
            out_copy(first_block, slot, n_blocks).wait()

        obuf[slot, pl.ds(0, n_blocks * MOE_BLOCK), :] = value_fn()
        out_copy(first_block, slot, n_blocks).start()
        pend[slot] = 1

    def walk(block_fn):
        def full(p, carry):
            block_fn(p * RUN_BLOCKS, p % 2, RUN_BLOCKS)
            return carry

        lax.fori_loop(0, nsub // RUN_BLOCKS, full, 0)
        done = nsub // RUN_BLOCKS * RUN_BLOCKS
        for slot in range(2, len(SLOT_BLOCKS)):
            n_blocks = SLOT_BLOCKS[slot]

            @pl.when((nsub - done) & n_blocks != 0)
            def _():
                block_fn(done + ((nsub - done) & ~(2 * n_blocks - 1)), slot, n_blocks)

    @pl.when((nsub > 0) & (zero == 0))
    def _compute():
        @pl.when(j == 0)
        def _():
            in_copy(0, 0).start()

        prep_w(w_ref, wbf)

        def block(first_block, slot, n_blocks):
            @pl.when(j == 0)
            def _():
                for k in range(n_blocks):
                    fetch_rows(first_block + k)

            rows = pl.ds(pl.multiple_of(first_block * MOE_BLOCK, MOE_BLOCK), n_blocks * MOE_BLOCK)
            emit(first_block, slot, n_blocks,
                 lambda: epilogue(xbf[rows, :], w_ref if prep_w.func is _keep_weights else wbf, b_ref))

        walk(block)

    @pl.when((nsub > 0) & (zero == 1))
    def _zeros():
        walk(lambda first_block, slot, n_blocks: emit(
            first_block, slot, n_blocks,
            lambda: jnp.zeros((n_blocks * MOE_BLOCK, tn_out), obuf.dtype)))

    @pl.when((t == pl.num_programs(0) - 1) & (j == pl.num_programs(1) - 1))
    def _drain():
        for k, n_blocks in enumerate(SLOT_BLOCKS):
            @pl.when(pend[k] == 1)
            def _():
                out_copy(0, k, n_blocks).wait()


def _moe_call(name, meta, x, w, b, out_cols, out_dtype, tn, tn_out, prep_w, epilogue, tmp_shape):
    p, kdim = x.shape[0], w.shape[1]
    n_items = _max_items(p)
    n_tiles = w.shape[2] // tn
    needs_stage = x.dtype != BF16

    def wmap(t, j, m):
        live = (m[2 * META_STRIDE + t] > 0) & (m[3 * META_STRIDE + t] == 0)
        return (m[t], 0, jnp.where(live, j, n_tiles - 1))

    scratch = [pltpu.VMEM((ITEM_ROWS, kdim), BF16)]
    if needs_stage:
        scratch.append(pltpu.VMEM((2, MOE_BLOCK, x.shape[1]), x.dtype))
    scratch += [
        pltpu.VMEM((kdim, tn), BF16),
        pltpu.VMEM((len(SLOT_BLOCKS), RUN_BLOCKS * MOE_BLOCK, tn_out), out_dtype),
        pltpu.SemaphoreType.DMA((2,)),
        pltpu.SemaphoreType.DMA((len(SLOT_BLOCKS),)),
        pltpu.SMEM((len(SLOT_BLOCKS),), jnp.int32),
    ]
    if tmp_shape is not None:
        scratch.append(pltpu.VMEM(tmp_shape, F32))

    def body(meta_ref, x_hbm, w_ref, b_ref, o_hbm, xbf, *rest):
        rest = list(rest)
        stage = rest.pop(0) if needs_stage else None
        wbf, obuf, sem_in, sem_out, pend = rest[:5]
        tmp = rest[5] if tmp_shape is not None else None
        _moe_body(meta_ref, x_hbm, w_ref, b_ref, o_hbm, xbf, stage, wbf, obuf, sem_in, sem_out,
                  pend, prep_w=functools.partial(prep_w, tmp=tmp), epilogue=epilogue)

    return pl.pallas_call(
        body,
        grid_spec=pltpu.PrefetchScalarGridSpec(
            num_scalar_prefetch=1,
            grid=(n_items, n_tiles),
            in_specs=[
                pl.BlockSpec(memory_space=pl.ANY),
                pl.BlockSpec((1, kdim, tn), wmap),
                pl.BlockSpec((1, 1, tn), wmap),
            ],
            out_specs=pl.BlockSpec(memory_space=pl.ANY),
            scratch_shapes=scratch,
        ),
        out_shape=jax.ShapeDtypeStruct((p, out_cols), out_dtype),
        compiler_params=_params(("arbitrary", "arbitrary")),
        name=name,
    )(meta, x, w, b.reshape(N_EXPERTS, 1, -1))


def _keep_weights(w_ref, wbf, tmp=None):
    pass


def _swiglu_tile(x, w_ref, b_ref):
    even = lax.broadcasted_iota(jnp.int32, (x.shape[0], LANES), 1) % 2 == 0
    cols = []
    for c in range(w_ref.shape[2] // (2 * LANES)):
        c0 = c * 2 * LANES
        gu = jnp.dot(x, w_ref[0, :, c0:c0 + 2 * LANES].astype(BF16), preferred_element_type=F32)
        gu = gu + b_ref[0, :, c0:c0 + 2 * LANES]
        v1, v2 = gu[:, :LANES], gu[:, LANES:]
        gate = jnp.where(even, v1, pltpu.roll(v2, 1, axis=1))
        up = jnp.where(even, pltpu.roll(v1, LANES - 1, axis=1), v2)
        gate = jnp.minimum(gate, SWIGLU_LIMIT)
        up = jnp.clip(up, -SWIGLU_LIMIT, SWIGLU_LIMIT)
        cols.append((gate * jax.nn.sigmoid(SWIGLU_ALPHA * gate) * (up + 1.0)).astype(BF16))
    return jnp.concatenate(cols, axis=1)


def _gate_up(meta, x_sorted, w_gate_up, b_gate_up, tn=1024):
    return _moe_call("gate_up", meta, x_sorted, w_gate_up, b_gate_up, w_gate_up.shape[2] // 2,
                     BF16, tn, tn // 2, _keep_weights, _swiglu_tile, None)


def _permute_weights(w_ref, wbf, tmp):
    half = LANES // 2

    def permute(g, carry):
        base = pl.multiple_of(g * LANES, LANES)
        for c in range(tmp.shape[0]):
            cs = slice(c * LANES, (c + 1) * LANES)
            tmp[c, pl.ds(0, half, stride=2), :] = w_ref[0, pl.ds(base, half), cs]
            tmp[c, pl.ds(1, half, stride=2), :] = w_ref[0, pl.ds(base + half, half), cs]
            wbf[pl.ds(base, LANES), cs] = tmp[c].astype(BF16)
        return carry

    lax.fori_loop(0, wbf.shape[0] // LANES, permute, 0)


PACK_COLS = 512


def _linear_tile(a, wbf, b_ref):
    parts = []
    for c in range(wbf.shape[1] // PACK_COLS):
        cs = slice(c * PACK_COLS, (c + 1) * PACK_COLS)
        parts.append(_pack_halves(jnp.dot(a, wbf[:, cs], preferred_element_type=F32)
                                  + b_ref[0, :, cs]))
    return jnp.concatenate(parts, axis=1)


DOWN_TILE = 1024


def _down(meta, act, w_down, b_down, tn=DOWN_TILE):
    return _moe_call("down", meta, act, w_down, b_down, w_down.shape[2] // 2, U32, tn, tn // 2,
                     _permute_weights, _linear_tile, (tn // LANES, LANES, LANES))


def _combine_body(dest_ref, y_ref, x_ref, gate_ref, g_ref, o_ref, ybuf, sem, *, n_tokens,
                  y_tiles):
    tm = x_ref.shape[0]
    i = pl.program_id(0)

    def row_copy(t, k):
        d = dest_ref[k * n_tokens + i * tm + t]
        return pltpu.make_async_copy(y_ref.at[pl.ds(d, 1), :], ybuf.at[k, pl.ds(t, 1), :], sem)

    def issue(t, c):
        for k in range(TOP_K):
            row_copy(t, k).start(priority=k % 2)
        return c

    lax.fori_loop(0, tm, issue, 0)

    def drain(t, c):
        for k in range(TOP_K):
            row_copy(t, k).wait()
        return c

    lax.fori_loop(0, tm, drain, 0)

    acc = x_ref[...]
    tile = x_ref.shape[1] // y_tiles
    for k in range(TOP_K):
        parts = []
        for c in range(y_tiles):
            parts += _unpack_halves(ybuf[k, :, c * tile // 2:(c + 1) * tile // 2])
        acc = acc + gate_ref[:, k:k + 1] * jnp.concatenate(parts, axis=1)
    o_ref[...] = _rms(acc, g_ref[...])


def _combine(dest_flat, y_buf, x2, gates, g, y_tiles, tm=256):
    n, d = x2.shape
    return pl.pallas_call(
        functools.partial(_combine_body, n_tokens=n, y_tiles=y_tiles),
        grid_spec=pltpu.PrefetchScalarGridSpec(
            num_scalar_prefetch=1,
            grid=(n // tm,),
            in_specs=[
                pl.BlockSpec(memory_space=pl.ANY),
                pl.BlockSpec((tm, d), lambda i, *_: (i, 0)),
                pl.BlockSpec((tm, TOP_K), lambda i, *_: (i, 0)),
                pl.BlockSpec((1, d), lambda i, *_: (0, 0)),
            ],
            out_specs=pl.BlockSpec((tm, d), lambda i, *_: (i, 0)),
            scratch_shapes=[pltpu.VMEM((TOP_K, tm) + y_buf.shape[1:], y_buf.dtype),
                            pltpu.SemaphoreType.DMA(())],
        ),
        out_shape=jax.ShapeDtypeStruct((n, d), F32),
        compiler_params=_params(("arbitrary",)),
        name="combine",
    )(dest_flat, y_buf, x2, gates, g.reshape(1, d))


def _cumsum_sublanes(x):
    row = lax.broadcasted_iota(jnp.int32, x.shape, 0)
    s = 1
    while s < x.shape[0]:
        x = x + jnp.where(row >= s, pltpu.roll(x, s, axis=0), 0)
        s *= 2
    return x


def _tables_body(cnt_ref, idx_ref, rank_ref, dest_ref, meta_ref, pad_ref, *, n_rows):
    cnt = cnt_ref[...]
    padded = (cnt + (MOE_BLOCK - 1)) // MOE_BLOCK * MOE_BLOCK
    pend = _cumsum_sublanes(padded)
    pstart = pend - padded
    total = pend[N_EXPERTS - 1:N_EXPERTS, 0:1]

    def lookup(table, sel):
        eid = lax.broadcasted_iota(jnp.int32, (N_EXPERTS, sel.shape[1]), 0)
        return jnp.sum(jnp.where(eid == sel, table[:, 0:1], 0), axis=0, keepdims=True)

    def count_le(table, v):
        return jnp.sum((table[:, 0:1] <= v).astype(jnp.int32), axis=0, keepdims=True)

    width = 2048
    for k in range(TOP_K):
        for c in range(idx_ref.shape[1] // width):
            sl = slice(c * width, (c + 1) * width)
            dest_ref[k:k + 1, sl] = lookup(pstart, idx_ref[k:k + 1, sl]) + rank_ref[k:k + 1, sl]

    n_it = (padded + (ITEM_ROWS - 1)) // ITEM_ROWS
    it_end = _cumsum_sublanes(n_it)
    t = lax.broadcasted_iota(jnp.int32, (1, META_STRIDE), 1)
    e = count_le(it_end, t)
    ec = jnp.minimum(e, N_EXPERTS - 1)
    k = t - lookup(it_end - n_it, ec)
    rows_real = jnp.minimum(lookup(padded, ec) - k * ITEM_ROWS, ITEM_ROWS)
    row0_zero = total + (t - it_end[N_EXPERTS - 1:N_EXPERTS, 0:1]) * ITEM_ROWS
    rows_zero = jnp.clip(n_rows - row0_zero, 0, ITEM_ROWS)
    real = e < N_EXPERTS
    meta_ref[:, 0:META_STRIDE] = ec
    meta_ref[:, META_STRIDE:2 * META_STRIDE] = jnp.where(
        real, lookup(pstart, ec) + k * ITEM_ROWS, jnp.minimum(row0_zero, n_rows - MOE_BLOCK))
    meta_ref[:, 2 * META_STRIDE:3 * META_STRIDE] = jnp.where(real, rows_real, rows_zero) // MOE_BLOCK
    meta_ref[:, 3 * META_STRIDE:4 * META_STRIDE] = jnp.where(real, 0, 1)

    gap = padded - cnt
    gap_end = _cumsum_sublanes(gap)
    q = lax.broadcasted_iota(jnp.int32, pad_ref.shape, 1)
    eq = count_le(gap_end, q)
    inside = lookup(pstart + cnt - (gap_end - gap), eq) + q
    tail = total + q - gap_end[N_EXPERTS - 1:N_EXPERTS, 0:1]
    pad_ref[...] = jnp.where(eq < N_EXPERTS, inside, tail)


def _max_items(n_rows):
    return N_EXPERTS + n_rows // ITEM_ROWS + (N_EXPERTS * MOE_BLOCK) // ITEM_ROWS + 1


def _tables(cnt, idx_t, rank_t, n_rows):
    n_pad = n_rows - idx_t.size
    assert _max_items(n_rows) <= META_STRIDE
    dest, meta, pad_dest = pl.pallas_call(
        functools.partial(_tables_body, n_rows=n_rows),
        out_shape=[
            jax.ShapeDtypeStruct(idx_t.shape, jnp.int32),
            jax.ShapeDtypeStruct((1, 4 * META_STRIDE), jnp.int32),
            jax.ShapeDtypeStruct((1, n_pad), jnp.int32),
        ],
        name="tables",
    )(cnt, idx_t, rank_t)
    return dest.reshape(-1), meta.reshape(-1), pad_dest.reshape(-1)


def kernel(x, norm_mix, w_in, conv_w, conv_b, w_a, b_a, w_x, b_x, lru_lambda, attn_out_norm, lru_out_norm, w_out, norm_ffn, w_router, b_router, w_gate_up, b_gate_up, w_down, b_down, norm_final):
    b, seq, d = x.shape
    n = b * seq
    x2 = x.reshape(n, d)

    proj = _in_proj(x2, norm_mix, w_in)
    proj3 = proj.reshape(b, seq, IN_COLS)

    slopes = jnp.asarray(2.0 ** (-8.0 * np.arange(1, N_HEADS + 1) / N_HEADS), F32)
    attn = _attention(proj3, slopes)

    def pair_blocks(w):
        w4 = w.reshape(-1, 2, LRU_BLOCK, LRU_BLOCK)
        z = jnp.zeros_like(w4[:, 0])
        top = jnp.concatenate([w4[:, 0], z], axis=2)
        bot = jnp.concatenate([z, w4[:, 1]], axis=2)
        return jnp.concatenate([top, bot], axis=1)

    w_bd = jnp.concatenate([pair_blocks(w_a), pair_blocks(w_x)], axis=2)
    rec = _rglru(proj3, conv_w, conv_b, w_bd, b_a, b_x, lru_lambda)

    x_mid = _out_proj(attn.reshape(n, ATTN_WIDTH), rec.reshape(n, LRU_WIDTH),
                      attn_out_norm, lru_out_norm, x2, w_out)

    xn, idx_t, gate_t, rank_t, cnt = _router(x_mid, norm_ffn, w_router.T, b_router)
    n_rows = n * TOP_K + N_EXPERTS * MOE_BLOCK
    dest_flat, meta, pad_dest = _tables(cnt, idx_t, rank_t, n_rows)

    x_sorted = _scatter(dest_flat, pad_dest, xn, n_rows)
    act = _gate_up(meta, x_sorted, w_gate_up, b_gate_up)
    y_buf = _down(meta, act, w_down, b_down)
    out = _combine(dest_flat, y_buf, x_mid, gate_t.T, norm_final, d // PACK_COLS)
    return out.reshape(b, seq, d)
```

```python
import functools

import numpy as np
import jax
import jax.numpy as jnp
from jax import lax
from jax.experimental import pallas as pl
from jax.experimental.pallas import tpu as pltpu

F32 = jnp.float32
BF16 = jnp.bfloat16
U32 = jnp.uint32
HIGH_HALF = np.uint32(0xFFFF0000)

D_MODEL = 2048
HEAD_DIM = 64
N_HEADS = 16
ATTN_WIDTH = N_HEADS * HEAD_DIM
LRU_WIDTH = D_MODEL - ATTN_WIDTH
LRU_BLOCK = 64
CONV_WIDTH = 4
LRU_C = 8.0
IN_COLS = 3 * ATTN_WIDTH + 2 * LRU_WIDTH
DILATIONS = (1, 4, 16)
ATTN_BLOCK = 128
UNITS_PER_TRIP = (15, 12, 8)
N_EXPERTS = 32
TOP_K = 4
D_FF = D_MODEL
SWIGLU_LIMIT = 7.0
SWIGLU_ALPHA = 1.702
MOE_BLOCK = 128
ITEM_ROWS = 16 * MOE_BLOCK
META_STRIDE = 128
RUN_BLOCKS = 4
SLOT_BLOCKS = (RUN_BLOCKS, RUN_BLOCKS, 2, 1)
EPS = 1e-6

LANES = 128
SUBLANES = 8
VMEM_LIMIT = 56 * 1024 * 1024


def _params(sem, vmem=VMEM_LIMIT):
    return pltpu.CompilerParams(dimension_semantics=sem, vmem_limit_bytes=vmem)


def _rms(xf, g):
    return xf * lax.rsqrt(jnp.mean(xf * xf, axis=-1, keepdims=True) + EPS) * g


def _pack_halves(x):
    half = x.shape[1] // 2
    lo = lax.bitcast_convert_type(x[:, :half].astype(BF16).astype(F32), U32) >> 16
    hi = lax.bitcast_convert_type(x[:, half:].astype(BF16).astype(F32), U32) & HIGH_HALF
    return lo | hi


def _unpack_halves(u):
    return (lax.bitcast_convert_type(u << 16, F32),
            lax.bitcast_convert_type(u & HIGH_HALF, F32))


def _in_proj_body(x_ref, g_ref, w_ref, o_ref, h_ref):
    @pl.when(pl.program_id(1) == 0)
    def _():
        h_ref[...] = _rms(x_ref[...], g_ref[...]).astype(BF16)

    o_ref[...] = jnp.dot(h_ref[...], w_ref[...].astype(BF16), preferred_element_type=F32)


def _in_proj(x2, g, w_in, tm=1024, tn=1024):
    n, d = x2.shape
    cols = w_in.shape[1]
    return pl.pallas_call(
        _in_proj_body,
        grid=(n // tm, cols // tn),
        in_specs=[
            pl.BlockSpec((tm, d), lambda i, j: (i, 0)),
            pl.BlockSpec((1, d), lambda i, j: (0, 0)),
            pl.BlockSpec((d, tn), lambda i, j: (0, j)),
        ],
        out_specs=pl.BlockSpec((tm, tn), lambda i, j: (i, j)),
        out_shape=jax.ShapeDtypeStruct((n, cols), F32),
        scratch_shapes=[pltpu.VMEM((tm, d), BF16)],
        compiler_params=_params(("parallel", "arbitrary")),
        name="in_proj",
    )(x2, g.reshape(1, d), w_in)


REGROUP = 4


def _attention_body(slope_ref, q_ref, k_ref, v_ref, o_ref, qh, kh, vh, acc, mx):
    seq = q_ref.shape[1]
    run = seq // REGROUP
    pair = pl.program_id(1)
    qi = lax.broadcasted_iota(jnp.int32, (ATTN_BLOCK, 2 * ATTN_BLOCK), 0)
    ki = lax.broadcasted_iota(jnp.int32, (ATTN_BLOCK, 2 * ATTN_BLOCK), 1)
    rel2 = qi + ATTN_BLOCK - ki
    ok2 = (rel2 >= 0) & (rel2 <= ATTN_BLOCK)
    rel1 = (qi - ki)[:, :ATTN_BLOCK]
    ok1 = rel1 >= 0

    nh = LANES // HEAD_DIM
    slopes = [slope_ref[pair * nh + hh] for hh in range(nh)]
    def stage(order, rows_dst, rows_src):
        lane = lax.broadcasted_iota(jnp.int32, (run, LANES), 1)
        qv, vv = q_ref[0, rows_src, :] * (HEAD_DIM ** -0.5), v_ref[0, rows_src, :]
        if order:
            kh[rows_dst, :] = k_ref[0, rows_src, :]
        for hh in range(nh):
            mine = (lane >= hh * HEAD_DIM) & (lane < (hh + 1) * HEAD_DIM)
            qh[order, hh, rows_dst, :] = jnp.where(mine, qv, 0.0)
            vh[order, hh, rows_dst, :] = jnp.where(mine, vv, 1.0)

    for c in range(REGROUP):
        stage(0, slice(c * run, (c + 1) * run), pl.ds(c * run, run))
        stage(1, slice(c * run, (c + 1) * run), pl.ds(c, run, stride=REGROUP))

    def run_units(units):
        loaded = [(qh[order, hh, qs, :].astype(BF16),
                   (kh[ks, :] if order else k_ref[0, ks, :]).astype(BF16),
                   vh[order, hh, ks, :].astype(BF16)) for hh, br, order, qs, ks, bias in units]
        scores = [lax.dot_general(q, k, (((1,), (1,)), ((), ())), preferred_element_type=F32)
                  + u[5] for (q, k, v), u in zip(loaded, units)]
        maxes = [jnp.max(s, axis=1, keepdims=True) for s in scores]
        probs = [jnp.exp(s - m).astype(BF16) for s, m in zip(scores, maxes)]
        results = [(jnp.dot(p, v, preferred_element_type=F32), m)
                   for p, (q, k, v), m in zip(probs, loaded, maxes)]
        for (a, m), (hh, br, order, qs, ks, bias) in zip(results, units):
            acc[hh, br, qs, :] = a
            mx[hh, br, qs, :] = jnp.broadcast_to(m, (ATTN_BLOCK, LANES))

    def bias_pair(ok, rel, d):
        return [jnp.where(ok, -sl * (rel * d).astype(F32), -jnp.inf) for sl in slopes]

    def contiguous_branch(br, order, d, n_runs, per_trip):
        nb = seq // (n_runs * ATTN_BLOCK)
        bias1, bias2 = bias_pair(ok1, rel1, d), bias_pair(ok2, rel2, d)
        run_units([(hh, br, order, pl.ds(r * nb * ATTN_BLOCK, ATTN_BLOCK),
                    pl.ds(r * nb * ATTN_BLOCK, ATTN_BLOCK), bias1[hh])
                   for r in range(n_runs) for hh in range(nh)])

        def later(g, carry):
            units = []
            for u in range(per_trip):
                idx = g * per_trip + u
                blk = idx // (nb - 1) * nb + idx % (nb - 1) + 1
                start = pl.multiple_of(blk * ATTN_BLOCK, ATTN_BLOCK)
                units += [(hh, br, order, pl.ds(start, ATTN_BLOCK),
                           pl.ds(start - ATTN_BLOCK, 2 * ATTN_BLOCK), bias2[hh])
                          for hh in range(nh)]
            run_units(units)
            return carry

        lax.fori_loop(0, n_runs * (nb - 1) // per_trip, later, 0)

    contiguous_branch(0, 0, DILATIONS[0], 1, UNITS_PER_TRIP[0])
    contiguous_branch(1, 1, DILATIONS[1], REGROUP, UNITS_PER_TRIP[1])

    d2 = DILATIONS[2]
    sub = d2 // REGROUP
    bias16 = bias_pair(ok1, rel1, d2)

    def strided(g, carry):
        units = []
        for u in range(UNITS_PER_TRIP[2]):
            idx = g * UNITS_PER_TRIP[2] + u
            rows = pl.ds(idx // sub * run + idx % sub, ATTN_BLOCK, stride=sub)
            units += [(hh, 2, 1, rows, rows, bias16[hh]) for hh in range(nh)]
        run_units(units)
        return carry

    lax.fori_loop(0, d2 // UNITS_PER_TRIP[2], strided, 0)

    def merge(c, carry):
        start = c // (run // ATTN_BLOCK) + c % (run // ATTN_BLOCK) * (ATTN_BLOCK * REGROUP)
        nat = pl.ds(start, ATTN_BLOCK, stride=REGROUP)
        reg = pl.ds(pl.multiple_of(c * ATTN_BLOCK, ATTN_BLOCK), ATTN_BLOCK)
        outs = []
        for hh in range(nh):
            m0, m1, m2 = mx[hh, 0, nat, :], mx[hh, 1, reg, :], mx[hh, 2, reg, :]
            mt = jnp.maximum(jnp.maximum(m0, m1), m2)
            tot = (jnp.exp(m0 - mt) * acc[hh, 0, nat, :] + jnp.exp(m1 - mt) * acc[hh, 1, reg, :]
                   + jnp.exp(m2 - mt) * acc[hh, 2, reg, :])
            den = pltpu.roll(tot, HEAD_DIM, axis=1)
            outs.append(tot / den)
        lane_r = lax.broadcasted_iota(jnp.int32, (ATTN_BLOCK, LANES), 1)
        o_ref[0, nat, :] = jnp.where(lane_r < HEAD_DIM, outs[0], outs[1])
        return carry

    lax.fori_loop(0, seq // ATTN_BLOCK, merge, 0, unroll=2)


def _attention(proj3, slopes):
    b, seq, _ = proj3.shape
    npair = ATTN_WIDTH // LANES
    nh = LANES // HEAD_DIM
    blk = (1, seq, LANES)
    return pl.pallas_call(
        _attention_body,
        grid_spec=pltpu.PrefetchScalarGridSpec(
            num_scalar_prefetch=1,
            grid=(b, npair),
            in_specs=[
                pl.BlockSpec(blk, lambda i, j, s: (i, 0, j)),
                pl.BlockSpec(blk, lambda i, j, s: (i, 0, npair + j)),
                pl.BlockSpec(blk, lambda i, j, s: (i, 0, 2 * npair + j)),
            ],
            out_specs=pl.BlockSpec(blk, lambda i, j, s: (i, 0, j)),
            scratch_shapes=[
                pltpu.VMEM((2, nh, seq, LANES), F32),
                pltpu.VMEM((seq, LANES), F32),
                pltpu.VMEM((2, nh, seq, LANES), F32),
                pltpu.VMEM((nh, len(DILATIONS), seq, LANES), F32),
                pltpu.VMEM((nh, len(DILATIONS), seq, LANES), F32),
            ],
        ),
        out_shape=jax.ShapeDtypeStruct((b, seq, ATTN_WIDTH), F32),
        compiler_params=_params(("parallel", "parallel")),
        name="attention",
    )(slopes, proj3, proj3, proj3)


def _rglru_body(xr_ref, gr_ref, cw_ref, cb_ref, w_ref, ba_ref, bx_ref, lam_ref, o_ref,
                xp, a_s, b_s):
    seq = xr_ref.shape[1]
    pad = SUBLANES
    xp[0:pad, :] = jnp.zeros((pad, LANES), F32)
    xp[pad:pad + seq, :] = xr_ref[0]
    lam = lam_ref[...]
    sp = jnp.maximum(-lam, 0.0) + jnp.log(1.0 + jnp.exp(-jnp.abs(lam)))
    w_hi = w_ref[0].astype(BF16)
    w_lo = (w_ref[0] - w_hi.astype(F32)).astype(BF16)
    rows = 256

    def gates(c, carry):
        base = pl.multiple_of(c * rows, rows)
        xc = cb_ref[...] + cw_ref[0:1, :] * xp[pl.ds(base + pad - 3, rows), :]
        for i in range(1, CONV_WIDTH):
            xc = xc + cw_ref[i:i + 1, :] * xp[pl.ds(base + pad - 3 + i, rows), :]
        hi = xc.astype(BF16)
        lo = (xc - hi.astype(F32)).astype(BF16)
        pre = (jnp.dot(hi, w_hi, preferred_element_type=F32)
               + jnp.dot(lo, w_hi, preferred_element_type=F32)
               + jnp.dot(hi, w_lo, preferred_element_type=F32))
        r = jax.nn.sigmoid(pre[:, :LANES] + ba_ref[...])
        ig = jax.nn.sigmoid(pre[:, LANES:] + bx_ref[...])
        log_a = -LRU_C * r * sp
        a = jnp.exp(log_a)
        t = jnp.tanh(log_a)
        b = jnp.sqrt(-2.0 * t / (1.0 - t)) * (ig * xc)
        a_s[pl.ds(base, rows), :] = a
        b_s[pl.ds(base, rows), :] = b
        return carry

    lax.fori_loop(0, seq // rows, gates, 0)

    row = lax.broadcasted_iota(jnp.int32, (SUBLANES, LANES), 0)

    def scan(c, h_prev):
        sl = pl.ds(pl.multiple_of(c * SUBLANES, SUBLANES), SUBLANES)
        a = a_s[sl, :]
        b = b_s[sl, :]
        for s in (1, 2, 4):
            keep = row >= s
            a_sh = jnp.where(keep, pltpu.roll(a, s, axis=0), 1.0)
            b_sh = jnp.where(keep, pltpu.roll(b, s, axis=0), 0.0)
            b = a * b_sh + b
            a = a * a_sh
        h = a * h_prev + b
        o_ref[0, sl, :] = h * jax.nn.gelu(gr_ref[0, sl, :])
        return jnp.broadcast_to(h[SUBLANES - 1:SUBLANES, :], (SUBLANES, LANES))

    lax.fori_loop(0, seq // SUBLANES, scan, jnp.zeros((SUBLANES, LANES), F32), unroll=8)


def _rglru(proj3, conv_w, conv_b, w_bd, b_a, b_x, lam):
    b, seq, _ = proj3.shape
    nt = LRU_WIDTH // LANES
    xr0 = 3 * ATTN_WIDTH // LANES
    gr0 = xr0 + nt
    blk = (1, seq, LANES)
    vec = lambda: pl.BlockSpec((1, LANES), lambda i, j: (0, j))
    return pl.pallas_call(
        _rglru_body,
        grid=(b, nt),
        in_specs=[
            pl.BlockSpec(blk, lambda i, j: (i, 0, xr0 + j)),
            pl.BlockSpec(blk, lambda i, j: (i, 0, gr0 + j)),
            pl.BlockSpec((CONV_WIDTH, LANES), lambda i, j: (0, j)),
            vec(),
            pl.BlockSpec((1, LANES, 2 * LANES), lambda i, j: (j, 0, 0)),
            vec(), vec(), vec(),
        ],
        out_specs=pl.BlockSpec(blk, lambda i, j: (i, 0, j)),
        out_shape=jax.ShapeDtypeStruct((b, seq, LRU_WIDTH), F32),
        scratch_shapes=[
            pltpu.VMEM((seq + SUBLANES, LANES), F32),
            pltpu.VMEM((seq, LANES), F32),
            pltpu.VMEM((seq, LANES), F32),
        ],
        compiler_params=_params(("parallel", "parallel")),
        name="rglru",
    )(proj3, proj3, conv_w, conv_b.reshape(1, -1), w_bd, b_a.reshape(1, -1),
      b_x.reshape(1, -1), lam.reshape(1, -1))


def _out_proj_body(at_ref, rc_ref, ga_ref, gr_ref, x_ref, w_ref, o_ref, h_ref):
    wa = at_ref.shape[1]

    @pl.when(pl.program_id(1) == 0)
    def _():
        h_ref[:, :wa] = _rms(at_ref[...], ga_ref[...]).astype(BF16)
        h_ref[:, wa:] = _rms(rc_ref[...], gr_ref[...]).astype(BF16)

    o_ref[...] = x_ref[...] + jnp.dot(h_ref[...], w_ref[...].astype(BF16),
                                      preferred_element_type=F32)


def _out_proj(attn2, rec2, g_attn, g_rec, x2, w_out, tm=1024, tn=512):
    n, d = x2.shape
    wa, wr = attn2.shape[1], rec2.shape[1]
    return pl.pallas_call(
        _out_proj_body,
        grid=(n // tm, d // tn),
        in_specs=[
            pl.BlockSpec((tm, wa), lambda i, j: (i, 0)),
            pl.BlockSpec((tm, wr), lambda i, j: (i, 0)),
            pl.BlockSpec((1, wa), lambda i, j: (0, 0)),
            pl.BlockSpec((1, wr), lambda i, j: (0, 0)),
            pl.BlockSpec((tm, tn), lambda i, j: (i, j)),
            pl.BlockSpec((d, tn), lambda i, j: (0, j)),
        ],
        out_specs=pl.BlockSpec((tm, tn), lambda i, j: (i, j)),
        out_shape=jax.ShapeDtypeStruct((n, d), F32),
        scratch_shapes=[pltpu.VMEM((tm, d), BF16)],
        compiler_params=_params(("parallel", "arbitrary")),
        name="out_proj",
    )(attn2, rec2, g_attn.reshape(1, wa), g_rec.reshape(1, wr), x2, w_out)


def _router_body(x_ref, g_ref, wt_ref, b_ref, xn_ref, idx_ref, gate_ref, rank_ref, cnt_ref,
                 base):
    tm = x_ref.shape[0]

    @pl.when(pl.program_id(0) == 0)
    def _():
        base[...] = jnp.zeros_like(base)

    xn = _rms(x_ref[...], g_ref[...])
    xn_ref[...] = _pack_halves(xn)
    x_hi = xn.astype(BF16)
    x_lo = (xn - x_hi.astype(F32)).astype(BF16)
    w = wt_ref[...]
    w_hi = w.astype(BF16)
    w_lo = (w - w_hi.astype(F32)).astype(BF16)
    nt = (((1,), (1,)), ((), ()))
    logits = (lax.dot_general(w_hi, x_hi, nt, preferred_element_type=F32)
              + lax.dot_general(w_hi, x_lo, nt, preferred_element_type=F32)
              + lax.dot_general(w_lo, x_hi, nt, preferred_element_type=F32)
              + b_ref[...])

    eid = lax.broadcasted_iota(jnp.int32, (N_EXPERTS, tm), 0)
    work = logits
    vals, hots = [], []
    for _ in range(TOP_K):
        best = jnp.max(work, axis=0, keepdims=True)
        pick = jnp.min(jnp.where(work == best, eid, N_EXPERTS), axis=0, keepdims=True)
        hot = eid == pick
        vals.append(best)
        hots.append(hot)
        work = jnp.where(hot, -jnp.inf, work)
        idx_ref[len(vals) - 1:len(vals), :] = pick

    ex = [jnp.exp(v - vals[0]) for v in vals]
    den = ex[0] + ex[1] + ex[2] + ex[3]
    for k in range(TOP_K):
        gate_ref[k:k + 1, :] = ex[k] / den

    chosen = (hots[0] | hots[1] | hots[2] | hots[3])
    si = lax.broadcasted_iota(jnp.int32, (tm, tm), 0)
    ti = lax.broadcasted_iota(jnp.int32, (tm, tm), 1)
    before = (si < ti).astype(BF16)
    prefix = jnp.dot(chosen.astype(BF16), before, preferred_element_type=F32)
    slot = base[:, 0:1] + prefix
    for k in range(TOP_K):
        rank_ref[k:k + 1, :] = jnp.sum(jnp.where(hots[k], slot, 0.0), axis=0,
                                       keepdims=True).astype(jnp.int32)
    base[...] = base[...] + jnp.sum(chosen.astype(F32), axis=1, keepdims=True)
    cnt_ref[...] = base[...].astype(jnp.int32)


def _router(x2, g, w_router_t, b_router, tm=256):
    n, d = x2.shape
    row = lambda: pl.BlockSpec((TOP_K, tm), lambda i: (0, i))
    return pl.pallas_call(
        _router_body,
        grid=(n // tm,),
        in_specs=[
            pl.BlockSpec((tm, d), lambda i: (i, 0)),
            pl.BlockSpec((1, d), lambda i: (0, 0)),
            pl.BlockSpec((N_EXPERTS, d), lambda i: (0, 0)),
            pl.BlockSpec((N_EXPERTS, 1), lambda i: (0, 0)),
        ],
        out_specs=[
            pl.BlockSpec((tm, d // 2), lambda i: (i, 0)),
            row(), row(), row(),
            pl.BlockSpec((N_EXPERTS, LANES), lambda i: (0, 0)),
        ],
        out_shape=[
            jax.ShapeDtypeStruct((n, d // 2), U32),
            jax.ShapeDtypeStruct((TOP_K, n), jnp.int32),
            jax.ShapeDtypeStruct((TOP_K, n), F32),
            jax.ShapeDtypeStruct((TOP_K, n), jnp.int32),
            jax.ShapeDtypeStruct((N_EXPERTS, LANES), jnp.int32),
        ],
        scratch_shapes=[pltpu.VMEM((N_EXPERTS, LANES), F32)],
        compiler_params=_params(("arbitrary",)),
        name="router",
    )(x2, g.reshape(1, d), w_router_t, b_router.reshape(N_EXPERTS, 1))


def _scatter_body(dest_ref, pad_ref, xn_ref, o_ref, zrow, sem, *, n_tokens, pad_per_step):
    tm = xn_ref.shape[0]
    i = pl.program_id(0)

    @pl.when(i == 0)
    def _():
        zrow[...] = jnp.zeros_like(zrow)

    def row_copy(t, k):
        d = dest_ref[k * n_tokens + i * tm + t]
        return pltpu.make_async_copy(xn_ref.at[pl.ds(t, 1), :], o_ref.at[pl.ds(d, 1), :], sem)

    def pad_copy(q):
        d = pad_ref[i * pad_per_step + q]
        return pltpu.make_async_copy(zrow.at[pl.ds(0, 1), :], o_ref.at[pl.ds(d, 1), :], sem)

    def issue(t, c):
        for k in range(TOP_K):
            row_copy(t, k).start(priority=k % 2)
        return c

    lax.fori_loop(0, tm, issue, 0)

    def issue_pad(q, c):
        pad_copy(q).start()
        return c

    lax.fori_loop(0, pad_per_step, issue_pad, 0)

    def drain(t, c):
        for k in range(TOP_K):
            row_copy(t, k).wait()
        return c

    lax.fori_loop(0, tm, drain, 0)

    def drain_pad(q, c):
        pad_copy(q).wait()
        return c

    lax.fori_loop(0, pad_per_step, drain_pad, 0)


def _scatter(dest_flat, pad_dest, xn, n_rows, tm=256):
    n, d = xn.shape
    steps = n // tm
    pad_per_step = pad_dest.shape[0] // steps
    return pl.pallas_call(
        functools.partial(_scatter_body, n_tokens=n, pad_per_step=pad_per_step),
        grid_spec=pltpu.PrefetchScalarGridSpec(
            num_scalar_prefetch=2,
            grid=(steps,),
            in_specs=[pl.BlockSpec((tm, d), lambda i, *_: (i, 0))],
            out_specs=pl.BlockSpec(memory_space=pl.ANY),
            scratch_shapes=[pltpu.VMEM((SUBLANES, d), xn.dtype), pltpu.SemaphoreType.DMA(())],
        ),
        out_shape=jax.ShapeDtypeStruct((n_rows, d), xn.dtype),
        compiler_params=_params(("arbitrary",)),
        name="scatter",
    )(dest_flat, pad_dest, xn)


def _moe_body(meta_ref, x_hbm, w_ref, b_ref, o_hbm, xbf, stage, wbf, obuf, sem_in, sem_out,
              pend, *, prep_w, epilogue):
    t, j = pl.program_id(0), pl.program_id(1)
    row0 = meta_ref[META_STRIDE + t]
    nsub = meta_ref[2 * META_STRIDE + t]
    zero = meta_ref[3 * META_STRIDE + t]
    tn_out = obuf.shape[2]
    col0 = pl.multiple_of(j * tn_out, tn_out)

    @pl.when((t == 0) & (j == 0))
    def _():
        for k in range(len(SLOT_BLOCKS)):
            pend[k] = 0

    def in_copy(c, slot):
        src = x_hbm.at[pl.ds(pl.multiple_of(row0 + c * MOE_BLOCK, MOE_BLOCK), MOE_BLOCK), :]
        if stage is None:
            dst = xbf.at[pl.ds(pl.multiple_of(c * MOE_BLOCK, MOE_BLOCK), MOE_BLOCK), :]
        else:
            dst = stage.at[slot]
        return pltpu.make_async_copy(src, dst, sem_in.at[slot])

    def fetch_rows(c):
        slot = c % 2

        @pl.when(c + 1 < nsub)
        def _():
            in_copy(c + 1, 1 - slot).start()

        in_copy(c, slot).wait()
        if stage is not None:
            rows = pl.ds(pl.multiple_of(c * MOE_BLOCK, MOE_BLOCK), MOE_BLOCK)
            lo, hi = _unpack_halves(stage[slot])
            xbf[rows, :lo.shape[1]] = lo.astype(BF16)
            xbf[rows, lo.shape[1]:] = hi.astype(BF16)

    def out_copy(first_block, slot, n_blocks):
        n_rows = n_blocks * MOE_BLOCK
        rows = pl.ds(pl.multiple_of(row0 + first_block * MOE_BLOCK, MOE_BLOCK), n_rows)
        return pltpu.make_async_copy(obuf.at[slot, pl.ds(0, n_rows), :],
                                     o_hbm.at[rows, pl.ds(col0, tn_out)], sem_out.at[slot])

    def emit(first_block, slot, n_blocks, value_fn):
        @pl.when(pend[slot] == 1)
        def _():
            out_copy(first_block, slot, n_blocks).wait()

        obuf[slot, pl.ds(0, n_blocks * MOE_BLOCK), :] = value_fn()
        out_copy(first_block, slot, n_blocks).start()
        pend[slot] = 1

    def walk(block_fn):
        def full(p, carry):
            block_fn(p * RUN_BLOCKS, p % 2, RUN_BLOCKS)
            return carry

        lax.fori_loop(0, nsub // RUN_BLOCKS, full, 0)
        done = nsub // RUN_BLOCKS * RUN_BLOCKS
        for slot in range(2, len(SLOT_BLOCKS)):
            n_blocks = SLOT_BLOCKS[slot]

            @pl.when((nsub - done) & n_blocks != 0)
            def _():
                block_fn(done + ((nsub - done) & ~(2 * n_blocks - 1)), slot, n_blocks)

    @pl.when((nsub > 0) & (zero == 0))
    def _compute():
        @pl.when(j == 0)
        def _():
            in_copy(0, 0).start()

        prep_w(w_ref, wbf)

        def block(first_block, slot, n_blocks):
            @pl.when(j == 0)
            def _():
                for k in range(n_blocks):
                    fetch_rows(first_block + k)

            rows = pl.ds(pl.multiple_of(first_block * MOE_BLOCK, MOE_BLOCK), n_blocks * MOE_BLOCK)
            emit(first_block, slot, n_blocks,
                 lambda: epilogue(xbf[rows, :], w_ref if prep_w.func is _keep_weights else wbf, b_ref))

        walk(block)

    @pl.when((nsub > 0) & (zero == 1))
    def _zeros():
        walk(lambda first_block, slot, n_blocks: emit(
            first_block, slot, n_blocks,
            lambda: jnp.zeros((n_blocks * MOE_BLOCK, tn_out), obuf.dtype)))

    @pl.when((t == pl.num_programs(0) - 1) & (j == pl.num_programs(1) - 1))
    def _drain():
        for k, n_blocks in enumerate(SLOT_BLOCKS):
            @pl.when(pend[k] == 1)
            def _():
                out_copy(0, k, n_blocks).wait()


def _moe_call(name, meta, x, w, b, out_cols, out_dtype, tn, tn_out, prep_w, epilogue, tmp_shape):
    p, kdim = x.shape[0], w.shape[1]
    n_items = _max_items(p)
    n_tiles = w.shape[2] // tn
    needs_stage = x.dtype != BF16

    def wmap(t, j, m):
        live = (m[2 * META_STRIDE + t] > 0) & (m[3 * META_STRIDE + t] == 0)
        return (m[t], 0, jnp.where(live, j, n_tiles - 1))

    scratch = [pltpu.VMEM((ITEM_ROWS, kdim), BF16)]
    if needs_stage:
        scratch.append(pltpu.VMEM((2, MOE_BLOCK, x.shape[1]), x.dtype))
    scratch += [
        pltpu.VMEM((kdim, tn), BF16),
        pltpu.VMEM((len(SLOT_BLOCKS), RUN_BLOCKS * MOE_BLOCK, tn_out), out_dtype),
        pltpu.SemaphoreType.DMA((2,)),
        pltpu.SemaphoreType.DMA((len(SLOT_BLOCKS),)),
        pltpu.SMEM((len(SLOT_BLOCKS),), jnp.int32),
    ]
    if tmp_shape is not None:
        scratch.append(pltpu.VMEM(tmp_shape, F32))

    def body(meta_ref, x_hbm, w_ref, b_ref, o_hbm, xbf, *rest):
        rest = list(rest)
        stage = rest.pop(0) if needs_stage else None
        wbf, obuf, sem_in, sem_out, pend = rest[:5]
        tmp = rest[5] if tmp_shape is not None else None
        _moe_body(meta_ref, x_hbm, w_ref, b_ref, o_hbm, xbf, stage, wbf, obuf, sem_in, sem_out,
                  pend, prep_w=functools.partial(prep_w, tmp=tmp), epilogue=epilogue)

    return pl.pallas_call(
        body,
        grid_spec=pltpu.PrefetchScalarGridSpec(
            num_scalar_prefetch=1,
            grid=(n_items, n_tiles),
            in_specs=[
                pl.BlockSpec(memory_space=pl.ANY),
                pl.BlockSpec((1, kdim, tn), wmap),
                pl.BlockSpec((1, 1, tn), wmap),
            ],
            out_specs=pl.BlockSpec(memory_space=pl.ANY),
            scratch_shapes=scratch,
        ),
        out_shape=jax.ShapeDtypeStruct((p, out_cols), out_dtype),
        compiler_params=_params(("arbitrary", "arbitrary")),
        name=name,
    )(meta, x, w, b.reshape(N_EXPERTS, 1, -1))


def _keep_weights(w_ref, wbf, tmp=None):
    pass


def _swiglu_tile(x, w_ref, b_ref):
    even = lax.broadcasted_iota(jnp.int32, (x.shape[0], LANES), 1) % 2 == 0
    cols = []
    for c in range(w_ref.shape[2] // (2 * LANES)):
        c0 = c * 2 * LANES
        gu = jnp.dot(x, w_ref[0, :, c0:c0 + 2 * LANES].astype(BF16), preferred_element_type=F32)
        gu = gu + b_ref[0, :, c0:c0 + 2 * LANES]
        v1, v2 = gu[:, :LANES], gu[:, LANES:]
        gate = jnp.where(even, v1, pltpu.roll(v2, 1, axis=1))
        up = jnp.where(even, pltpu.roll(v1, LANES - 1, axis=1), v2)
        gate = jnp.minimum(gate, SWIGLU_LIMIT)
        up = jnp.clip(up, -SWIGLU_LIMIT, SWIGLU_LIMIT)
        cols.append((gate * jax.nn.sigmoid(SWIGLU_ALPHA * gate) * (up + 1.0)).astype(BF16))
    return jnp.concatenate(cols, axis=1)


def _gate_up(meta, x_sorted, w_gate_up, b_gate_up, tn=1024):
    return _moe_call("gate_up", meta, x_sorted, w_gate_up, b_gate_up, w_gate_up.shape[2] // 2,
                     BF16, tn, tn // 2, _keep_weights, _swiglu_tile, None)


def _permute_weights(w_ref, wbf, tmp):
    half = LANES // 2

    def permute(g, carry):
        base = pl.multiple_of(g * LANES, LANES)
        for c in range(tmp.shape[0]):
            cs = slice(c * LANES, (c + 1) * LANES)
            tmp[c, pl.ds(0, half, stride=2), :] = w_ref[0, pl.ds(base, half), cs]
            tmp[c, pl.ds(1, half, stride=2), :] = w_ref[0, pl.ds(base + half, half), cs]
            wbf[pl.ds(base, LANES), cs] = tmp[c].astype(BF16)
        return carry

    lax.fori_loop(0, wbf.shape[0] // LANES, permute, 0)


def _linear_tile(a, wbf, b_ref):
    return _pack_halves(jnp.dot(a, wbf[...], preferred_element_type=F32) + b_ref[0])


DOWN_TILE = 1024


def _down(meta, act, w_down, b_down, tn=DOWN_TILE):
    return _moe_call("down", meta, act, w_down, b_down, w_down.shape[2] // 2, U32, tn, tn // 2,
                     _permute_weights, _linear_tile, (tn // LANES, LANES, LANES))


def _combine_body(dest_ref, y_ref, x_ref, gate_ref, g_ref, o_ref, ybuf, sem, *, n_tokens,
                  y_tiles):
    tm = x_ref.shape[0]
    i = pl.program_id(0)

    def row_copy(t, k):
        d = dest_ref[k * n_tokens + i * tm + t]
        return pltpu.make_async_copy(y_ref.at[pl.ds(d, 1), :], ybuf.at[k, pl.ds(t, 1), :], sem)

    def issue(t, c):
        for k in range(TOP_K):
            row_copy(t, k).start(priority=k % 2)
        return c

    lax.fori_loop(0, tm, issue, 0)

    def drain(t, c):
        for k in range(TOP_K):
            row_copy(t, k).wait()
        return c

    lax.fori_loop(0, tm, drain, 0)

    acc = x_ref[...]
    tile = x_ref.shape[1] // y_tiles
    for k in range(TOP_K):
        parts = []
        for c in range(y_tiles):
            parts += _unpack_halves(ybuf[k, :, c * tile // 2:(c + 1) * tile // 2])
        acc = acc + gate_ref[:, k:k + 1] * jnp.concatenate(parts, axis=1)
    o_ref[...] = _rms(acc, g_ref[...])


def _combine(dest_flat, y_buf, x2, gates, g, y_tiles, tm=256):
    n, d = x2.shape
    return pl.pallas_call(
        functools.partial(_combine_body, n_tokens=n, y_tiles=y_tiles),
        grid_spec=pltpu.PrefetchScalarGridSpec(
            num_scalar_prefetch=1,
            grid=(n // tm,),
            in_specs=[
                pl.BlockSpec(memory_space=pl.ANY),
                pl.BlockSpec((tm, d), lambda i, *_: (i, 0)),
                pl.BlockSpec((tm, TOP_K), lambda i, *_: (i, 0)),
                pl.BlockSpec((1, d), lambda i, *_: (0, 0)),
            ],
            out_specs=pl.BlockSpec((tm, d), lambda i, *_: (i, 0)),
            scratch_shapes=[pltpu.VMEM((TOP_K, tm) + y_buf.shape[1:], y_buf.dtype),
                            pltpu.SemaphoreType.DMA(())],
        ),
        out_shape=jax.ShapeDtypeStruct((n, d), F32),
        compiler_params=_params(("arbitrary",)),
        name="combine",
    )(dest_flat, y_buf, x2, gates, g.reshape(1, d))


def _cumsum_sublanes(x):
    row = lax.broadcasted_iota(jnp.int32, x.shape, 0)
    s = 1
    while s < x.shape[0]:
        x = x + jnp.where(row >= s, pltpu.roll(x, s, axis=0), 0)
        s *= 2
    return x


def _tables_body(cnt_ref, idx_ref, rank_ref, dest_ref, meta_ref, pad_ref, *, n_rows):
    cnt = cnt_ref[...]
    padded = (cnt + (MOE_BLOCK - 1)) // MOE_BLOCK * MOE_BLOCK
    pend = _cumsum_sublanes(padded)
    pstart = pend - padded
    total = pend[N_EXPERTS - 1:N_EXPERTS, 0:1]

    def lookup(table, sel):
        eid = lax.broadcasted_iota(jnp.int32, (N_EXPERTS, sel.shape[1]), 0)
        return jnp.sum(jnp.where(eid == sel, table[:, 0:1], 0), axis=0, keepdims=True)

    def count_le(table, v):
        return jnp.sum((table[:, 0:1] <= v).astype(jnp.int32), axis=0, keepdims=True)

    width = 2048
    for k in range(TOP_K):
        for c in range(idx_ref.shape[1] // width):
            sl = slice(c * width, (c + 1) * width)
            dest_ref[k:k + 1, sl] = lookup(pstart, idx_ref[k:k + 1, sl]) + rank_ref[k:k + 1, sl]

    n_it = (padded + (ITEM_ROWS - 1)) // ITEM_ROWS
    it_end = _cumsum_sublanes(n_it)
    t = lax.broadcasted_iota(jnp.int32, (1, META_STRIDE), 1)
    e = count_le(it_end, t)
    ec = jnp.minimum(e, N_EXPERTS - 1)
    k = t - lookup(it_end - n_it, ec)
    rows_real = jnp.minimum(lookup(padded, ec) - k * ITEM_ROWS, ITEM_ROWS)
    row0_zero = total + (t - it_end[N_EXPERTS - 1:N_EXPERTS, 0:1]) * ITEM_ROWS
    rows_zero = jnp.clip(n_rows - row0_zero, 0, ITEM_ROWS)
    real = e < N_EXPERTS
    meta_ref[:, 0:META_STRIDE] = ec
    meta_ref[:, META_STRIDE:2 * META_STRIDE] = jnp.where(
        real, lookup(pstart, ec) + k * ITEM_ROWS, jnp.minimum(row0_zero, n_rows - MOE_BLOCK))
    meta_ref[:, 2 * META_STRIDE:3 * META_STRIDE] = jnp.where(real, rows_real, rows_zero) // MOE_BLOCK
    meta_ref[:, 3 * META_STRIDE:4 * META_STRIDE] = jnp.where(real, 0, 1)

    gap = padded - cnt
    gap_end = _cumsum_sublanes(gap)
    q = lax.broadcasted_iota(jnp.int32, pad_ref.shape, 1)
    eq = count_le(gap_end, q)
    inside = lookup(pstart + cnt - (gap_end - gap), eq) + q
    tail = total + q - gap_end[N_EXPERTS - 1:N_EXPERTS, 0:1]
    pad_ref[...] = jnp.where(eq < N_EXPERTS, inside, tail)


def _max_items(n_rows):
    return N_EXPERTS + n_rows // ITEM_ROWS + (N_EXPERTS * MOE_BLOCK) // ITEM_ROWS + 1


def _tables(cnt, idx_t, rank_t, n_rows):
    n_pad = n_rows - idx_t.size
    assert _max_items(n_rows) <= META_STRIDE
    dest, meta, pad_dest = pl.pallas_call(
        functools.partial(_tables_body, n_rows=n_rows),
        out_shape=[
            jax.ShapeDtypeStruct(idx_t.shape, jnp.int32),
            jax.ShapeDtypeStruct((1, 4 * META_STRIDE), jnp.int32),
            jax.ShapeDtypeStruct((1, n_pad), jnp.int32),
        ],
        name="tables",
    )(cnt, idx_t, rank_t)
    return dest.reshape(-1), meta.reshape(-1), pad_dest.reshape(-1)


def kernel(x, norm_mix, w_in, conv_w, conv_b, w_a, b_a, w_x, b_x, lru_lambda, attn_out_norm, lru_out_norm, w_out, norm_ffn, w_router, b_router, w_gate_up, b_gate_up, w_down, b_down, norm_final):
    b, seq, d = x.shape
    n = b * seq
    x2 = x.reshape(n, d)

    proj = _in_proj(x2, norm_mix, w_in)
    proj3 = proj.reshape(b, seq, IN_COLS)

    slopes = jnp.asarray(2.0 ** (-8.0 * np.arange(1, N_HEADS + 1) / N_HEADS), F32)
    attn = _attention(proj3, slopes)

    def pair_blocks(w):
        w4 = w.reshape(-1, 2, LRU_BLOCK, LRU_BLOCK)
        z = jnp.zeros_like(w4[:, 0])
        top = jnp.concatenate([w4[:, 0], z], axis=2)
        bot = jnp.concatenate([z, w4[:, 1]], axis=2)
        return jnp.concatenate([top, bot], axis=1)

    w_bd = jnp.concatenate([pair_blocks(w_a), pair_blocks(w_x)], axis=2)
    rec = _rglru(proj3, conv_w, conv_b, w_bd, b_a, b_x, lru_lambda)

    x_mid = _out_proj(attn.reshape(n, ATTN_WIDTH), rec.reshape(n, LRU_WIDTH),
                      attn_out_norm, lru_out_norm, x2, w_out)

    xn, idx_t, gate_t, rank_t, cnt = _router(x_mid, norm_ffn, w_router.T, b_router)
    n_rows = n * TOP_K + N_EXPERTS * MOE_BLOCK
    dest_flat, meta, pad_dest = _tables(cnt, idx_t, rank_t, n_rows)

    x_sorted = _scatter(dest_flat, pad_dest, xn, n_rows)
    act = _gate_up(meta, x_sorted, w_gate_up, b_gate_up)
    y_buf = _down(meta, act, w_down, b_down)
    out = _combine(dest_flat, y_buf, x_mid, gate_t.T, norm_final, d // DOWN_TILE)
    return out.reshape(b, seq, d)
```
